```python
import jax, jax.numpy as jnp
from jax import lax
import numpy as np

D_MODEL = 1024
BATCH = 16
SEQ = 4096
DEPTH = 4
DEC_BATCH = 8
DEC_SEQ = 32
PAST_LEN = 2048

CHUNK = 64
Q_BLOCK = 128
HEAD_DIM = 64
RW_HEADS = 4
RW_WIDTH = RW_HEADS * HEAD_DIM
FOX_HEADS = 8
FOX_WIDTH = FOX_HEADS * HEAD_DIM
HG_HEADS = 4
HG_WIDTH = HG_HEADS * HEAD_DIM
D_MIX = RW_WIDTH + FOX_WIDTH + HG_WIDTH
RW_W_LORA = 32
RW_A_LORA = 32
RW_G_LORA = 64
RW_COLS = 3 * RW_WIDTH + RW_W_LORA + RW_A_LORA + RW_G_LORA
FOX_COLS = 3 * FOX_WIDTH + FOX_HEADS
HG_COLS = 4 * HG_WIDTH
IN_COLS = RW_COLS + FOX_COLS + HG_COLS
D_FF = -(-8 * D_MODEL // (3 * 256)) * 256
EPS = 1e-6
RW_GN_EPS = 64e-5
FOX_SCALE = HEAD_DIM ** -0.5
RW_SPLITS = [RW_WIDTH, 2 * RW_WIDTH, 3 * RW_WIDTH, 3 * RW_WIDTH + RW_W_LORA, 3 * RW_WIDTH + RW_W_LORA + RW_A_LORA]
F32 = jnp.float32

kernel_name = "hybrid_stream_rwkv7_fox_hgrn2_step"


def rmsnorm(x, g):
    xf = x.astype(F32)
    return xf * lax.rsqrt(jnp.mean(xf * xf, axis=-1, keepdims=True) + EPS) * g.astype(F32)


def rwkv7_mix(p, prev, S0, mu, w0, w2, a0, a2, g2, k_k, k_a, r_k, ln_w, ln_b):
    B, T, _ = p.shape
    p = p.astype(F32)
    p_prev = jnp.concatenate([prev.astype(F32), p[:, :-1]], axis=1)
    xs = p + mu * (p_prev - p)
    r, k, v, wl, al, gl = jnp.split(xs, RW_SPLITS, axis=-1)
    w_raw = -jax.nn.softplus(-(w0 + jnp.tanh(wl) @ w2)) - 0.5
    w = jnp.exp(-jnp.exp(w_raw))
    a = jax.nn.sigmoid(a0 + al @ a2)
    g = jax.nn.sigmoid(gl) @ g2
    hd = lambda z: z.reshape(B, T, RW_HEADS, HEAD_DIM)
    kk = hd(k * k_k)
    kk = kk / jnp.maximum(jnp.sqrt(jnp.sum(kk * kk, axis=-1, keepdims=True)), 1e-12)
    k = hd(k * (1.0 + (a - 1.0) * k_a))
    r, v, a, w = hd(r), hd(v), hd(a), hd(w)

    def step(S, inp):
        r_t, k_t, v_t, kk_t, a_t, w_t = inp
        s_kk = jnp.einsum('bhvk,bhk->bhv', S, kk_t)
        S = (S * w_t[:, :, None, :] - s_kk[..., None] * (kk_t * a_t)[:, :, None, :]
             + v_t[..., None] * k_t[:, :, None, :])
        return S, jnp.einsum('bhvk,bhk->bhv', S, r_t)

    tm = lambda z: jnp.swapaxes(z, 0, 1)
    S_T, y = lax.scan(step, S0.astype(F32), (tm(r), tm(k), tm(v), tm(kk), tm(a), tm(w)))
    y = tm(y)
    mean = jnp.mean(y, axis=-1, keepdims=True)
    var = jnp.mean(jnp.square(y - mean), axis=-1, keepdims=True)
    y = ((y - mean) * lax.rsqrt(var + RW_GN_EPS)).reshape(B, T, RW_WIDTH) * ln_w + ln_b
    bonus = jnp.sum(r * k * r_k, axis=-1, keepdims=True) * v
    y = (y + bonus.reshape(B, T, RW_WIDTH)) * g
    return y, S_T


def fox_attend(q, c_q, q_pos, k, v, c_k, k_pos):
    s = jnp.einsum('bqhd,bkhd->bhqk', q, k) * FOX_SCALE
    bias = jnp.swapaxes(c_q, 1, 2)[..., :, None] - jnp.swapaxes(c_k, 1, 2)[..., None, :]
    s = jnp.where(k_pos[None, :] <= q_pos[:, None], s + bias, -jnp.inf)
    return jnp.einsum('bhqk,bkhd->bqhd', jax.nn.softmax(s, axis=-1), v)


def fox_mix(p, b_f, cache_k, cache_v, cache_logf):
    B, T, _ = p.shape
    p = p.astype(F32)
    q, k, v, f = jnp.split(p, [FOX_WIDTH, 2 * FOX_WIDTH, 3 * FOX_WIDTH], axis=-1)
    q = q.reshape(B, T, FOX_HEADS, HEAD_DIM)
    k = k.reshape(B, T, FOX_HEADS, HEAD_DIM)
    v = v.reshape(B, T, FOX_HEADS, HEAD_DIM)
    logf = jax.nn.log_sigmoid(f + b_f)
    if cache_k is None:
        c = jnp.cumsum(logf, axis=1)
        k_pos = jnp.arange(T)

        def block(i):
            start = i * Q_BLOCK
            qb = lax.dynamic_slice_in_dim(q, start, Q_BLOCK, axis=1)
            cb = lax.dynamic_slice_in_dim(c, start, Q_BLOCK, axis=1)
            return fox_attend(qb, cb, start + jnp.arange(Q_BLOCK), k, v, c, k_pos)

        o = lax.map(block, jnp.arange(T // Q_BLOCK))
        o = jnp.swapaxes(o, 0, 1).reshape(B, T, FOX_WIDTH)
    else:
        P = cache_k.shape[1]
        k_all = jnp.concatenate([cache_k.astype(F32), k], axis=1)
        v_all = jnp.concatenate([cache_v.astype(F32), v], axis=1)
        c_all = jnp.cumsum(jnp.concatenate([cache_logf.astype(F32), logf], axis=1), axis=1)
        o = fox_attend(q, c_all[:, P:], P + jnp.arange(T), k_all, v_all, c_all, jnp.arange(P + T))
        o = o.reshape(B, T, FOX_WIDTH)
    return o, k, v, logf


def hgrn2_mix(p, lb, S0, norm_g, chunk):
    B, T, _ = p.shape
    p = p.astype(F32)
    q, fx, i, g = jnp.split(p, 4, axis=-1)
    log_f = jnp.logaddexp(jnp.log(lb), jnp.log1p(-lb) + jax.nn.log_sigmoid(fx))
    k = (1.0 - lb) * jax.nn.sigmoid(-fx)
    n = T // chunk

    def to_chunks(z):
        return z.reshape(B, n, chunk, HG_HEADS, HEAD_DIM).transpose(1, 0, 3, 2, 4)

    causal = jnp.tril(jnp.ones((chunk, chunk), dtype=bool))[:, :, None]

    def step(S, inp):
        qc, kc, vc, lfc = inp
        b = jnp.cumsum(lfc, axis=2)
        o_inter = jnp.einsum('bhtk,bhkv->bhtv', qc * jnp.exp(b), S)
        diff = b[:, :, :, None, :] - b[:, :, None, :, :]
        dec = jnp.where(causal, jnp.exp(jnp.where(causal, diff, 0.0)), 0.0)
        attn = jnp.sum(qc[:, :, :, None, :] * kc[:, :, None, :, :] * dec, axis=-1)
        o = o_inter + jnp.einsum('bhts,bhsv->bhtv', attn, vc)
        b_last = b[:, :, -1:, :]
        S = (jnp.exp(b_last)[:, :, 0, :, None] * S
             + jnp.einsum('bhsk,bhsv->bhkv', kc * jnp.exp(b_last - b), vc))
        return S, o

    S_T, o = lax.scan(step, S0.astype(F32), (to_chunks(q), to_chunks(k), to_chunks(i), to_chunks(log_f)))
    o = o.transpose(1, 0, 3, 2, 4).reshape(B, T, HG_HEADS, HEAD_DIM)
    o = o * lax.rsqrt(jnp.mean(o * o, axis=-1, keepdims=True) + EPS)
    o = o.reshape(B, T, HG_WIDTH) * norm_g * jax.nn.silu(g)
    return o, S_T


def trunk_layer(x, c, p, rw_prev, rw_S, hg_S, fox_cache, hg_chunk):
    mod = jax.nn.silu(c.astype(F32)) @ p["w_ada"] + p["b_ada"]
    sh1, sc1, ga1, sh2, sc2, ga2 = [m[:, None, :] for m in jnp.split(mod, 6, axis=-1)]
    h = rmsnorm(x, p["norm1_g"]) * (1.0 + sc1) + sh1
    proj = h @ p["w_in"]
    p_rw, p_fox, p_hg = jnp.split(proj, [RW_COLS, RW_COLS + FOX_COLS], axis=-1)
    y_rw, rw_S_new = rwkv7_mix(p_rw, rw_prev, rw_S, *p["rw"])
    y_fox, k_new, v_new, logf_new = fox_mix(p_fox, p["fox_b_f"], *fox_cache)
    y_hg, hg_S_new = hgrn2_mix(p_hg, p["hg_lb"], hg_S, p["hg_norm_g"], hg_chunk)
    mix = jnp.concatenate([y_rw, y_fox, y_hg], axis=-1) @ p["w_out"]
    x = x + (ga1 * mix).astype(x.dtype)
    h2 = rmsnorm(x, p["norm2_g"]) * (1.0 + sc2) + sh2
    gate, up = jnp.split(h2 @ p["w_ffn_in"], 2, axis=-1)
    x = x + (ga2 * ((jax.nn.silu(gate) * up) @ p["w_ffn_out"])).astype(x.dtype)
    return x, (k_new, v_new, logf_new, rw_S_new, p_rw[:, -1:], hg_S_new)


def setup_inputs(seed: int = 0) -> dict:
    key = jax.random.key(seed)
    ks = jax.random.split(key, 40)
    nrm = lambda i, shape, s: s * jax.random.normal(ks[i], shape, F32)
    return {
        "x_prompt": nrm(0, (BATCH, SEQ, D_MODEL), 1.0),
        "x_sample": nrm(1, (DEC_BATCH, DEC_SEQ, D_MODEL), 1.0),
        "c_prompt": nrm(2, (BATCH, D_MODEL), 1.0),
        "c_sample": nrm(3, (DEC_BATCH, D_MODEL), 1.0),
        "cache_fox_k": nrm(4, (DEPTH, DEC_BATCH, PAST_LEN, FOX_HEADS, HEAD_DIM), 1.0),
        "cache_fox_v": nrm(5, (DEPTH, DEC_BATCH, PAST_LEN, FOX_HEADS, HEAD_DIM), 1.0),
        "cache_fox_logf": jax.nn.log_sigmoid(2.0 + nrm(6, (DEPTH, DEC_BATCH, PAST_LEN, FOX_HEADS), 1.0)),
        "state_rwkv": nrm(7, (DEPTH, DEC_BATCH, RW_HEADS, HEAD_DIM, HEAD_DIM), 0.5),
        "state_rwkv_shift": nrm(8, (DEPTH, DEC_BATCH, 1, RW_COLS), 1.0),
        "state_hgrn": nrm(9, (DEPTH, DEC_BATCH, HG_HEADS, HEAD_DIM, HEAD_DIM), 0.5),
        "norm1_g": 1.0 + nrm(10, (DEPTH, D_MODEL), 0.1),
        "w_ada": nrm(11, (DEPTH, D_MODEL, 6 * D_MODEL), 0.5 * D_MODEL ** -0.5),
        "b_ada": nrm(12, (DEPTH, 6 * D_MODEL), 0.1),
        "w_in": nrm(13, (DEPTH, D_MODEL, IN_COLS), D_MODEL ** -0.5),
        "rw_mu": jax.random.uniform(ks[14], (DEPTH, RW_COLS), F32),
        "rw_w0": nrm(15, (DEPTH, RW_WIDTH), 0.5),
        "rw_w2": nrm(16, (DEPTH, RW_W_LORA, RW_WIDTH), RW_W_LORA ** -0.5),
        "rw_a0": nrm(17, (DEPTH, RW_WIDTH), 0.5),
        "rw_a2": nrm(18, (DEPTH, RW_A_LORA, RW_WIDTH), RW_A_LORA ** -0.5),
        "rw_g2": nrm(19, (DEPTH, RW_G_LORA, RW_WIDTH), RW_G_LORA ** -0.5),
        "rw_k_k": 0.85 + nrm(20, (DEPTH, RW_WIDTH), 0.1),
        "rw_k_a": 1.0 + nrm(21, (DEPTH, RW_WIDTH), 0.1),
        "rw_r_k": nrm(22, (DEPTH, RW_HEADS, HEAD_DIM), 0.1),
        "rw_ln_w": 1.0 + nrm(23, (DEPTH, RW_WIDTH), 0.1),
        "rw_ln_b": nrm(24, (DEPTH, RW_WIDTH), 0.01),
        "fox_b_f": 2.0 + nrm(25, (DEPTH, FOX_HEADS), 0.5),
        "hg_lb_logits": nrm(26, (DEPTH, HG_WIDTH), 0.5),
        "hg_norm_g": 1.0 + nrm(27, (DEPTH, HG_WIDTH), 0.1),
        "w_out": nrm(28, (DEPTH, D_MIX, D_MODEL), D_MIX ** -0.5),
        "norm2_g": 1.0 + nrm(29, (DEPTH, D_MODEL), 0.1),
        "w_ffn_in": nrm(30, (DEPTH, D_MODEL, 2 * D_FF), D_MODEL ** -0.5),
        "w_ffn_out": nrm(31, (DEPTH, D_FF, D_MODEL), D_FF ** -0.5),
        "final_norm_g": 1.0 + nrm(32, (D_MODEL,), 0.1),
    }


def reference(x_prompt, x_sample, c_prompt, c_sample, cache_fox_k, cache_fox_v, cache_fox_logf,
              state_rwkv, state_rwkv_shift, state_hgrn, norm1_g, w_ada, b_ada, w_in, rw_mu, rw_w0,
              rw_w2, rw_a0, rw_a2, rw_g2, rw_k_k, rw_k_a, rw_r_k, rw_ln_w, rw_ln_b, fox_b_f,
              hg_lb_logits, hg_norm_g, w_out, norm2_g, w_ffn_in, w_ffn_out, final_norm_g):
    dt = x_prompt.dtype
    lbs = jnp.cumsum(jax.nn.softmax(hg_lb_logits.astype(F32), axis=0), axis=0)
    lbs = lbs - lbs[0:1]
    bp = x_prompt.shape[0]
    rw_prev0 = jnp.zeros((bp, 1, RW_COLS), F32)
    rw_S0 = jnp.zeros((bp, RW_HEADS, HEAD_DIM, HEAD_DIM), F32)
    hg_S0 = jnp.zeros((bp, HG_HEADS, HEAD_DIM, HEAD_DIM), F32)
    xp, xs = x_prompt, x_sample
    outs_p, outs_s = [], []
    for l in range(DEPTH):
        p = {
            "norm1_g": norm1_g[l], "w_ada": w_ada[l], "b_ada": b_ada[l], "w_in": w_in[l],
            "rw": (rw_mu[l], rw_w0[l], rw_w2[l], rw_a0[l], rw_a2[l], rw_g2[l], rw_k_k[l], rw_k_a[l],
                   rw_r_k[l], rw_ln_w[l], rw_ln_b[l]),
            "fox_b_f": fox_b_f[l], "hg_lb": lbs[l], "hg_norm_g": hg_norm_g[l], "w_out": w_out[l],
            "norm2_g": norm2_g[l], "w_ffn_in": w_ffn_in[l], "w_ffn_out": w_ffn_out[l],
        }
        xp, st_p = trunk_layer(xp, c_prompt, p, rw_prev0, rw_S0, hg_S0, (None, None, None), CHUNK)
        xs, st_s = trunk_layer(xs, c_sample, p, state_rwkv_shift[l], state_rwkv[l], state_hgrn[l],
                               (cache_fox_k[l], cache_fox_v[l], cache_fox_logf[l]), xs.shape[1])
        outs_p.append(st_p)
        outs_s.append(st_s)
    stk = lambda outs, j: jnp.stack([o[j] for o in outs], axis=0).astype(dt)
    y_prompt = rmsnorm(xp, final_norm_g).astype(dt)
    y_sample = rmsnorm(xs, final_norm_g).astype(dt)
    return (y_prompt, y_sample,
            stk(outs_p, 0), stk(outs_p, 1), stk(outs_p, 2), stk(outs_p, 3), stk(outs_p, 4), stk(outs_p, 5),
            stk(outs_s, 0), stk(outs_s, 1), stk(outs_s, 2), stk(outs_s, 3), stk(outs_s, 4), stk(outs_s, 5))
```

```python
import functools

import jax
import jax.numpy as jnp
from jax import lax
from jax.experimental import pallas as pl
from jax.experimental.pallas import tpu as pltpu

F32 = jnp.float32
BF16 = jnp.bfloat16
HIGHEST = lax.Precision.HIGHEST

D_MODEL = 1024
DEPTH = 4
HEAD_DIM = 64
RW_HEADS = 4
RW_WIDTH = RW_HEADS * HEAD_DIM
FOX_HEADS = 8
FOX_WIDTH = FOX_HEADS * HEAD_DIM
HG_HEADS = 4
HG_WIDTH = HG_HEADS * HEAD_DIM
RW_W_LORA = 32
RW_A_LORA = 32
RW_G_LORA = 64
RW_LORA = RW_W_LORA + RW_A_LORA + RW_G_LORA
RW_COLS = 3 * RW_WIDTH + RW_LORA
FOX_COLS = 3 * FOX_WIDTH + FOX_HEADS
HG_COLS = 4 * HG_WIDTH
D_FF = 2816
EPS = 1e-6
RW_GN_EPS = 64e-5
FOX_SCALE = HEAD_DIM ** -0.5
HG_CHUNK = 64
RW_CHUNK = 64

VMEM_LIMIT_BYTES = 56 * 1024 * 1024

NT_DIMS = (((1,), (1,)), ((), ()))
TN_DIMS = (((0,), (0,)), ((), ()))
NN_DIMS = (((1,), (0,)), ((), ()))


def _params(*sem):
    return pltpu.CompilerParams(dimension_semantics=sem, vmem_limit_bytes=VMEM_LIMIT_BYTES)


def _const_spec(shape):
    nd = len(shape)
    return pl.BlockSpec(shape, lambda *_: (0,) * nd, pipeline_mode=pl.Buffered(1))


def _mm(a, b, dims=NN_DIMS):
    return lax.dot_general(a.astype(BF16), b.astype(BF16), dims, preferred_element_type=F32)


def _split(a):
    hi = a.astype(BF16)
    lo = (a - hi.astype(F32)).astype(BF16)
    return hi, lo


def _mm3(a, b, dims=NN_DIMS):
    ah, al = _split(a)
    bh, bl = _split(b)
    d = lambda x, y: lax.dot_general(x, y, dims, preferred_element_type=F32)
    return d(ah, bh) + (d(ah, bl) + d(al, bh))


def _head_sum(x, ones_bd):
    hi, lo = _split(x)
    d = lambda u: jnp.dot(u, ones_bd, preferred_element_type=F32)
    return d(hi) + d(lo)


def _block_ones(width, dtype):
    r = lax.broadcasted_iota(jnp.int32, (width, width), 0) // HEAD_DIM
    c = lax.broadcasted_iota(jnp.int32, (width, width), 1) // HEAD_DIM
    return (r == c).astype(dtype)


def _log_sigmoid(z):
    return jnp.minimum(z, 0.0) - jnp.log1p(jnp.exp(-jnp.abs(z)))


def _sigmoid(z):
    return 1.0 / (1.0 + jnp.exp(-z))


def _lower_tri(n, dtype):
    r = lax.broadcasted_iota(jnp.int32, (n, n), 0)
    c = lax.broadcasted_iota(jnp.int32, (n, n), 1)
    return (r >= c).astype(dtype)


def _mod_kernel(c_ref, w_ref, b_ref, o_ref):
    c = c_ref[...]
    s = c * _sigmoid(c)
    o_ref[0] = jnp.dot(s, w_ref[0], precision=HIGHEST, preferred_element_type=F32) + b_ref[0]


def _modulation(c_all, w_ada, b_ada):
    nb = c_all.shape[0]
    tn = 1536
    n_out = w_ada.shape[2]
    return pl.pallas_call(
        _mod_kernel,
        grid=(DEPTH, n_out // tn),
        in_specs=[
            pl.BlockSpec((nb, D_MODEL), lambda l, j: (0, 0)),
            pl.BlockSpec((1, D_MODEL, tn), lambda l, j: (l, 0, j)),
            pl.BlockSpec((1, 1, tn), lambda l, j: (l, 0, j)),
        ],
        out_specs=pl.BlockSpec((1, nb, tn), lambda l, j: (l, 0, j)),
        out_shape=jax.ShapeDtypeStruct((DEPTH, nb, n_out), F32),
        compiler_params=_params("arbitrary", "arbitrary"),
        name="adaln_mod",
    )(c_all, w_ada, b_ada.reshape(DEPTH, 1, n_out))


def _lbs_kernel(x_ref, o_ref):
    x = x_ref[...]
    m = jnp.max(x, axis=0, keepdims=True)
    e = jnp.exp(x - m)
    p = e / jnp.sum(e, axis=0, keepdims=True)
    acc = jnp.zeros_like(p[0:1])
    rows = []
    for l in range(DEPTH):
        acc = acc + p[l:l + 1]
        rows.append(acc)
    cs = jnp.concatenate(rows, axis=0)
    o_ref[...] = cs - cs[0:1]


def _hg_lower_bounds(logits):
    return pl.pallas_call(
        _lbs_kernel,
        out_shape=jax.ShapeDtypeStruct(logits.shape, F32),
        name="hgrn_lower_bounds",
    )(logits)


def _normproj_kernel(x_ref, sh_ref, sc_ref, g_ref, wrw_ref, wqkv_ref, wft_ref, whg_ref,
                     prw_ref, q_ref, k_ref, v_ref, ft_ref, phg_ref):
    x = x_ref[...]
    ms = jnp.mean(x * x, axis=-1, keepdims=True)
    h = x * lax.rsqrt(ms + EPS) * g_ref[...]
    h = h * (1.0 + sc_ref[0]) + sh_ref[0]
    hb = h.astype(BF16)
    prw_ref[...] = jnp.dot(hb, wrw_ref[...], preferred_element_type=F32)
    qkv = jnp.dot(hb, wqkv_ref[...], preferred_element_type=F32)
    q_ref[...] = qkv[:, :FOX_WIDTH]
    k_ref[...] = qkv[:, FOX_WIDTH:2 * FOX_WIDTH]
    v_ref[...] = qkv[:, 2 * FOX_WIDTH:]
    ft_ref[...] = lax.dot_general(wft_ref[...], hb, NT_DIMS, preferred_element_type=F32)
    phg_ref[...] = jnp.dot(hb, whg_ref[...], preferred_element_type=F32)


def _mod_spec(arr, tiles_per_batch):
    _, r, d = arr.shape
    return pl.BlockSpec((1, r, d), lambda i: (i // tiles_per_batch, 0, 0))


def _normproj(x, sh, sc, g, wts, tm, tiles_per_batch):
    n = x.shape[0]
    wrw, wqkv, wft, whg = wts
    row = lambda w: pl.BlockSpec((tm, w), lambda i: (i, 0))
    return pl.pallas_call(
        _normproj_kernel,
        grid=(n // tm,),
        in_specs=[
            row(D_MODEL), _mod_spec(sh, tiles_per_batch), _mod_spec(sc, tiles_per_batch),
            _const_spec((1, D_MODEL)),
            _const_spec(wrw.shape), _const_spec(wqkv.shape), _const_spec(wft.shape), _const_spec(whg.shape),
        ],
        out_specs=[row(RW_COLS), row(FOX_WIDTH), row(FOX_WIDTH), row(FOX_WIDTH),
                   pl.BlockSpec((FOX_HEADS, tm), lambda i: (0, i)), row(HG_COLS)],
        out_shape=[
            jax.ShapeDtypeStruct((n, RW_COLS), F32),
            jax.ShapeDtypeStruct((n, FOX_WIDTH), F32),
            jax.ShapeDtypeStruct((n, FOX_WIDTH), F32),
            jax.ShapeDtypeStruct((n, FOX_WIDTH), F32),
            jax.ShapeDtypeStruct((FOX_HEADS, n), F32),
            jax.ShapeDtypeStruct((n, HG_COLS), F32),
        ],
        compiler_params=_params("arbitrary"),
        name="normproj",
    )(x, sh, sc, g, wrw, wqkv, wft, whg)


def _cumsum_lanes(x, carry):
    length = x.shape[1]
    step = min(512, length)
    r = lax.broadcasted_iota(jnp.int32, (step, step), 0)
    c = lax.broadcasted_iota(jnp.int32, (step, step), 1)
    upper = (r <= c).astype(F32)
    outs = []
    for j in range(length // step):
        seg = x[:, j * step:(j + 1) * step]
        cs = jnp.dot(seg, upper, precision=HIGHEST, preferred_element_type=F32) + carry
        carry = cs[:, step - 1:step]
        outs.append(cs)
    return outs, carry, step


def _gate_kernel(*refs, past):
    if past:
        f_ref, bf_ref, cache_ref, logf_ref, c_ref = refs
    else:
        f_ref, bf_ref, logf_ref, c_ref = refs
    logf = _log_sigmoid(f_ref[0] + bf_ref[...])
    logf_ref[0] = logf
    carry = jnp.zeros((FOX_HEADS, 1), F32)
    if past:
        outs, carry, step = _cumsum_lanes(cache_ref[0], carry)
        for j, cs in enumerate(outs):
            c_ref[0, :, j * step:(j + 1) * step] = cs
    outs, carry, step = _cumsum_lanes(logf, carry)
    for j, cs in enumerate(outs):
        c_ref[0, :, past + j * step:past + (j + 1) * step] = cs


def _gate(f_bht, b_f, cache_logf_bhp):
    nb, _, t = f_bht.shape
    past = 0 if cache_logf_bhp is None else cache_logf_bhp.shape[2]
    blk = lambda w: pl.BlockSpec((1, FOX_HEADS, w), lambda b: (b, 0, 0))
    in_specs = [blk(t), _const_spec((FOX_HEADS, 1))]
    args = [f_bht, b_f.reshape(FOX_HEADS, 1)]
    if past:
        in_specs.append(blk(past))
        args.append(cache_logf_bhp)
    return pl.pallas_call(
        functools.partial(_gate_kernel, past=past),
        grid=(nb,),
        in_specs=in_specs,
        out_specs=[blk(t), blk(past + t)],
        out_shape=[jax.ShapeDtypeStruct((nb, FOX_HEADS, t), F32),
                   jax.ShapeDtypeStruct((nb, FOX_HEADS, past + t), F32)],
        compiler_params=_params("arbitrary"),
        name="fox_gate",
    )(*args)


def _fox_kernel(q_ref, k_ref, v_ref, c_ref, o_ref, *, tq):
    qi = pl.program_id(2)
    lane = lax.broadcasted_iota(jnp.int32, (1, 2 * HEAD_DIM), 1)
    row = lax.broadcasted_iota(jnp.int32, (tq, tq), 0)
    col = lax.broadcasted_iota(jnp.int32, (tq, tq), 1)
    q = q_ref[...] * FOX_SCALE
    outs = []
    for hh in range(2):
        qh = jnp.where(lane // HEAD_DIM == hh, q, 0.0).astype(BF16)

        def chunk(j, carry, diagonal):
            m, l, acc = carry
            start = pl.multiple_of(j * tq, tq)
            kj = k_ref[pl.ds(start, tq), :].astype(BF16)
            vj = v_ref[pl.ds(start, tq), :].astype(BF16)
            s = lax.dot_general(qh, kj, NT_DIMS, preferred_element_type=F32)
            s = s - c_ref[0, hh, :, pl.ds(start, tq)]
            if diagonal:
                s = jnp.where(col <= row, s, -jnp.inf)
            m_new = jnp.maximum(m, jnp.max(s, axis=-1, keepdims=True))
            alpha = jnp.exp(m - m_new)
            p = jnp.exp(s - m_new)
            l = alpha * l + jnp.sum(p, axis=-1, keepdims=True)
            acc = alpha * acc + jnp.dot(p.astype(BF16), vj, preferred_element_type=F32)
            return m_new, l, acc

        init = (jnp.full((tq, 1), -jnp.inf, F32), jnp.zeros((tq, 1), F32),
                jnp.zeros((tq, 2 * HEAD_DIM), F32))
        carry = lax.fori_loop(0, qi, lambda j, c: chunk(j, c, False), init)
        _, l, acc = chunk(qi, carry, True)
        outs.append(acc / l)
    o_ref[...] = jnp.where(lane < HEAD_DIM, outs[0], outs[1])


def _fox(q, k, v, c_bht, nb, t, tq):
    n = q.shape[0]
    nq = t // tq
    hw = 2 * HEAD_DIM
    return pl.pallas_call(
        functools.partial(_fox_kernel, tq=tq),
        grid=(nb, FOX_HEADS // 2, nq),
        in_specs=[
            pl.BlockSpec((tq, hw), lambda b, h, i: (b * nq + i, h)),
            pl.BlockSpec((t, hw), lambda b, h, i: (b, h)),
            pl.BlockSpec((t, hw), lambda b, h, i: (b, h)),
            pl.BlockSpec((1, 2, 1, t), lambda b, h, i: (b, h, 0, 0)),
        ],
        out_specs=pl.BlockSpec((tq, hw), lambda b, h, i: (b * nq + i, h)),
        out_shape=jax.ShapeDtypeStruct((n, FOX_WIDTH), F32),
        compiler_params=_params("arbitrary", "arbitrary", "arbitrary"),
        name="fox_attention",
    )(q, k, v, c_bht.reshape(nb, FOX_HEADS, 1, t))


def _fox_cached_kernel(q_ref, kn_ref, vn_ref, kc_ref, vc_ref, c_ref, o_ref, *, past, t):
    lane = lax.broadcasted_iota(jnp.int32, (1, 2 * HEAD_DIM), 1)
    row = lax.broadcasted_iota(jnp.int32, (t, t), 0)
    col = lax.broadcasted_iota(jnp.int32, (t, t), 1)
    q = q_ref[...] * FOX_SCALE
    kc = kc_ref[...].astype(BF16)
    vc = vc_ref[...].astype(BF16)
    kn = kn_ref[...].astype(BF16)
    vn = vn_ref[...].astype(BF16)
    outs = []
    for hh in range(2):
        qh = jnp.where(lane // HEAD_DIM == hh, q, 0.0).astype(BF16)
        s1 = lax.dot_general(qh, kc, NT_DIMS, preferred_element_type=F32)
        s1 = s1 - c_ref[0, hh, :, 0:past]
        s2 = lax.dot_general(qh, kn, NT_DIMS, preferred_element_type=F32)
        s2 = s2 - c_ref[0, hh, :, past:past + t]
        s2 = jnp.where(col <= row, s2, -jnp.inf)
        m = jnp.maximum(jnp.max(s1, axis=-1, keepdims=True), jnp.max(s2, axis=-1, keepdims=True))
        p1 = jnp.exp(s1 - m)
        p2 = jnp.exp(s2 - m)
        l = jnp.sum(p1, axis=-1, keepdims=True) + jnp.sum(p2, axis=-1, keepdims=True)
        acc = (jnp.dot(p1.astype(BF16), vc, preferred_element_type=F32)
               + jnp.dot(p2.astype(BF16), vn, preferred_element_type=F32))
        outs.append(acc / l)
    o_ref[...] = jnp.where(lane < HEAD_DIM, outs[0], outs[1])


def _fox_cached(q, k, v, cache_k, cache_v, c_bhl, nb, t):
    n = q.shape[0]
    past = cache_k.shape[0] // nb
    hw = 2 * HEAD_DIM
    new = pl.BlockSpec((t, hw), lambda b, h: (b, h))
    old = pl.BlockSpec((past, hw), lambda b, h: (b, h))
    return pl.pallas_call(
        functools.partial(_fox_cached_kernel, past=past, t=t),
        grid=(nb, FOX_HEADS // 2),
        in_specs=[new, new, new, old, old,
                  pl.BlockSpec((1, 2, 1, past + t), lambda b, h: (b, h, 0, 0))],
        out_specs=new,
        out_shape=jax.ShapeDtypeStruct((n, FOX_WIDTH), F32),
        compiler_params=_params("arbitrary", "arbitrary"),
        name="fox_attention_cached",
    )(q, k, v, cache_k, cache_v, c_bhl.reshape(nb, FOX_HEADS, 1, past + t))


def _hgrn_kernel(p_ref, lb_ref, ng_ref, s0_ref, o_ref, st_ref, st_scr, *, chunk, nchunks):
    ti = pl.program_id(1)

    @pl.when(ti == 0)
    def _():
        st_scr[...] = s0_ref[0]

    w = HG_WIDTH
    lb = lb_ref[...]
    log_lb = jnp.log(lb)
    log_1m_lb = jnp.log1p(-lb)
    ltri = _lower_tri(chunk, F32)
    ones_bd = _block_ones(w, BF16)
    same_head = _block_ones(w, F32) > 0.5
    row = lax.broadcasted_iota(jnp.int32, (chunk, 1), 0)

    def one_chunk(ci, _):
        r0 = pl.multiple_of(ci * chunk, chunk)
        rows = pl.ds(r0, chunk)
        q = p_ref[rows, 0:w]
        fx = p_ref[rows, w:2 * w]
        val = p_ref[rows, 2 * w:3 * w]
        g = p_ref[rows, 3 * w:4 * w]
        bv = log_1m_lb + _log_sigmoid(fx)
        logf = jnp.maximum(log_lb, bv) + jnp.log1p(jnp.exp(-jnp.abs(log_lb - bv)))
        kd = (1.0 - lb) * _sigmoid(-fx)
        b = jnp.dot(ltri, logf, precision=HIGHEST, preferred_element_type=F32)
        st = st_scr[...]
        o = _mm(q * jnp.exp(b), st, NT_DIMS)

        def shift_step(d, carry):
            kr, br, vr, acc = carry
            valid = row >= d
            e = jnp.exp(jnp.where(valid, b - br, 0.0))
            term = jnp.where(valid, q * kr * e, 0.0)
            a = jnp.dot(term.astype(BF16), ones_bd, preferred_element_type=F32)
            acc = acc + a * vr
            return (pltpu.roll(kr, 1, 0), pltpu.roll(br, 1, 0), pltpu.roll(vr, 1, 0), acc)

        _, _, _, o = lax.fori_loop(0, chunk, shift_step, (kd, b, val, o))

        b_last = b[chunk - 1:chunk, :]
        kt = kd * jnp.exp(b_last - b)
        upd = _mm(val, kt, TN_DIMS)
        st_scr[...] = jnp.exp(b_last) * st + jnp.where(same_head, upd, 0.0)

        ms = _head_sum(o * o, ones_bd) * (1.0 / HEAD_DIM)
        o_ref[rows, :] = o * lax.rsqrt(ms + EPS) * ng_ref[...] * (g * _sigmoid(g))
        return 0

    lax.fori_loop(0, nchunks, one_chunk, 0)

    @pl.when(ti == pl.num_programs(1) - 1)
    def _():
        st_ref[0] = st_scr[...]


def _hgrn(p_hg, lb, norm_g, st0, nb, t, chunk, tile):
    n = p_hg.shape[0]
    nt = t // tile
    w = HG_WIDTH
    return pl.pallas_call(
        functools.partial(_hgrn_kernel, chunk=chunk, nchunks=tile // chunk),
        grid=(nb, nt),
        in_specs=[
            pl.BlockSpec((tile, HG_COLS), lambda b, i: (b * nt + i, 0)),
            _const_spec((1, w)), _const_spec((1, w)),
            pl.BlockSpec((1, w, w), lambda b, i: (b, 0, 0)),
        ],
        out_specs=[pl.BlockSpec((tile, w), lambda b, i: (b * nt + i, 0)),
                   pl.BlockSpec((1, w, w), lambda b, i: (b, 0, 0))],
        out_shape=[jax.ShapeDtypeStruct((n, w), F32), jax.ShapeDtypeStruct((nb, w, w), F32)],
        scratch_shapes=[pltpu.VMEM((w, w), F32)],
        compiler_params=_params("arbitrary", "arbitrary"),
        name="hgrn2",
    )(p_hg, lb, norm_g, st0)


def _rwkv_kernel(p_ref, prev_ref, s0_ref, mu_ref, vec_ref, w2_ref, a2_ref, g2_ref,
                 y_ref, st_ref, st_scr, prev_scr, *, chunk, nchunks):
    ti = pl.program_id(1)

    @pl.when(ti == 0)
    def _():
        st_scr[...] = s0_ref[0]
        prev_scr[...] = prev_ref[0]

    w = RW_WIDTH
    hc = RW_HEADS * chunk
    w0 = vec_ref[0:1, :]
    a0 = vec_ref[1:2, :]
    k_k = vec_ref[2:3, :]
    k_a = vec_ref[3:4, :]
    r_k = vec_ref[4:5, :]
    ln_w = vec_ref[5:6, :]
    ln_b = vec_ref[6:7, :]
    ltri = _lower_tri(chunk, F32)
    ones_bd = _block_ones(w, BF16)
    row = lax.broadcasted_iota(jnp.int32, (chunk, 1), 0)
    lane_head = lax.broadcasted_iota(jnp.int32, (1, w), 1) // HEAD_DIM
    gr = lax.broadcasted_iota(jnp.int32, (hc, hc), 0)
    gc = lax.broadcasted_iota(jnp.int32, (hc, hc), 1)
    same_chunk = (gr // chunk) == (gc // chunk)
    strict = same_chunk & (gc < gr)
    incl = same_chunk & (gc <= gr)
    eye = (gr == gc).astype(F32)

    def stack(x):
        return jnp.concatenate([jnp.where(lane_head == h, x, 0.0) for h in range(RW_HEADS)], axis=0)

    def one_chunk(ci, _):
        r0 = pl.multiple_of(ci * chunk, chunk)
        rows = pl.ds(r0, chunk)
        p = p_ref[rows, :]
        p_prev = jnp.where(row == 0, prev_scr[...], pltpu.roll(p, 1, 0))
        prev_scr[...] = p[chunk - 1:chunk, :]
        xs = p + mu_ref[...] * (p_prev - p)
        r = xs[:, 0:w]
        k = xs[:, w:2 * w]
        v = xs[:, 2 * w:3 * w]
        tail = xs[:, 3 * w:]
        w_lin = w0 + _mm3(jnp.tanh(tail), w2_ref[...])
        a = _sigmoid(a0 + _mm3(tail, a2_ref[...]))
        g = _mm3(_sigmoid(tail), g2_ref[...])
        z = -w_lin
        w_raw = -(jnp.maximum(z, 0.0) + jnp.log1p(jnp.exp(-jnp.abs(z)))) - 0.5
        logw = -jnp.exp(w_raw)
        kk = k * k_k
        kk = kk / jnp.maximum(jnp.sqrt(_head_sum(kk * kk, ones_bd)), 1e-12)
        kmod = k * (1.0 + (a - 1.0) * k_a)
        beta = kk * a

        lg = jnp.dot(ltri, logw, precision=HIGHEST, preferred_element_type=F32)
        lg_end = lg[chunk - 1:chunk, :]
        a_s = stack(kk * jnp.exp(lg - logw))
        r_s = stack(r * jnp.exp(lg))
        inv_g = jnp.exp(-lg)
        b_s = stack(beta * inv_g)
        k_s = stack(kmod * inv_g)
        to_end = jnp.exp(lg_end - lg)
        bh_s = stack(beta * to_end)
        kh_s = stack(kmod * to_end)
        v_s = stack(v)

        n_b = jnp.where(strict, _mm3(a_s, b_s, NT_DIMS), 0.0)
        m_k = jnp.where(strict, _mm(a_s, k_s, NT_DIMS), 0.0)
        l_b = jnp.where(incl, _mm(r_s, b_s, NT_DIMS), 0.0)
        l_k = jnp.where(incl, _mm(r_s, k_s, NT_DIMS), 0.0)

        tinv = eye - jnp.where((gr // 2) == (gc // 2), n_b, 0.0)
        blk = 2
        while blk < chunk:
            off = jnp.where(((gr // (2 * blk)) == (gc // (2 * blk))) & ((gr // blk) != (gc // blk)), n_b, 0.0)
            tinv = tinv - _mm3(_mm3(tinv, off), tinv)
            blk *= 2

        st = st_scr[...]
        e = _mm3(tinv, _mm3(a_s, st, NT_DIMS) + _mm(m_k, v_s))
        y_s = _mm3(r_s, st, NT_DIMS) + _mm(l_k, v_s) - _mm(l_b, e)
        st_scr[...] = st * jnp.exp(lg_end) + _mm(v_s, kh_s, TN_DIMS) - _mm(e, bh_s, TN_DIMS)

        y = y_s[0:chunk]
        for h in range(1, RW_HEADS):
            y = y + y_s[h * chunk:(h + 1) * chunk]

        inv_n = 1.0 / HEAD_DIM
        mean = _head_sum(y, ones_bd) * inv_n
        yc = y - mean
        var = _head_sum(yc * yc, ones_bd) * inv_n
        yn = yc * lax.rsqrt(var + RW_GN_EPS) * ln_w + ln_b
        bonus = _head_sum(r * kmod * r_k, ones_bd) * v
        y_ref[rows, :] = (yn + bonus) * g
        return 0

    lax.fori_loop(0, nchunks, one_chunk, 0)

    @pl.when(ti == pl.num_programs(1) - 1)
    def _():
        st_ref[0] = st_scr[...]


def _rwkv(p_rw, prev, st0, mu, vecs, w2p, a2p, g2p, nb, t, chunk, tile):
    n = p_rw.shape[0]
    nt = t // tile
    w = RW_WIDTH
    return pl.pallas_call(
        functools.partial(_rwkv_kernel, chunk=chunk, nchunks=tile // chunk),
        grid=(nb, nt),
        in_specs=[
            pl.BlockSpec((tile, RW_COLS), lambda b, i: (b * nt + i, 0)),
            pl.BlockSpec((1, 1, RW_COLS), lambda b, i: (b, 0, 0)),
            pl.BlockSpec((1, w, w), lambda b, i: (b, 0, 0)),
            _const_spec((1, RW_COLS)), _const_spec((8, w)),
            _const_spec((RW_LORA, w)), _const_spec((RW_LORA, w)), _const_spec((RW_LORA, w)),
        ],
        out_specs=[pl.BlockSpec((tile, w), lambda b, i: (b * nt + i, 0)),
                   pl.BlockSpec((1, w, w), lambda b, i: (b, 0, 0))],
        out_shape=[jax.ShapeDtypeStruct((n, w), F32), jax.ShapeDtypeStruct((nb, w, w), F32)],
        scratch_shapes=[pltpu.VMEM((w, w), F32), pltpu.VMEM((1, RW_COLS), F32)],
        compiler_params=_params("arbitrary", "arbitrary"),
        name="rwkv7",
    )(p_rw, prev, st0, mu, vecs, w2p, a2p, g2p)


def _outffn_kernel(yrw_ref, yfox_ref, yhg_ref, x_ref, ga1_ref, sh2_ref, sc2_ref, ga2_ref, g2_ref,
                   worw_ref, wofox_ref, wohg_ref, wg_ref, wu_ref, wo_ref, fg_ref, o_ref, *, final_norm, ff_step):
    mix = (jnp.dot(yrw_ref[...].astype(BF16), worw_ref[...], preferred_element_type=F32)
           + jnp.dot(yfox_ref[...].astype(BF16), wofox_ref[...], preferred_element_type=F32)
           + jnp.dot(yhg_ref[...].astype(BF16), wohg_ref[...], preferred_element_type=F32))
    x1 = x_ref[...] + ga1_ref[0] * mix
    ms = jnp.mean(x1 * x1, axis=-1, keepdims=True)
    h2 = x1 * lax.rsqrt(ms + EPS) * g2_ref[...]
    h2 = (h2 * (1.0 + sc2_ref[0]) + sh2_ref[0]).astype(BF16)
    ffn = jnp.zeros_like(x1)
    for j in range(D_FF // ff_step):
        cols = slice(j * ff_step, (j + 1) * ff_step)
        gate = jnp.dot(h2, wg_ref[:, cols], preferred_element_type=F32)
        up = jnp.dot(h2, wu_ref[:, cols], preferred_element_type=F32)
        act = (gate * _sigmoid(gate) * up).astype(BF16)
        ffn = ffn + jnp.dot(act, wo_ref[cols, :], preferred_element_type=F32)
    x2 = x1 + ga2_ref[0] * ffn
    if final_norm:
        ms2 = jnp.mean(x2 * x2, axis=-1, keepdims=True)
        x2 = x2 * lax.rsqrt(ms2 + EPS) * fg_ref[...]
    o_ref[...] = x2


def _outffn(y_rw, y_fox, y_hg, x, ga1, sh2, sc2, ga2, g2, wts, final_g, final_norm, tm, tiles_per_batch):
    n = x.shape[0]
    row = lambda w: pl.BlockSpec((tm, w), lambda i: (i, 0))
    mods = [_mod_spec(m, tiles_per_batch) for m in (ga1, sh2, sc2, ga2)]
    return pl.pallas_call(
        functools.partial(_outffn_kernel, final_norm=final_norm, ff_step=D_FF // 2),
        grid=(n // tm,),
        in_specs=[row(RW_WIDTH), row(FOX_WIDTH), row(HG_WIDTH), row(D_MODEL)] + mods
        + [_const_spec((1, D_MODEL))] + [_const_spec(w.shape) for w in wts] + [_const_spec((1, D_MODEL))],
        out_specs=row(D_MODEL),
        out_shape=jax.ShapeDtypeStruct((n, D_MODEL), F32),
        compiler_params=_params("arbitrary"),
        name="outffn",
    )(y_rw, y_fox, y_hg, x, ga1, sh2, sc2, ga2, g2, *wts, final_g)


def _block_diag_t(s):
    nb, h, dk, dv = s.shape
    st = jnp.swapaxes(s, 2, 3)
    eye = jnp.eye(h, dtype=s.dtype)
    return jnp.einsum("bhvk,hg->bhvgk", st, eye).reshape(nb, h * dv, h * dk)


def _block_diag(s):
    nb, h, dv, dk = s.shape
    eye = jnp.eye(h, dtype=s.dtype)
    return jnp.einsum("bhvk,hg->bhvgk", s, eye).reshape(nb, h * dv, h * dk)


def _diag_blocks(sbd, h):
    nb, rows, cols = sbd.shape
    x = sbd.reshape(nb, h, rows // h, h, cols // h)
    return jnp.stack([x[:, i, :, i, :] for i in range(h)], axis=1)


def _layer_weights(l, w_in, rw_w2, rw_a2, rw_g2, w_out, w_ffn_in, w_ffn_out):
    wi = w_in[l]
    c0, c1 = RW_COLS, RW_COLS + 3 * FOX_WIDTH
    proj = (wi[:, :c0].astype(BF16), wi[:, c0:c1].astype(BF16),
            wi[:, c1:c1 + FOX_HEADS].T.astype(BF16), wi[:, c1 + FOX_HEADS:].astype(BF16))
    zeros = lambda r: jnp.zeros((r, RW_WIDTH), F32)
    w2p = jnp.concatenate([rw_w2[l], zeros(RW_LORA - RW_W_LORA)], axis=0)
    a2p = jnp.concatenate([zeros(RW_W_LORA), rw_a2[l], zeros(RW_G_LORA)], axis=0)
    g2p = jnp.concatenate([zeros(RW_W_LORA + RW_A_LORA), rw_g2[l]], axis=0)
    wo = w_out[l].astype(BF16)
    ffn = (wo[:RW_WIDTH], wo[RW_WIDTH:RW_WIDTH + FOX_WIDTH], wo[RW_WIDTH + FOX_WIDTH:],
           w_ffn_in[l][:, :D_FF].astype(BF16), w_ffn_in[l][:, D_FF:].astype(BF16), w_ffn_out[l].astype(BF16))
    return proj, (w2p, a2p, g2p), ffn


def _group_layer(x, mods, lw, lp, prev, rw_st0, hg_st0, cache, nb, t, cfg, final_g, final_norm):
    proj_w, lora_w, ffn_w = lw
    sh1, sc1, ga1, sh2, sc2, ga2 = mods
    tm, tpb = cfg["tm"], cfg["tiles_per_batch"]
    p_rw, q, k, v, f_t, p_hg = _normproj(x, sh1, sc1, lp["norm1_g"], proj_w, tm, tpb)
    f_bht = jnp.swapaxes(f_t.reshape(FOX_HEADS, nb, t), 0, 1)
    if cache is None:
        logf_bht, c_bhl = _gate(f_bht, lp["fox_b_f"], None)
        y_fox = _fox(q, k, v, c_bhl, nb, t, cfg["tq"])
    else:
        cache_k, cache_v, cache_logf = cache
        past = cache_k.shape[1]
        logf_bht, c_bhl = _gate(f_bht, lp["fox_b_f"], jnp.swapaxes(cache_logf, 1, 2))
        y_fox = _fox_cached(q, k, v, cache_k.reshape(nb * past, FOX_WIDTH),
                            cache_v.reshape(nb * past, FOX_WIDTH), c_bhl, nb, t)
    y_hg, hg_st = _hgrn(p_hg, lp["hg_lb"], lp["hg_norm_g"], hg_st0, nb, t, cfg["hg_chunk"], cfg["hg_tile"])
    y_rw, rw_st = _rwkv(p_rw, prev, rw_st0, lp["rw_mu"], lp["rw_vecs"], *lora_w, nb, t,
                        cfg["rw_chunk"], cfg["rw_tile"])
    x = _outffn(y_rw, y_fox, y_hg, x, ga1, sh2, sc2, ga2, lp["norm2_g"], ffn_w, final_g, final_norm, tm, tpb)
    states = (
        k.reshape(nb, t, FOX_HEADS, HEAD_DIM), v.reshape(nb, t, FOX_HEADS, HEAD_DIM),
        jnp.swapaxes(logf_bht, 1, 2),
        _diag_blocks(rw_st, RW_HEADS),
        p_rw.reshape(nb, t, RW_COLS)[:, t - 1:t, :],
        jnp.swapaxes(_diag_blocks(hg_st, HG_HEADS), 2, 3),
    )
    return x, states


def kernel(x_prompt, x_sample, c_prompt, c_sample, cache_fox_k, cache_fox_v, cache_fox_logf, state_rwkv,
           state_rwkv_shift, state_hgrn, norm1_g, w_ada, b_ada, w_in, rw_mu, rw_w0, rw_w2, rw_a0, rw_a2, rw_g2,
           rw_k_k, rw_k_a, rw_r_k, rw_ln_w, rw_ln_b, fox_b_f, hg_lb_logits, hg_norm_g, w_out, norm2_g,
           w_ffn_in, w_ffn_out, final_norm_g):
    dt = x_prompt.dtype
    bp, tp, _ = x_prompt.shape
    bs, ts, _ = x_sample.shape

    mod = _modulation(jnp.concatenate([c_prompt, c_sample], axis=0).astype(F32), w_ada, b_ada)
    lbs = _hg_lower_bounds(hg_lb_logits.astype(F32))

    cfg_p = dict(tm=512, tiles_per_batch=tp // 512, tq=512, hg_chunk=HG_CHUNK, hg_tile=512,
                 rw_chunk=RW_CHUNK, rw_tile=512)
    cfg_s = dict(tm=bs * ts, tiles_per_batch=1, hg_chunk=ts, hg_tile=ts, rw_chunk=ts, rw_tile=ts)

    xp = x_prompt.reshape(bp * tp, D_MODEL).astype(F32)
    xs = x_sample.reshape(bs * ts, D_MODEL).astype(F32)
    zero_prev = jnp.zeros((bp, 1, RW_COLS), F32)
    zero_state = jnp.zeros((bp, RW_WIDTH, RW_WIDTH), F32)
    final_g = final_norm_g.reshape(1, D_MODEL)
    row = lambda a: a.reshape(1, -1)

    outs_p, outs_s = [], []
    for l in range(DEPTH):
        lw = _layer_weights(l, w_in, rw_w2, rw_a2, rw_g2, w_out, w_ffn_in, w_ffn_out)
        zero_row = jnp.zeros((RW_WIDTH,), F32)
        lp = dict(
            norm1_g=row(norm1_g[l]), norm2_g=row(norm2_g[l]), fox_b_f=fox_b_f[l],
            hg_lb=row(lbs[l]), hg_norm_g=row(hg_norm_g[l]), rw_mu=row(rw_mu[l]),
            rw_vecs=jnp.stack([rw_w0[l], rw_a0[l], rw_k_k[l], rw_k_a[l], rw_r_k[l].reshape(-1),
                               rw_ln_w[l], rw_ln_b[l], zero_row], axis=0),
        )
        chunks_p = [m[:, None, :] for m in jnp.split(mod[l, :bp], 6, axis=-1)]
        chunks_s = [jnp.repeat(m, ts, axis=0)[None] for m in jnp.split(mod[l, bp:], 6, axis=-1)]
        last = l == DEPTH - 1
        xp, st_p = _group_layer(xp, chunks_p, lw, lp, zero_prev, zero_state, zero_state, None,
                                bp, tp, cfg_p, final_g, last)
        xs, st_s = _group_layer(xs, chunks_s, lw, lp, state_rwkv_shift[l].astype(F32),
                                _block_diag(state_rwkv[l].astype(F32)),
                                _block_diag_t(state_hgrn[l].astype(F32)),
                                (cache_fox_k[l].astype(F32), cache_fox_v[l].astype(F32),
                                 cache_fox_logf[l].astype(F32)),
                                bs, ts, cfg_s, final_g, last)
        outs_p.append(st_p)
        outs_s.append(st_s)

    stk = lambda outs, j: jnp.stack([o[j] for o in outs], axis=0).astype(dt)
    y_prompt = xp.reshape(bp, tp, D_MODEL).astype(dt)
    y_sample = xs.reshape(bs, ts, D_MODEL).astype(dt)
    return (y_prompt, y_sample,
            stk(outs_p, 0), stk(outs_p, 1), stk(outs_p, 2), stk(outs_p, 3), stk(outs_p, 4), stk(outs_p, 5),
            stk(outs_s, 0), stk(outs_s, 1), stk(outs_s, 2), stk(outs_s, 3), stk(outs_s, 4), stk(outs_s, 5))
```

```python
import functools

import numpy as np
import jax
import jax.numpy as jnp
from jax import lax
from jax.experimental import pallas as pl
from jax.experimental.pallas import tpu as pltpu

F32 = jnp.float32
BF16 = jnp.bfloat16
HIGHEST = lax.Precision.HIGHEST

D_MODEL = 1024
DEPTH = 4
HEAD_DIM = 64
RW_HEADS = 4
RW_WIDTH = RW_HEADS * HEAD_DIM
FOX_HEADS = 8
FOX_WIDTH = FOX_HEADS * HEAD_DIM
HG_HEADS = 4
HG_WIDTH = HG_HEADS * HEAD_DIM
RW_W_LORA = 32
RW_A_LORA = 32
RW_G_LORA = 64
RW_LORA = RW_W_LORA + RW_A_LORA + RW_G_LORA
RW_COLS = 3 * RW_WIDTH + RW_LORA
FOX_COLS = 3 * FOX_WIDTH + FOX_HEADS
HG_COLS = 4 * HG_WIDTH
D_FF = 2816
EPS = 1e-6
RW_GN_EPS = 64e-5
FOX_SCALE = HEAD_DIM ** -0.5
HG_CHUNK = 64
RW_CHUNK = 64

VMEM_LIMIT_BYTES = 56 * 1024 * 1024

NT_DIMS = (((1,), (1,)), ((), ()))
TN_DIMS = (((0,), (0,)), ((), ()))
NN_DIMS = (((1,), (0,)), ((), ()))


def _params(*sem):
    return pltpu.CompilerParams(dimension_semantics=sem, vmem_limit_bytes=VMEM_LIMIT_BYTES)


def _const_spec(shape):
    nd = len(shape)
    return pl.BlockSpec(shape, lambda *_: (0,) * nd, pipeline_mode=pl.Buffered(1))


def _mm(a, b, dims=NN_DIMS):
    return lax.dot_general(a.astype(BF16), b.astype(BF16), dims, preferred_element_type=F32)


def _split(a):
    hi = a.astype(BF16)
    lo = (a - hi.astype(F32)).astype(BF16)
    return hi, lo


def _mm3(a, b, dims=NN_DIMS):
    ah, al = _split(a)
    bh, bl = _split(b)
    d = lambda x, y: lax.dot_general(x, y, dims, preferred_element_type=F32)
    return d(ah, bh) + (d(ah, bl) + d(al, bh))


def _head_sum(x, ones_bd):
    hi, lo = _split(x)
    d = lambda u: jnp.dot(u, ones_bd, preferred_element_type=F32)
    return d(hi) + d(lo)


def _block_ones(width, dtype):
    r = lax.broadcasted_iota(jnp.int32, (width, width), 0) // HEAD_DIM
    c = lax.broadcasted_iota(jnp.int32, (width, width), 1) // HEAD_DIM
    return (r == c).astype(dtype)


def _log_sigmoid(z):
    return jnp.minimum(z, 0.0) - jnp.log1p(jnp.exp(-jnp.abs(z)))


def _sigmoid(z):
    return 1.0 / (1.0 + jnp.exp(-z))


def _lower_tri(n, dtype):
    r = lax.broadcasted_iota(jnp.int32, (n, n), 0)
    c = lax.broadcasted_iota(jnp.int32, (n, n), 1)
    return (r >= c).astype(dtype)


def _mod_kernel(c_ref, w_ref, b_ref, o_ref):
    c = c_ref[...]
    s = c * _sigmoid(c)
    o_ref[0] = jnp.dot(s, w_ref[0], precision=HIGHEST, preferred_element_type=F32) + b_ref[0]


def _modulation(c_all, w_ada, b_ada):
    nb = c_all.shape[0]
    tn = 1536
    n_out = w_ada.shape[2]
    return pl.pallas_call(
        _mod_kernel,
        grid=(DEPTH, n_out // tn),
        in_specs=[
            pl.BlockSpec((nb, D_MODEL), lambda l, j: (0, 0)),
            pl.BlockSpec((1, D_MODEL, tn), lambda l, j: (l, 0, j)),
            pl.BlockSpec((1, 1, tn), lambda l, j: (l, 0, j)),
        ],
        out_specs=pl.BlockSpec((1, nb, tn), lambda l, j: (l, 0, j)),
        out_shape=jax.ShapeDtypeStruct((DEPTH, nb, n_out), F32),
        compiler_params=_params("arbitrary", "arbitrary"),
        name="adaln_mod",
    )(c_all, w_ada, b_ada.reshape(DEPTH, 1, n_out))


def _lbs_kernel(x_ref, o_ref):
    x = x_ref[...]
    m = jnp.max(x, axis=0, keepdims=True)
    e = jnp.exp(x - m)
    p = e / jnp.sum(e, axis=0, keepdims=True)
    acc = jnp.zeros_like(p[0:1])
    rows = []
    for l in range(DEPTH):
        acc = acc + p[l:l + 1]
        rows.append(acc)
    cs = jnp.concatenate(rows, axis=0)
    o_ref[...] = cs - cs[0:1]


def _hg_lower_bounds(logits):
    return pl.pallas_call(
        _lbs_kernel,
        out_shape=jax.ShapeDtypeStruct(logits.shape, F32),
        name="hgrn_lower_bounds",
    )(logits)


def _normproj_kernel(x_ref, sh_ref, sc_ref, g_ref, wrw_ref, wqkv_ref, wft_ref, whg_ref,
                     prw_ref, q_ref, k_ref, v_ref, ft_ref, phg_ref):
    x = x_ref[...]
    ms = jnp.mean(x * x, axis=-1, keepdims=True)
    h = x * lax.rsqrt(ms + EPS) * g_ref[...]
    h = h * (1.0 + sc_ref[0]) + sh_ref[0]
    hb = h.astype(BF16)
    prw_ref[...] = jnp.dot(hb, wrw_ref[...], preferred_element_type=F32)
    qkv = jnp.dot(hb, wqkv_ref[...], preferred_element_type=F32)
    q_ref[...] = qkv[:, :FOX_WIDTH]
    k_ref[...] = qkv[:, FOX_WIDTH:2 * FOX_WIDTH]
    v_ref[...] = qkv[:, 2 * FOX_WIDTH:]
    ft_ref[...] = lax.dot_general(wft_ref[...], hb, NT_DIMS, preferred_element_type=F32)
    phg_ref[...] = jnp.dot(hb, whg_ref[...], preferred_element_type=F32)


def _mod_spec(arr, tiles_per_batch):
    _, r, d = arr.shape
    return pl.BlockSpec((1, r, d), lambda i: (i // tiles_per_batch, 0, 0))


def _normproj(x, sh, sc, g, wts, tm, tiles_per_batch):
    n = x.shape[0]
    wrw, wqkv, wft, whg = wts
    row = lambda w: pl.BlockSpec((tm, w), lambda i: (i, 0))
    return pl.pallas_call(
        _normproj_kernel,
        grid=(n // tm,),
        in_specs=[
            row(D_MODEL), _mod_spec(sh, tiles_per_batch), _mod_spec(sc, tiles_per_batch),
            _const_spec((1, D_MODEL)),
            _const_spec(wrw.shape), _const_spec(wqkv.shape), _const_spec(wft.shape), _const_spec(whg.shape),
        ],
        out_specs=[row(RW_COLS), row(FOX_WIDTH), row(FOX_WIDTH), row(FOX_WIDTH),
                   pl.BlockSpec((FOX_HEADS, tm), lambda i: (0, i)), row(HG_COLS)],
        out_shape=[
            jax.ShapeDtypeStruct((n, RW_COLS), F32),
            jax.ShapeDtypeStruct((n, FOX_WIDTH), F32),
            jax.ShapeDtypeStruct((n, FOX_WIDTH), F32),
            jax.ShapeDtypeStruct((n, FOX_WIDTH), F32),
            jax.ShapeDtypeStruct((FOX_HEADS, n), F32),
            jax.ShapeDtypeStruct((n, HG_COLS), F32),
        ],
        compiler_params=_params("arbitrary"),
        name="normproj",
    )(x, sh, sc, g, wrw, wqkv, wft, whg)


def _cumsum_lanes(x, carry):
    length = x.shape[1]
    step = min(512, length)
    r = lax.broadcasted_iota(jnp.int32, (step, step), 0)
    c = lax.broadcasted_iota(jnp.int32, (step, step), 1)
    upper = (r <= c).astype(F32)
    outs = []
    for j in range(length // step):
        seg = x[:, j * step:(j + 1) * step]
        cs = jnp.dot(seg, upper, precision=HIGHEST, preferred_element_type=F32) + carry
        carry = cs[:, step - 1:step]
        outs.append(cs)
    return outs, carry, step


def _gate_kernel(*refs, past):
    if past:
        f_ref, bf_ref, cache_ref, logf_ref, c_ref = refs
    else:
        f_ref, bf_ref, logf_ref, c_ref = refs
    logf = _log_sigmoid(f_ref[0] + bf_ref[...])
    logf_ref[0] = logf
    carry = jnp.zeros((FOX_HEADS, 1), F32)
    if past:
        outs, carry, step = _cumsum_lanes(cache_ref[0], carry)
        for j, cs in enumerate(outs):
            c_ref[0, :, j * step:(j + 1) * step] = cs
    outs, carry, step = _cumsum_lanes(logf, carry)
    for j, cs in enumerate(outs):
        c_ref[0, :, past + j * step:past + (j + 1) * step] = cs


def _gate(f_bht, b_f, cache_logf_bhp):
    nb, _, t = f_bht.shape
    past = 0 if cache_logf_bhp is None else cache_logf_bhp.shape[2]
    blk = lambda w: pl.BlockSpec((1, FOX_HEADS, w), lambda b: (b, 0, 0))
    in_specs = [blk(t), _const_spec((FOX_HEADS, 1))]
    args = [f_bht, b_f.reshape(FOX_HEADS, 1)]
    if past:
        in_specs.append(blk(past))
        args.append(cache_logf_bhp)
    return pl.pallas_call(
        functools.partial(_gate_kernel, past=past),
        grid=(nb,),
        in_specs=in_specs,
        out_specs=[blk(t), blk(past + t)],
        out_shape=[jax.ShapeDtypeStruct((nb, FOX_HEADS, t), F32),
                   jax.ShapeDtypeStruct((nb, FOX_HEADS, past + t), F32)],
        compiler_params=_params("arbitrary"),
        name="fox_gate",
    )(*args)


def _fox_kernel(q_ref, k_ref, v_ref, c_ref, o_ref, *, tq):
    qi = pl.program_id(2)
    lane = lax.broadcasted_iota(jnp.int32, (1, 2 * HEAD_DIM), 1)
    row = lax.broadcasted_iota(jnp.int32, (tq, tq), 0)
    col = lax.broadcasted_iota(jnp.int32, (tq, tq), 1)
    q = q_ref[...] * FOX_SCALE
    outs = []
    for hh in range(2):
        qh = jnp.where(lane // HEAD_DIM == hh, q, 0.0).astype(BF16)

        def chunk(j, carry, diagonal):
            m, l, acc = carry
            start = pl.multiple_of(j * tq, tq)
            kj = k_ref[pl.ds(start, tq), :].astype(BF16)
            vj = v_ref[pl.ds(start, tq), :].astype(BF16)
            s = lax.dot_general(qh, kj, NT_DIMS, preferred_element_type=F32)
            s = s - c_ref[0, hh, :, pl.ds(start, tq)]
            if diagonal:
                s = jnp.where(col <= row, s, -jnp.inf)
            m_new = jnp.maximum(m, jnp.max(s, axis=-1, keepdims=True))
            alpha = jnp.exp(m - m_new)
            p = jnp.exp(s - m_new)
            l = alpha * l + jnp.sum(p, axis=-1, keepdims=True)
            acc = alpha * acc + jnp.dot(p.astype(BF16), vj, preferred_element_type=F32)
            return m_new, l, acc

        init = (jnp.full((tq, 1), -jnp.inf, F32), jnp.zeros((tq, 1), F32),
                jnp.zeros((tq, 2 * HEAD_DIM), F32))
        carry = lax.fori_loop(0, qi, lambda j, c: chunk(j, c, False), init)
        _, l, acc = chunk(qi, carry, True)
        outs.append(acc / l)
    o_ref[...] = jnp.where(lane < HEAD_DIM, outs[0], outs[1])


def _fox(q, k, v, c_bht, nb, t, tq):
    n = q.shape[0]
    nq = t // tq
    hw = 2 * HEAD_DIM
    return pl.pallas_call(
        functools.partial(_fox_kernel, tq=tq),
        grid=(nb, FOX_HEADS // 2, nq),
        in_specs=[
            pl.BlockSpec((tq, hw), lambda b, h, i: (b * nq + i, h)),
            pl.BlockSpec((t, hw), lambda b, h, i: (b, h)),
            pl.BlockSpec((t, hw), lambda b, h, i: (b, h)),
            pl.BlockSpec((1, 2, 1, t), lambda b, h, i: (b, h, 0, 0)),
        ],
        out_specs=pl.BlockSpec((tq, hw), lambda b, h, i: (b * nq + i, h)),
        out_shape=jax.ShapeDtypeStruct((n, FOX_WIDTH), F32),
        compiler_params=_params("arbitrary", "arbitrary", "arbitrary"),
        name="fox_attention",
    )(q, k, v, c_bht.reshape(nb, FOX_HEADS, 1, t))


def _fox_cached_kernel(q_ref, kn_ref, vn_ref, kc_ref, vc_ref, c_ref, o_ref, *, past, t):
    lane = lax.broadcasted_iota(jnp.int32, (1, 2 * HEAD_DIM), 1)
    row = lax.broadcasted_iota(jnp.int32, (t, t), 0)
    col = lax.broadcasted_iota(jnp.int32, (t, t), 1)
    q = q_ref[...] * FOX_SCALE
    kc = kc_ref[...].astype(BF16)
    vc = vc_ref[...].astype(BF16)
    kn = kn_ref[...].astype(BF16)
    vn = vn_ref[...].astype(BF16)
    outs = []
    for hh in range(2):
        qh = jnp.where(lane // HEAD_DIM == hh, q, 0.0).astype(BF16)
        s1 = lax.dot_general(qh, kc, NT_DIMS, preferred_element_type=F32)
        s1 = s1 - c_ref[0, hh, :, 0:past]
        s2 = lax.dot_general(qh, kn, NT_DIMS, preferred_element_type=F32)
        s2 = s2 - c_ref[0, hh, :, past:past + t]
        s2 = jnp.where(col <= row, s2, -jnp.inf)
        m = jnp.maximum(jnp.max(s1, axis=-1, keepdims=True), jnp.max(s2, axis=-1, keepdims=True))
        p1 = jnp.exp(s1 - m)
        p2 = jnp.exp(s2 - m)
        l = jnp.sum(p1, axis=-1, keepdims=True) + jnp.sum(p2, axis=-1, keepdims=True)
        acc = (jnp.dot(p1.astype(BF16), vc, preferred_element_type=F32)
               + jnp.dot(p2.astype(BF16), vn, preferred_element_type=F32))
        outs.append(acc / l)
    o_ref[...] = jnp.where(lane < HEAD_DIM, outs[0], outs[1])


def _fox_cached(q, k, v, cache_k, cache_v, c_bhl, nb, t):
    n = q.shape[0]
    past = cache_k.shape[0] // nb
    hw = 2 * HEAD_DIM
    new = pl.BlockSpec((t, hw), lambda b, h: (b, h))
    old = pl.BlockSpec((past, hw), lambda b, h: (b, h))
    return pl.pallas_call(
        functools.partial(_fox_cached_kernel, past=past, t=t),
        grid=(nb, FOX_HEADS // 2),
        in_specs=[new, new, new, old, old,
                  pl.BlockSpec((1, 2, 1, past + t), lambda b, h: (b, h, 0, 0))],
        out_specs=new,
        out_shape=jax.ShapeDtypeStruct((n, FOX_WIDTH), F32),
        compiler_params=_params("arbitrary", "arbitrary"),
        name="fox_attention_cached",
    )(q, k, v, cache_k, cache_v, c_bhl.reshape(nb, FOX_HEADS, 1, past + t))


def _levels(chunk):
    out, m = [], chunk // 2
    while m >= 1:
        out.append(m)
        m //= 2
    return out


def _decay_sum_matrix(chunk):
    r = np.arange(chunk)[:, None]
    j = np.arange(chunk)[None, :]
    incl = (j <= r).astype(np.float32)
    blocks = [incl]
    for m in _levels(chunk):
        mid = (r // (2 * m)) * (2 * m) + m - 1
        blocks.append(incl - (j <= mid).astype(np.float32))
    return np.concatenate(blocks, axis=0)


def _level_ids(chunk, heads):
    levels = _levels(chunk)
    idx = np.arange(heads * chunk)
    h, t = idx // chunk, idx % chunk
    same = h[:, None] == h[None, :]
    x = t[:, None] ^ t[None, :]
    m = np.where(x > 0, 2 ** np.floor(np.log2(np.maximum(x, 1))).astype(np.int64), 0)
    lvl = np.full((heads * chunk, heads * chunk), -1, np.int32)
    for li, lm in enumerate(levels):
        lvl[same & (t[:, None] > t[None, :]) & (m == lm)] = li
    lvl[same & (x == 0)] = len(levels)
    return lvl


def _two_pass(mat_bf16, x):
    hi, lo = _split(x)
    d = lambda u: jnp.dot(mat_bf16, u, preferred_element_type=F32)
    return d(hi) + d(lo)


def _hgrn_kernel(p_ref, lb_ref, ng_ref, s0_ref, dmat_ref, lvl_ref, o_ref, st_ref, st_scr, *, chunk, nchunks):
    ti = pl.program_id(1)

    @pl.when(ti == 0)
    def _():
        st_scr[...] = s0_ref[0]

    w = HG_WIDTH
    levels = _levels(chunk)
    lb = lb_ref[...]
    log_lb = jnp.log(lb)
    log_1m_lb = jnp.log1p(-lb)
    ones_bd = _block_ones(w, BF16)
    same_head = _block_ones(w, F32) > 0.5
    row = lax.broadcasted_iota(jnp.int32, (chunk, 1), 0)
    lane_head = lax.broadcasted_iota(jnp.int32, (1, w), 1) // HEAD_DIM

    def stack(x):
        return jnp.concatenate([jnp.where(lane_head == h, x, 0.0) for h in range(HG_HEADS)], axis=0)

    def tile(x):
        return jnp.concatenate([x] * HG_HEADS, axis=0)

    def one_chunk(ci, _):
        r0 = pl.multiple_of(ci * chunk, chunk)
        rows = pl.ds(r0, chunk)
        q = p_ref[rows, 0:w]
        fx = p_ref[rows, w:2 * w]
        val = p_ref[rows, 2 * w:3 * w]
        g = p_ref[rows, 3 * w:4 * w]
        bv = log_1m_lb + _log_sigmoid(fx)
        logf = jnp.maximum(log_lb, bv) + jnp.log1p(jnp.exp(-jnp.abs(log_lb - bv)))
        kd = (1.0 - lb) * _sigmoid(-fx)
        sums = _two_pass(dmat_ref[...], logf)
        b = sums[0:chunk]
        st = st_scr[...]
        o = _mm(q * jnp.exp(b), st, NT_DIMS)

        lvl = lvl_ref[...]
        attn = jnp.where(lvl == len(levels), _mm(tile(q), stack(kd), NT_DIMS), 0.0)
        for li, m in enumerate(levels):
            wgt = jnp.exp(-jnp.abs(sums[(li + 1) * chunk:(li + 2) * chunk]))
            second = ((row // m) % 2) == 1
            ql = jnp.where(second, q * wgt, 0.0)
            kl = jnp.where(second, 0.0, kd * wgt)
            attn = jnp.where(lvl == li, _mm(tile(ql), stack(kl), NT_DIMS), attn)
        o_s = _mm(attn, stack(val))
        for h in range(HG_HEADS):
            o = o + o_s[h * chunk:(h + 1) * chunk]

        b_last = b[chunk - 1:chunk, :]
        kt = kd * jnp.exp(b_last - b)
        upd = _mm(val, kt, TN_DIMS)
        st_scr[...] = jnp.exp(b_last) * st + jnp.where(same_head, upd, 0.0)

        ms = _head_sum(o * o, ones_bd) * (1.0 / HEAD_DIM)
        o_ref[rows, :] = o * lax.rsqrt(ms + EPS) * ng_ref[...] * (g * _sigmoid(g))
        return 0

    lax.fori_loop(0, nchunks, one_chunk, 0)

    @pl.when(ti == pl.num_programs(1) - 1)
    def _():
        st_ref[0] = st_scr[...]


def _hgrn(p_hg, lb, norm_g, st0, nb, t, chunk, tile):
    n = p_hg.shape[0]
    nt = t // tile
    w = HG_WIDTH
    dmat = jnp.asarray(_decay_sum_matrix(chunk), BF16)
    lvl = jnp.asarray(_level_ids(chunk, HG_HEADS))
    return pl.pallas_call(
        functools.partial(_hgrn_kernel, chunk=chunk, nchunks=tile // chunk),
        grid=(nb, nt),
        in_specs=[
            pl.BlockSpec((tile, HG_COLS), lambda b, i: (b * nt + i, 0)),
            _const_spec((1, w)), _const_spec((1, w)),
            pl.BlockSpec((1, w, w), lambda b, i: (b, 0, 0)),
            _const_spec(dmat.shape), _const_spec(lvl.shape),
        ],
        out_specs=[pl.BlockSpec((tile, w), lambda b, i: (b * nt + i, 0)),
                   pl.BlockSpec((1, w, w), lambda b, i: (b, 0, 0))],
        out_shape=[jax.ShapeDtypeStruct((n, w), F32), jax.ShapeDtypeStruct((nb, w, w), F32)],
        scratch_shapes=[pltpu.VMEM((w, w), F32)],
        compiler_params=_params("arbitrary", "arbitrary"),
        name="hgrn2",
    )(p_hg, lb, norm_g, st0, dmat, lvl)


def _rwkv_kernel(p_ref, prev_ref, s0_ref, mu_ref, vec_ref, w2_ref, a2_ref, g2_ref,
                 y_ref, st_ref, st_scr, prev_scr, *, chunk, nchunks):
    ti = pl.program_id(1)

    @pl.when(ti == 0)
    def _():
        st_scr[...] = s0_ref[0]
        prev_scr[...] = prev_ref[0]

    w = RW_WIDTH
    hc = RW_HEADS * chunk
    w0 = vec_ref[0:1, :]
    a0 = vec_ref[1:2, :]
    k_k = vec_ref[2:3, :]
    k_a = vec_ref[3:4, :]
    r_k = vec_ref[4:5, :]
    ln_w = vec_ref[5:6, :]
    ln_b = vec_ref[6:7, :]
    ltri = _lower_tri(chunk, BF16)
    ones_bd = _block_ones(w, BF16)
    row = lax.broadcasted_iota(jnp.int32, (chunk, 1), 0)
    lane_head = lax.broadcasted_iota(jnp.int32, (1, w), 1) // HEAD_DIM
    gr =lax.broadcasted_iota(jnp.int32, (hc, hc), 0)
    gc = lax.broadcasted_iota(jnp.int32, (hc, hc), 1)
    same_chunk = (gr // chunk) == (gc // chunk)
    strict = same_chunk & (gc < gr)
    incl = same_chunk & (gc <= gr)
    eye = (gr == gc).astype(F32)

    def stack(x):
        return jnp.concatenate([jnp.where(lane_head == h, x, 0.0) for h in range(RW_HEADS)], axis=0)

    def one_chunk(ci, _):
        r0 = pl.multiple_of(ci * chunk, chunk)
        rows = pl.ds(r0, chunk)
        p = p_ref[rows, :]
        p_prev = jnp.where(row == 0, prev_scr[...], pltpu.roll(p, 1, 0))
        prev_scr[...] = p[chunk - 1:chunk, :]
        xs = p + mu_ref[...] * (p_prev - p)
        r = xs[:, 0:w]
        k = xs[:, w:2 * w]
        v = xs[:, 2 * w:3 * w]
        tail = xs[:, 3 * w:]
        w_lin = w0 + _mm3(jnp.tanh(tail), w2_ref[...])
        a = _sigmoid(a0 + _mm(tail, a2_ref[...]))
        g = _mm(_sigmoid(tail), g2_ref[...])
        z = -w_lin
        w_raw = -(jnp.maximum(z, 0.0) + jnp.log1p(jnp.exp(-jnp.abs(z)))) - 0.5
        logw = -jnp.exp(w_raw)
        kk = k * k_k
        kk = kk / jnp.maximum(jnp.sqrt(_head_sum(kk * kk, ones_bd)), 1e-12)
        kmod = k * (1.0 + (a - 1.0) * k_a)
        beta = kk * a

        lg = _two_pass(ltri, logw)
        lg_end = lg[chunk - 1:chunk, :]
        a_s = stack(kk * jnp.exp(lg - logw))
        r_s = stack(r * jnp.exp(lg))
        inv_g = jnp.exp(-lg)
        b_s = stack(beta * inv_g)
        k_s = stack(kmod * inv_g)
        to_end = jnp.exp(lg_end - lg)
        bh_s = stack(beta * to_end)
        kh_s = stack(kmod * to_end)
        v_s = stack(v)

        n_b = jnp.where(strict, _mm(a_s, b_s, NT_DIMS), 0.0).astype(BF16)
        m_k = jnp.where(strict, _mm(a_s, k_s, NT_DIMS), 0.0)
        l_b = jnp.where(incl, _mm(r_s, b_s, NT_DIMS), 0.0)
        l_k = jnp.where(incl, _mm(r_s, k_s, NT_DIMS), 0.0)

        zero = jnp.zeros_like(n_b)
        tinv = eye - jnp.where((gr // 2) == (gc // 2), n_b, zero).astype(F32)
        blk = 2
        while blk < chunk:
            off = jnp.where(((gr // (2 * blk)) == (gc // (2 * blk))) & ((gr // blk) != (gc // blk)), n_b, zero)
            tinv = tinv - _mm(_mm(tinv, off), tinv)
            blk *= 2

        st = st_scr[...]
        rhs = _mm(a_s, st, NT_DIMS) + _mm(m_k, v_s)
        e = _mm(tinv, rhs)
        e_hi, e_lo = _split(e)
        n_e = (jnp.dot(n_b, e_hi, preferred_element_type=F32) + jnp.dot(n_b, e_lo, preferred_element_type=F32))
        e = e + _mm(tinv, rhs - e - n_e)
        y_s = _mm(r_s, st, NT_DIMS) + _mm(l_k, v_s) - _mm(l_b, e)
        st_scr[...] = st * jnp.exp(lg_end) + _mm(v_s, kh_s, TN_DIMS) - _mm(e, bh_s, TN_DIMS)

        y = y_s[0:chunk]
        for h in range(1, RW_HEADS):
            y = y + y_s[h * chunk:(h + 1) * chunk]

        inv_n = 1.0 / HEAD_DIM
        mean = _head_sum(y, ones_bd) * inv_n
        yc = y - mean
        var = _head_sum(yc * yc, ones_bd) * inv_n
        yn = yc * lax.rsqrt(var + RW_GN_EPS) * ln_w + ln_b
        bonus = _head_sum(r * kmod * r_k, ones_bd) * v
        y_ref[rows, :] = (yn + bonus) * g
        return 0

    lax.fori_loop(0, nchunks, one_chunk, 0)

    @pl.when(ti == pl.num_programs(1) - 1)
    def _():
        st_ref[0] = st_scr[...]


def _rwkv(p_rw, prev, st0, mu, vecs, w2p, a2p, g2p, nb, t, chunk, tile):
    n = p_rw.shape[0]
    nt = t // tile
    w = RW_WIDTH
    return pl.pallas_call(
        functools.partial(_rwkv_kernel, chunk=chunk, nchunks=tile // chunk),
        grid=(nb, nt),
        in_specs=[
            pl.BlockSpec((tile, RW_COLS), lambda b, i: (b * nt + i, 0)),
            pl.BlockSpec((1, 1, RW_COLS), lambda b, i: (b, 0, 0)),
            pl.BlockSpec((1, w, w), lambda b, i: (b, 0, 0)),
            _const_spec((1, RW_COLS)), _const_spec((8, w)),
            _const_spec((RW_LORA, w)), _const_spec((RW_LORA, w)), _const_spec((RW_LORA, w)),
        ],
        out_specs=[pl.BlockSpec((tile, w), lambda b, i: (b * nt + i, 0)),
                   pl.BlockSpec((1, w, w), lambda b, i: (b, 0, 0))],
        out_shape=[jax.ShapeDtypeStruct((n, w), F32), jax.ShapeDtypeStruct((nb, w, w), F32)],
        scratch_shapes=[pltpu.VMEM((w, w), F32), pltpu.VMEM((1, RW_COLS), F32)],
        compiler_params=_params("arbitrary", "arbitrary"),
        name="rwkv7",
    )(p_rw, prev, st0, mu, vecs, w2p, a2p, g2p)


def _outffn_kernel(yrw_ref, yfox_ref, yhg_ref, x_ref, ga1_ref, sh2_ref, sc2_ref, ga2_ref, g2_ref,
                   worw_ref, wofox_ref, wohg_ref, wg_ref, wu_ref, wo_ref, fg_ref, o_ref, *, final_norm, ff_step):
    mix = (jnp.dot(yrw_ref[...].astype(BF16), worw_ref[...], preferred_element_type=F32)
           + jnp.dot(yfox_ref[...].astype(BF16), wofox_ref[...], preferred_element_type=F32)
           + jnp.dot(yhg_ref[...].astype(BF16), wohg_ref[...], preferred_element_type=F32))
    x1 = x_ref[...] + ga1_ref[0] * mix
    ms = jnp.mean(x1 * x1, axis=-1, keepdims=True)
    h2 = x1 * lax.rsqrt(ms + EPS) * g2_ref[...]
    h2 = (h2 * (1.0 + sc2_ref[0]) + sh2_ref[0]).astype(BF16)
    ffn = jnp.zeros_like(x1)
    for j in range(D_FF // ff_step):
        cols = slice(j * ff_step, (j + 1) * ff_step)
        gate = jnp.dot(h2, wg_ref[:, cols], preferred_element_type=F32)
        up = jnp.dot(h2, wu_ref[:, cols], preferred_element_type=F32)
        act = (gate * _sigmoid(gate) * up).astype(BF16)
        ffn = ffn + jnp.dot(act, wo_ref[cols, :], preferred_element_type=F32)
    x2 = x1 + ga2_ref[0] * ffn
    if final_norm:
        ms2 = jnp.mean(x2 * x2, axis=-1, keepdims=True)
        x2 = x2 * lax.rsqrt(ms2 + EPS) * fg_ref[...]
    o_ref[...] = x2


def _outffn(y_rw, y_fox, y_hg, x, ga1, sh2, sc2, ga2, g2, wts, final_g, final_norm, tm, tiles_per_batch):
    n = x.shape[0]
    row = lambda w: pl.BlockSpec((tm, w), lambda i: (i, 0))
    mods = [_mod_spec(m, tiles_per_batch) for m in (ga1, sh2, sc2, ga2)]
    return pl.pallas_call(
        functools.partial(_outffn_kernel, final_norm=final_norm, ff_step=D_FF // 2),
        grid=(n // tm,),
        in_specs=[row(RW_WIDTH), row(FOX_WIDTH), row(HG_WIDTH), row(D_MODEL)] + mods
        + [_const_spec((1, D_MODEL))] + [_const_spec(w.shape) for w in wts] + [_const_spec((1, D_MODEL))],
        out_specs=row(D_MODEL),
        out_shape=jax.ShapeDtypeStruct((n, D_MODEL), F32),
        compiler_params=_params("arbitrary"),
        name="outffn",
    )(y_rw, y_fox, y_hg, x, ga1, sh2, sc2, ga2, g2, *wts, final_g)


def _block_diag_t(s):
    nb, h, dk, dv = s.shape
    st = jnp.swapaxes(s, 2, 3)
    eye = jnp.eye(h, dtype=s.dtype)
    return jnp.einsum("bhvk,hg->bhvgk", st, eye).reshape(nb, h * dv, h * dk)


def _block_diag(s):
    nb, h, dv, dk = s.shape
    eye = jnp.eye(h, dtype=s.dtype)
    return jnp.einsum("bhvk,hg->bhvgk", s, eye).reshape(nb, h * dv, h * dk)


def _diag_blocks(sbd, h):
    nb, rows, cols = sbd.shape
    x = sbd.reshape(nb, h, rows // h, h, cols // h)
    return jnp.stack([x[:, i, :, i, :] for i in range(h)], axis=1)


def _layer_weights(l, w_in, rw_w2, rw_a2, rw_g2, w_out, w_ffn_in, w_ffn_out):
    wi = w_in[l]
    c0, c1 = RW_COLS, RW_COLS + 3 * FOX_WIDTH
    proj = (wi[:, :c0].astype(BF16), wi[:, c0:c1].astype(BF16),
            wi[:, c1:c1 + FOX_HEADS].T.astype(BF16), wi[:, c1 + FOX_HEADS:].astype(BF16))
    zeros = lambda r: jnp.zeros((r, RW_WIDTH), F32)
    w2p = jnp.concatenate([rw_w2[l], zeros(RW_LORA - RW_W_LORA)], axis=0)
    a2p = jnp.concatenate([zeros(RW_W_LORA), rw_a2[l], zeros(RW_G_LORA)], axis=0)
    g2p = jnp.concatenate([zeros(RW_W_LORA + RW_A_LORA), rw_g2[l]], axis=0)
    wo = w_out[l].astype(BF16)
    ffn = (wo[:RW_WIDTH], wo[RW_WIDTH:RW_WIDTH + FOX_WIDTH], wo[RW_WIDTH + FOX_WIDTH:],
           w_ffn_in[l][:, :D_FF].astype(BF16), w_ffn_in[l][:, D_FF:].astype(BF16), w_ffn_out[l].astype(BF16))
    return proj, (w2p, a2p, g2p), ffn


def _group_layer(x, mods, lw, lp, prev, rw_st0, hg_st0, cache, nb, t, cfg, final_g, final_norm):
    proj_w, lora_w, ffn_w = lw
    sh1, sc1, ga1, sh2, sc2, ga2 = mods
    tm, tpb = cfg["tm"], cfg["tiles_per_batch"]
    p_rw, q, k, v, f_t, p_hg = _normproj(x, sh1, sc1, lp["norm1_g"], proj_w, tm, tpb)
    f_bht = jnp.swapaxes(f_t.reshape(FOX_HEADS, nb, t), 0, 1)
    if cache is None:
        logf_bht, c_bhl = _gate(f_bht, lp["fox_b_f"], None)
        y_fox = _fox(q, k, v, c_bhl, nb, t, cfg["tq"])
    else:
        cache_k, cache_v, cache_logf = cache
        past = cache_k.shape[1]
        logf_bht, c_bhl = _gate(f_bht, lp["fox_b_f"], jnp.swapaxes(cache_logf, 1, 2))
        y_fox = _fox_cached(q, k, v, cache_k.reshape(nb * past, FOX_WIDTH),
                            cache_v.reshape(nb * past, FOX_WIDTH), c_bhl, nb, t)
    y_hg, hg_st = _hgrn(p_hg, lp["hg_lb"], lp["hg_norm_g"], hg_st0, nb, t, cfg["hg_chunk"], cfg["hg_tile"])
    y_rw, rw_st = _rwkv(p_rw, prev, rw_st0, lp["rw_mu"], lp["rw_vecs"], *lora_w, nb, t,
                        cfg["rw_chunk"], cfg["rw_tile"])
    x = _outffn(y_rw, y_fox, y_hg, x, ga1, sh2, sc2, ga2, lp["norm2_g"], ffn_w, final_g, final_norm, tm, tpb)
    states = (
        k.reshape(nb, t, FOX_HEADS, HEAD_DIM), v.reshape(nb, t, FOX_HEADS, HEAD_DIM),
        jnp.swapaxes(logf_bht, 1, 2),
        _diag_blocks(rw_st, RW_HEADS),
        p_rw.reshape(nb, t, RW_COLS)[:, t - 1:t, :],
        jnp.swapaxes(_diag_blocks(hg_st, HG_HEADS), 2, 3),
    )
    return x, states


def kernel(x_prompt, x_sample, c_prompt, c_sample, cache_fox_k, cache_fox_v, cache_fox_logf, state_rwkv,
           state_rwkv_shift, state_hgrn, norm1_g, w_ada, b_ada, w_in, rw_mu, rw_w0, rw_w2, rw_a0, rw_a2, rw_g2,
           rw_k_k, rw_k_a, rw_r_k, rw_ln_w, rw_ln_b, fox_b_f, hg_lb_logits, hg_norm_g, w_out, norm2_g,
           w_ffn_in, w_ffn_out, final_norm_g):
    dt = x_prompt.dtype
    bp, tp, _ = x_prompt.shape
    bs, ts, _ = x_sample.shape

    mod = _modulation(jnp.concatenate([c_prompt, c_sample], axis=0).astype(F32), w_ada, b_ada)
    lbs = _hg_lower_bounds(hg_lb_logits.astype(F32))

    cfg_p = dict(tm=512, tiles_per_batch=tp // 512, tq=512, hg_chunk=HG_CHUNK, hg_tile=512,
                 rw_chunk=RW_CHUNK, rw_tile=512)
    cfg_s = dict(tm=bs * ts, tiles_per_batch=1, hg_chunk=ts, hg_tile=ts, rw_chunk=ts, rw_tile=ts)

    xp = x_prompt.reshape(bp * tp, D_MODEL).astype(F32)
    xs = x_sample.reshape(bs * ts, D_MODEL).astype(F32)
    zero_prev = jnp.zeros((bp, 1, RW_COLS), F32)
    zero_state = jnp.zeros((bp, RW_WIDTH, RW_WIDTH), F32)
    final_g = final_norm_g.reshape(1, D_MODEL)
    row = lambda a: a.reshape(1, -1)

    outs_p, outs_s = [], []
    for l in range(DEPTH):
        lw = _layer_weights(l, w_in, rw_w2, rw_a2, rw_g2, w_out, w_ffn_in, w_ffn_out)
        zero_row = jnp.zeros((RW_WIDTH,), F32)
        lp = dict(
            norm1_g=row(norm1_g[l]), norm2_g=row(norm2_g[l]), fox_b_f=fox_b_f[l],
            hg_lb=row(lbs[l]), hg_norm_g=row(hg_norm_g[l]), rw_mu=row(rw_mu[l]),
            rw_vecs=jnp.stack([rw_w0[l], rw_a0[l], rw_k_k[l], rw_k_a[l], rw_r_k[l].reshape(-1),
                               rw_ln_w[l], rw_ln_b[l], zero_row], axis=0),
        )
        chunks_p = [m[:, None, :] for m in jnp.split(mod[l, :bp], 6, axis=-1)]
        chunks_s = [jnp.repeat(m, ts, axis=0)[None] for m in jnp.split(mod[l, bp:], 6, axis=-1)]
        last = l == DEPTH - 1
        xp, st_p = _group_layer(xp, chunks_p, lw, lp, zero_prev, zero_state, zero_state, None,
                                bp, tp, cfg_p, final_g, last)
        xs, st_s = _group_layer(xs, chunks_s, lw, lp, state_rwkv_shift[l].astype(F32),
                                _block_diag(state_rwkv[l].astype(F32)),
                                _block_diag_t(state_hgrn[l].astype(F32)),
                                (cache_fox_k[l].astype(F32), cache_fox_v[l].astype(F32),
                                 cache_fox_logf[l].astype(F32)),
                                bs, ts, cfg_s, final_g, last)
        outs_p.append(st_p)
        outs_s.append(st_s)

    stk = lambda outs, j: jnp.stack([o[j] for o in outs], axis=0).astype(dt)
    y_prompt = xp.reshape(bp, tp, D_MODEL).astype(dt)
    y_sample = xs.reshape(bs, ts, D_MODEL).astype(dt)
    return (y_prompt, y_sample,
            stk(outs_p, 0), stk(outs_p, 1), stk(outs_p, 2), stk(outs_p, 3), stk(outs_p, 4), stk(outs_p, 5),
            stk(outs_s, 0), stk(outs_s, 1), stk(outs_s, 2), stk(outs_s, 3), stk(outs_s, 4), stk(outs_s, 5))
```

```python
import functools

import numpy as np
import jax
import jax.numpy as jnp
from jax import lax
from jax.experimental import pallas as pl
from jax.experimental.pallas import tpu as pltpu

F32 = jnp.float32
BF16 = jnp.bfloat16
HIGHEST = lax.Precision.HIGHEST

D_MODEL = 1024
DEPTH = 4
HEAD_DIM = 64
RW_HEADS = 4
RW_WIDTH = RW_HEADS * HEAD_DIM
FOX_HEADS = 8
FOX_WIDTH = FOX_HEADS * HEAD_DIM
HG_HEADS = 4
HG_WIDTH = HG_HEADS * HEAD_DIM
RW_W_LORA = 32
RW_A_LORA = 32
RW_G_LORA = 64
RW_LORA = RW_W_LORA + RW_A_LORA + RW_G_LORA
RW_COLS = 3 * RW_WIDTH + RW_LORA
FOX_COLS = 3 * FOX_WIDTH + FOX_HEADS
HG_COLS = 4 * HG_WIDTH
D_FF = 2816
EPS = 1e-6
RW_GN_EPS = 64e-5
FOX_SCALE = HEAD_DIM ** -0.5
HG_CHUNK = 64
RW_CHUNK = 64

VMEM_LIMIT_BYTES = 56 * 1024 * 1024

NT_DIMS = (((1,), (1,)), ((), ()))
TN_DIMS = (((0,), (0,)), ((), ()))
NN_DIMS = (((1,), (0,)), ((), ()))


def _params(*sem):
    return pltpu.CompilerParams(dimension_semantics=sem, vmem_limit_bytes=VMEM_LIMIT_BYTES)


def _const_spec(shape):
    nd = len(shape)
    return pl.BlockSpec(shape, lambda *_: (0,) * nd, pipeline_mode=pl.Buffered(1))


def _mm(a, b, dims=NN_DIMS):
    return lax.dot_general(a.astype(BF16), b.astype(BF16), dims, preferred_element_type=F32)


def _split(a):
    hi = a.astype(BF16)
    lo = (a - hi.astype(F32)).astype(BF16)
    return hi, lo


def _mm3(a, b, dims=NN_DIMS):
    ah, al = _split(a)
    bh, bl = _split(b)
    d = lambda x, y: lax.dot_general(x, y, dims, preferred_element_type=F32)
    return d(ah, bh) + (d(ah, bl) + d(al, bh))


def _head_sum(x, ones_bd):
    hi, lo = _split(x)
    d = lambda u: jnp.dot(u, ones_bd, preferred_element_type=F32)
    return d(hi) + d(lo)


def _block_ones_np(width):
    r = np.arange(width) // HEAD_DIM
    return (r[:, None] == r[None, :]).astype(np.float32)


def _log_sigmoid(z):
    return jnp.minimum(z, 0.0) - jnp.log1p(jnp.exp(-jnp.abs(z)))


def _sigmoid(z):
    return 1.0 / (1.0 + jnp.exp(-z))


def _mod_kernel(c_ref, w_ref, b_ref, o_ref):
    c = c_ref[...]
    s = c * _sigmoid(c)
    o_ref[0] = jnp.dot(s, w_ref[0], precision=HIGHEST, preferred_element_type=F32) + b_ref[0]


def _modulation(c_all, w_ada, b_ada):
    nb = c_all.shape[0]
    tn = 1536
    n_out = w_ada.shape[2]
    return pl.pallas_call(
        _mod_kernel,
        grid=(DEPTH, n_out // tn),
        in_specs=[
            pl.BlockSpec((nb, D_MODEL), lambda l, j: (0, 0)),
            pl.BlockSpec((1, D_MODEL, tn), lambda l, j: (l, 0, j)),
            pl.BlockSpec((1, 1, tn), lambda l, j: (l, 0, j)),
        ],
        out_specs=pl.BlockSpec((1, nb, tn), lambda l, j: (l, 0, j)),
        out_shape=jax.ShapeDtypeStruct((DEPTH, nb, n_out), F32),
        compiler_params=_params("arbitrary", "arbitrary"),
        name="adaln_mod",
    )(c_all, w_ada, b_ada.reshape(DEPTH, 1, n_out))


def _lbs_kernel(x_ref, o_ref):
    x = x_ref[...]
    m = jnp.max(x, axis=0, keepdims=True)
    e = jnp.exp(x - m)
    p = e / jnp.sum(e, axis=0, keepdims=True)
    acc = jnp.zeros_like(p[0:1])
    rows = []
    for l in range(DEPTH):
        acc = acc + p[l:l + 1]
        rows.append(acc)
    cs = jnp.concatenate(rows, axis=0)
    o_ref[...] = cs - cs[0:1]


def _hg_lower_bounds(logits):
    return pl.pallas_call(
        _lbs_kernel,
        out_shape=jax.ShapeDtypeStruct(logits.shape, F32),
        name="hgrn_lower_bounds",
    )(logits)


def _normproj_kernel(x_ref, sh_ref, sc_ref, g_ref, wrw_ref, wqkv_ref, wkt_ref, wft_ref, whg_ref, kin_ref, vin_ref,
                     prw_ref, q_ref, k_ref, v_ref, kt_ref, ft_ref, phg_ref, *, emit_kt):
    del kin_ref, vin_ref
    x = x_ref[...]
    ms = jnp.mean(x * x, axis=-1, keepdims=True)
    h = x * lax.rsqrt(ms + EPS) * g_ref[...]
    h = h * (1.0 + sc_ref[0]) + sh_ref[0]
    hb = h.astype(BF16)
    prw_ref[...] = jnp.dot(hb, wrw_ref[...], preferred_element_type=F32)
    qkv = jnp.dot(hb, wqkv_ref[...], preferred_element_type=F32)
    q_ref[...] = qkv[:, :FOX_WIDTH]
    k_ref[...] = qkv[:, FOX_WIDTH:2 * FOX_WIDTH]
    v_ref[...] = qkv[:, 2 * FOX_WIDTH:]
    if emit_kt:
        kt_ref[...] = lax.dot_general(wkt_ref[...], hb, NT_DIMS, preferred_element_type=F32).astype(BF16)
    else:
        kt_ref[...] = jnp.zeros(kt_ref.shape, BF16)
    ft_ref[...] = lax.dot_general(wft_ref[...], hb, NT_DIMS, preferred_element_type=F32)
    phg_ref[...] = jnp.dot(hb, whg_ref[...], preferred_element_type=F32)


def _mod_spec(arr, tiles_per_batch):
    _, r, d = arr.shape
    return pl.BlockSpec((1, r, d), lambda i: (i // tiles_per_batch, 0, 0))


def _normproj(x, sh, sc, g, wts, kv_stack, layer, tm, tiles_per_batch, emit_kt):
    n = x.shape[0]
    wrw, wqkv, wkt, wft, whg = wts
    row = lambda w: pl.BlockSpec((tm, w), lambda i: (i, 0))
    stack_spec = pl.BlockSpec((None, tm, FOX_WIDTH), lambda i: (layer, i, 0))
    stack_shape = jax.ShapeDtypeStruct((DEPTH, n, FOX_WIDTH), F32)
    kt_shape = (FOX_WIDTH, n) if emit_kt else (8, 128)
    kt_spec = (pl.BlockSpec((FOX_WIDTH, tm), lambda i: (0, i)) if emit_kt
               else pl.BlockSpec((8, 128), lambda i: (0, 0)))
    any_spec = pl.BlockSpec(memory_space=pl.ANY)
    return pl.pallas_call(
        functools.partial(_normproj_kernel, emit_kt=emit_kt),
        grid=(n // tm,),
        in_specs=[
            row(D_MODEL), _mod_spec(sh, tiles_per_batch), _mod_spec(sc, tiles_per_batch),
            _const_spec((1, D_MODEL)),
            _const_spec(wrw.shape), _const_spec(wqkv.shape), _const_spec(wkt.shape), _const_spec(wft.shape),
            _const_spec(whg.shape), any_spec, any_spec,
        ],
        out_specs=[row(RW_COLS), row(FOX_WIDTH), stack_spec, stack_spec, kt_spec,
                   pl.BlockSpec((FOX_HEADS, tm), lambda i: (0, i)), row(HG_COLS)],
        out_shape=[
            jax.ShapeDtypeStruct((n, RW_COLS), F32),
            jax.ShapeDtypeStruct((n, FOX_WIDTH), F32),
            stack_shape, stack_shape,
            jax.ShapeDtypeStruct(kt_shape, BF16),
            jax.ShapeDtypeStruct((FOX_HEADS, n), F32),
            jax.ShapeDtypeStruct((n, HG_COLS), F32),
        ],
        input_output_aliases={9: 2, 10: 3},
        compiler_params=_params("arbitrary"),
        name="normproj",
    )(x, sh, sc, g, wrw, wqkv, wkt, wft, whg, *kv_stack)


def _cumsum_lanes(x, carry):
    length = x.shape[1]
    step = min(512, length)
    r = lax.broadcasted_iota(jnp.int32, (step, step), 0)
    c = lax.broadcasted_iota(jnp.int32, (step, step), 1)
    upper = (r <= c).astype(F32)
    outs = []
    for j in range(length // step):
        seg = x[:, j * step:(j + 1) * step]
        cs = jnp.dot(seg, upper, precision=HIGHEST, preferred_element_type=F32) + carry
        carry = cs[:, step - 1:step]
        outs.append(cs)
    return outs, carry, step


def _gate_kernel(*refs, past):
    if past:
        f_ref, bf_ref, cache_ref, logf_ref, c_ref = refs
    else:
        f_ref, bf_ref, logf_ref, c_ref = refs
    logf = _log_sigmoid(f_ref[0] + bf_ref[...])
    logf_ref[0] = logf
    carry = jnp.zeros((FOX_HEADS, 1), F32)
    if past:
        outs, carry, step = _cumsum_lanes(cache_ref[0], carry)
        for j, cs in enumerate(outs):
            c_ref[0, :, j * step:(j + 1) * step] = cs
    outs, carry, step = _cumsum_lanes(logf, carry)
    for j, cs in enumerate(outs):
        c_ref[0, :, past + j * step:past + (j + 1) * step] = cs


def _gate(f_bht, b_f, cache_logf_bhp):
    nb, _, t = f_bht.shape
    past = 0 if cache_logf_bhp is None else cache_logf_bhp.shape[2]
    blk = lambda w: pl.BlockSpec((1, FOX_HEADS, w), lambda b: (b, 0, 0))
    in_specs = [blk(t), _const_spec((FOX_HEADS, 1))]
    args = [f_bht, b_f.reshape(FOX_HEADS, 1)]
    if past:
        in_specs.append(blk(past))
        args.append(cache_logf_bhp)
    return pl.pallas_call(
        functools.partial(_gate_kernel, past=past),
        grid=(nb,),
        in_specs=in_specs,
        out_specs=[blk(t), blk(past + t)],
        out_shape=[jax.ShapeDtypeStruct((nb, FOX_HEADS, t), F32),
                   jax.ShapeDtypeStruct((nb, FOX_HEADS, past + t), F32)],
        compiler_params=_params("arbitrary"),
        name="fox_gate",
    )(*args)


LOG2E = 1.4426950408889634
BIAS_ROWS = 16


def _fox_kernel(q_ref, kt_ref, v_ref, c_ref, o_ref, kaug_scr, vaug_scr, *, tq):
    qi = pl.program_id(2)
    hd = HEAD_DIM
    lane = lax.broadcasted_iota(jnp.int32, (1, 2 * hd), 1)
    bias_lane0 = (hd, 0)
    ones_lane = (hd, 0)

    @pl.when(qi == 0)
    def _():
        kt = kt_ref[...]
        v = v_ref[...]
        t = kt.shape[1]
        for hh in range(2):
            c2 = c_ref[0, hh] * LOG2E
            hi = c2.astype(BF16).astype(F32)
            mid = (c2 - hi).astype(BF16).astype(F32)
            lo = c2 - hi - mid
            bias = jnp.concatenate([-hi, -mid, -lo, jnp.zeros((BIAS_ROWS - 3, t), F32)], axis=0).astype(BF16)
            pad = jnp.zeros((hd - BIAS_ROWS, t), BF16)
            if hh == 0:
                kaug_scr[hh] = jnp.concatenate([kt[0:hd], bias, pad], axis=0)
            else:
                kaug_scr[hh] = jnp.concatenate([bias, pad, kt[hd:2 * hd]], axis=0)
            own = (lane // hd) == hh
            vaug_scr[hh] = jnp.where(own, v, jnp.where(lane == ones_lane[hh], 1.0, 0.0)).astype(BF16)

    row = lax.broadcasted_iota(jnp.int32, (tq, tq), 0)
    col = lax.broadcasted_iota(jnp.int32, (tq, tq), 1)
    q = q_ref[...] * (FOX_SCALE * LOG2E)
    qa = []
    for hh in range(2):
        own = (lane // hd) == hh
        is_bias = (lane >= bias_lane0[hh]) & (lane < bias_lane0[hh] + 3)
        qa.append(jnp.where(own, q, jnp.where(is_bias, 1.0, 0.0)).astype(BF16))

    def chunk(j, carry, diagonal):
        start = pl.multiple_of(j * tq, tq)
        scores = [jnp.dot(qa[hh], kaug_scr[hh, :, pl.ds(start, tq)], preferred_element_type=F32)
                  for hh in range(2)]
        out = []
        for hh in range(2):
            m, acc = carry[hh]
            s = scores[hh]
            if diagonal:
                s = jnp.where(col <= row, s, -jnp.inf)
            m_new = jnp.maximum(m, jnp.max(s, axis=-1, keepdims=True))
            alpha = jnp.exp2(m - m_new)
            p = jnp.exp2(s - m_new).astype(BF16)
            acc = alpha * acc + jnp.dot(p, vaug_scr[hh, pl.ds(start, tq), :], preferred_element_type=F32)
            out.append((m_new, acc))
        return tuple(out)

    init = tuple((jnp.full((tq, 1), -jnp.inf, F32), jnp.zeros((tq, 2 * hd), F32)) for _ in range(2))
    carry = lax.fori_loop(0, qi, lambda j, c: chunk(j, c, False), init)
    (_, acc0), (_, acc1) = chunk(qi, carry, True)
    out0 = acc0 / acc0[:, ones_lane[0]:ones_lane[0] + 1]
    out1 = acc1 / acc1[:, ones_lane[1]:ones_lane[1] + 1]
    o_ref[...] = jnp.where(lane < hd, out0, out1)


def _fox(q, kt, v_stack, layer, c_bht, nb, t, tq):
    n = q.shape[0]
    nq = t // tq
    hw = 2 * HEAD_DIM
    return pl.pallas_call(
        functools.partial(_fox_kernel, tq=tq),
        grid=(nb, FOX_HEADS // 2, nq),
        in_specs=[
            pl.BlockSpec((tq, hw), lambda b, h, i: (b * nq + i, h)),
            pl.BlockSpec((hw, t), lambda b, h, i: (h, b)),
            pl.BlockSpec((None, t, hw), lambda b, h, i: (layer, b, h)),
            pl.BlockSpec((1, 2, 1, t), lambda b, h, i: (b, h, 0, 0)),
        ],
        out_specs=pl.BlockSpec((tq, hw), lambda b, h, i: (b * nq + i, h)),
        out_shape=jax.ShapeDtypeStruct((n, FOX_WIDTH), F32),
        scratch_shapes=[pltpu.VMEM((2, hw, t), BF16), pltpu.VMEM((2, t, hw), BF16)],
        compiler_params=_params("arbitrary", "arbitrary", "arbitrary"),
        name="fox_attention",
    )(q, kt, v_stack, c_bht.reshape(nb, FOX_HEADS, 1, t))


def _fox_cached_kernel(q_ref, kn_ref, vn_ref, kc_ref, vc_ref, c_ref, o_ref, *, past, t):
    lane = lax.broadcasted_iota(jnp.int32, (1, 2 * HEAD_DIM), 1)
    row = lax.broadcasted_iota(jnp.int32, (t, t), 0)
    col = lax.broadcasted_iota(jnp.int32, (t, t), 1)
    q = q_ref[...] * FOX_SCALE
    kc = kc_ref[...].astype(BF16)
    vc = vc_ref[...].astype(BF16)
    kn = kn_ref[...].astype(BF16)
    vn = vn_ref[...].astype(BF16)
    outs = []
    for hh in range(2):
        qh = jnp.where(lane // HEAD_DIM == hh, q, 0.0).astype(BF16)
        s1 = lax.dot_general(qh, kc, NT_DIMS, preferred_element_type=F32)
        s1 = s1 - c_ref[0, hh, :, 0:past]
        s2 = lax.dot_general(qh, kn, NT_DIMS, preferred_element_type=F32)
        s2 = s2 - c_ref[0, hh, :, past:past + t]
        s2 = jnp.where(col <= row, s2, -jnp.inf)
        m = jnp.maximum(jnp.max(s1, axis=-1, keepdims=True), jnp.max(s2, axis=-1, keepdims=True))
        p1 = jnp.exp(s1 - m)
        p2 = jnp.exp(s2 - m)
        l = jnp.sum(p1, axis=-1, keepdims=True) + jnp.sum(p2, axis=-1, keepdims=True)
        acc = (jnp.dot(p1.astype(BF16), vc, preferred_element_type=F32)
               + jnp.dot(p2.astype(BF16), vn, preferred_element_type=F32))
        outs.append(acc / l)
    o_ref[...] = jnp.where(lane < HEAD_DIM, outs[0], outs[1])


def _fox_cached(q, k_stack, v_stack, layer, cache_k, cache_v, c_bhl, nb, t):
    n = q.shape[0]
    past = cache_k.shape[0] // nb
    hw = 2 * HEAD_DIM
    new = pl.BlockSpec((t, hw), lambda b, h: (b, h))
    new_l = pl.BlockSpec((None, t, hw), lambda b, h: (layer, b, h))
    old = pl.BlockSpec((past, hw), lambda b, h: (b, h))
    return pl.pallas_call(
        functools.partial(_fox_cached_kernel, past=past, t=t),
        grid=(nb, FOX_HEADS // 2),
        in_specs=[new, new_l, new_l, old, old,
                  pl.BlockSpec((1, 2, 1, past + t), lambda b, h: (b, h, 0, 0))],
        out_specs=new,
        out_shape=jax.ShapeDtypeStruct((n, FOX_WIDTH), F32),
        compiler_params=_params("arbitrary", "arbitrary"),
        name="fox_attention_cached",
    )(q, k_stack, v_stack, cache_k, cache_v, c_bhl.reshape(nb, FOX_HEADS, 1, past + t))


def _levels(chunk):
    out, m = [], chunk // 2
    while m >= 1:
        out.append(m)
        m //= 2
    return out


FINE_LEVEL = 4


def _decay_sum_matrix(chunk):
    r = np.arange(chunk)[:, None]
    j = np.arange(chunk)[None, :]
    incl = (j <= r).astype(np.float32)
    blocks = [incl]
    for m in _levels(chunk):
        if m <= FINE_LEVEL:
            mid = (r // (2 * m)) * (2 * m) + m - 1
            blocks.append(incl - (j <= mid).astype(np.float32))
    return np.concatenate(blocks, axis=0)


def _level_ids(chunk):
    levels = _levels(chunk)
    t = np.arange(chunk)
    x = t[:, None] ^ t[None, :]
    m = np.where(x > 0, 2 ** np.floor(np.log2(np.maximum(x, 1))).astype(np.int64), 0)
    lvl = np.full((chunk, chunk), len(levels) + 1, np.int32)
    for li, lm in enumerate(levels):
        lvl[(t[:, None] > t[None, :]) & (m == lm)] = li
    lvl[x == 0] = len(levels)
    return lvl


def _level_ids_wide(chunk, heads):
    return np.tile(_level_ids(chunk), (1, heads))


def _level_ids_blockdiag(chunk, heads):
    out = np.full((heads * chunk, heads * chunk), len(_levels(chunk)) + 1, np.int32)
    for h in range(heads):
        out[h * chunk:(h + 1) * chunk, h * chunk:(h + 1) * chunk] = _level_ids(chunk)
    return out


def _two_pass(mat_bf16, x):
    hi, lo = _split(x)
    d = lambda u: jnp.dot(mat_bf16, u, preferred_element_type=F32)
    return d(hi) + d(lo)


def _hgrn_kernel(p_ref, lb_ref, ng_ref, s0_ref, dmat_ref, lvl_ref, ones_ref, o_ref, st_ref, st_scr, *, chunk,
                 nchunks, group):
    ti = pl.program_id(1)

    @pl.when(ti == 0)
    def _():
        st_scr[...] = s0_ref[0]

    w = HG_WIDTH
    levels = _levels(chunk)
    lb = lb_ref[...]
    log_lb = jnp.log(lb)
    log_1m_lb = jnp.log1p(-lb)
    row = lax.broadcasted_iota(jnp.int32, (chunk, 1), 0)
    lane_head = lax.broadcasted_iota(jnp.int32, (1, w), 1) // HEAD_DIM
    row_head = lax.broadcasted_iota(jnp.int32, (w, 1), 0) // HEAD_DIM

    def stack(x):
        return jnp.concatenate([jnp.where(lane_head == h, x, 0.0) for h in range(HG_HEADS)], axis=0)

    def each(fn, *lists):
        return [fn(*xs) for xs in zip(*lists)]

    def chunk_group(gi, _):
        base = gi * (group * chunk)
        rows = [pl.ds(pl.multiple_of(base + i * chunk, chunk), chunk) for i in range(group)]
        q = [p_ref[r, 0:w] for r in rows]
        fx = [p_ref[r, w:2 * w] for r in rows]
        val = [p_ref[r, 2 * w:3 * w] for r in rows]
        bv = each(lambda x: log_1m_lb + _log_sigmoid(x), fx)
        logf = each(lambda x: jnp.maximum(log_lb, x) + jnp.log1p(jnp.exp(-jnp.abs(log_lb - x))), bv)
        kd = each(lambda x: (1.0 - lb) * _sigmoid(-x), fx)
        sums = each(lambda x: _two_pass(dmat_ref[...], x), logf)
        b = [x[0:chunk] for x in sums]

        lvl = lvl_ref[...]
        attn = each(lambda qi, ki: jnp.where(lvl == len(levels), _mm(qi, stack(ki), NT_DIMS), 0.0), q, kd)
        fine = 0
        for li, m in enumerate(levels):
            if m > FINE_LEVEL:
                def rel(bi):
                    mids = [jnp.broadcast_to(bi[s0 + m - 1:s0 + m, :], (2 * m, w)) for s0 in range(0, chunk, 2 * m)]
                    return bi - jnp.concatenate(mids, axis=0)
                x = each(rel, b)
            else:
                fine += 1
                x = [sm[fine * chunk:(fine + 1) * chunk] for sm in sums]
            wgt = each(lambda xi: jnp.exp(-jnp.abs(xi)), x)
            second = ((row // m) % 2) == 1
            ql = each(lambda qi, wi: jnp.where(second, qi * wi, 0.0), q, wgt)
            kl = each(lambda ki, wi: jnp.where(second, 0.0, ki * wi), kd, wgt)
            attn = each(lambda at, qi, ki: jnp.where(lvl == li, _mm(qi, stack(ki), NT_DIMS), at), attn, ql, kl)
        o_intra = each(lambda at, vi: _mm(at, stack(vi)), attn, val)
        q_dec = each(lambda qi, bi: qi * jnp.exp(bi), q, b)
        b_last = [bi[chunk - 1:chunk, :] for bi in b]
        upd = each(lambda vi, ki, bl, bi: _mm(vi, ki * jnp.exp(bl - bi), TN_DIMS), val, kd, b_last, b)

        st = st_scr[...]
        for i in range(group):
            o = _mm(q_dec[i], st, NT_DIMS) + o_intra[i]
            st = jnp.exp(b_last[i]) * st + jnp.where(row_head == lane_head, upd[i], 0.0)
            ms = _head_sum(o * o, ones_ref[...]) * (1.0 / HEAD_DIM)
            g = p_ref[rows[i], 3 * w:4 * w]
            o_ref[rows[i], :] = o * lax.rsqrt(ms + EPS) * ng_ref[...] * (g * _sigmoid(g))
        st_scr[...] = st
        return 0

    lax.fori_loop(0, nchunks // group, chunk_group, 0)

    @pl.when(ti == pl.num_programs(1) - 1)
    def _():
        st_ref[0] = st_scr[...]


def _hgrn(p_hg, lb, norm_g, st0, nb, t, chunk, tile):
    n = p_hg.shape[0]
    nt = t // tile
    w = HG_WIDTH
    dmat = jnp.asarray(_decay_sum_matrix(chunk), BF16)
    lvl = jnp.asarray(_level_ids_wide(chunk, HG_HEADS))
    ones = jnp.asarray(_block_ones_np(w), BF16)
    return pl.pallas_call(
        functools.partial(_hgrn_kernel, chunk=chunk, nchunks=tile // chunk, group=min(4, tile // chunk)),
        grid=(nb, nt),
        in_specs=[
            pl.BlockSpec((tile, HG_COLS), lambda b, i: (b * nt + i, 0)),
            _const_spec((1, w)), _const_spec((1, w)),
            pl.BlockSpec((1, w, w), lambda b, i: (b, 0, 0)),
            _const_spec(dmat.shape), _const_spec(lvl.shape), _const_spec(ones.shape),
        ],
        out_specs=[pl.BlockSpec((tile, w), lambda b, i: (b * nt + i, 0)),
                   pl.BlockSpec((1, w, w), lambda b, i: (b, 0, 0))],
        out_shape=[jax.ShapeDtypeStruct((n, w), F32), jax.ShapeDtypeStruct((nb, w, w), F32)],
        scratch_shapes=[pltpu.VMEM((w, w), F32)],
        compiler_params=_params("arbitrary", "arbitrary"),
        name="hgrn2",
    )(p_hg, lb, norm_g, st0, dmat, lvl, ones)


def _rwkv_kernel(p_ref, prev_ref, s0_ref, mu_ref, vec_ref, w2_ref, a2_ref, g2_ref, ltri_ref, ones_ref,
                 lvlw_ref, lvlbd_ref, y_ref, st_ref, st_scr, prev_scr, *, chunk, nchunks, group):
    ti = pl.program_id(1)

    @pl.when(ti == 0)
    def _():
        st_scr[...] = s0_ref[0]
        prev_scr[...] = prev_ref[0]

    w = RW_WIDTH
    hc = RW_HEADS * chunk
    levels = _levels(chunk)
    nl = len(levels)
    w0 = vec_ref[0:1, :]
    a0 = vec_ref[1:2, :]
    k_k = vec_ref[2:3, :]
    k_a = vec_ref[3:4, :]
    r_k = vec_ref[4:5, :]
    ln_w = vec_ref[5:6, :]
    ln_b = vec_ref[6:7, :]
    row = lax.broadcasted_iota(jnp.int32, (chunk, 1), 0)
    lane_head = lax.broadcasted_iota(jnp.int32, (1, w), 1) // HEAD_DIM
    row_head = lax.broadcasted_iota(jnp.int32, (w, 1), 0) // HEAD_DIM
    wide_head = lax.broadcasted_iota(jnp.int32, (1, hc), 1) // chunk

    def stack(x):
        return jnp.concatenate([jnp.where(lane_head == h, x, jnp.zeros_like(x)) for h in range(RW_HEADS)], axis=0)

    def blockdiag(xw):
        return jnp.concatenate([jnp.where(wide_head == h, xw, jnp.zeros_like(xw)) for h in range(RW_HEADS)], axis=0)

    def each(fn, *lists):
        return [fn(*xs) for xs in zip(*lists)]

    def chunk_group(gi, _):
        base = gi * (group * chunk)
        rows = [pl.ds(pl.multiple_of(base + i * chunk, chunk), chunk) for i in range(group)]
        ones_bd = ones_ref[...]
        mu = mu_ref[...]
        ps = [p_ref[r, :] for r in rows]
        prevs = [prev_scr[...]] + [p[chunk - 1:chunk, :] for p in ps[:-1]]
        prev_scr[...] = ps[-1][chunk - 1:chunk, :]
        xs = each(lambda p, pv: p + mu * (jnp.where(row == 0, pv, pltpu.roll(p, 1, 0)) - p), ps, prevs)
        r = [x[:, 0:w] for x in xs]
        k = [x[:, w:2 * w] for x in xs]
        v = [x[:, 2 * w:3 * w] for x in xs]
        tail = [x[:, 3 * w:] for x in xs]
        w_lin = each(lambda t: w0 + _mm3(jnp.tanh(t), w2_ref[...]), tail)
        a = each(lambda t: _sigmoid(a0 + _mm(t, a2_ref[...])), tail)
        g = each(lambda t: _mm(_sigmoid(t), g2_ref[...]), tail)
        logw = each(lambda wl: -jnp.exp(-(jnp.maximum(-wl, 0.0) + jnp.log1p(jnp.exp(-jnp.abs(wl)))) - 0.5), w_lin)
        kk = each(lambda x: x * k_k, k)
        kk = each(lambda x: x / jnp.maximum(jnp.sqrt(_head_sum(x * x, ones_bd)), 1e-12), kk)
        kmod = each(lambda x, ai: x * (1.0 + (ai - 1.0) * k_a), k, a)
        beta = each(lambda x, ai: x * ai, kk, a)

        lg = each(lambda x: _two_pass(ltri_ref[...], x), logw)
        lg_end = [x[chunk - 1:chunk, :] for x in lg]
        a_n = each(lambda x, l, lw: x * jnp.exp(l - lw), kk, lg, logw)
        r_n = each(lambda x, l: x * jnp.exp(l), r, lg)
        inv_g = each(lambda l: jnp.exp(-l), lg)
        to_end = each(lambda le, l: jnp.exp(le - l), lg_end, lg)
        v_s = each(stack, v)

        gram = each(lambda an, rn, be, km, ig: _mm(jnp.concatenate([an, rn], axis=0),
                                                   jnp.concatenate([stack(be * ig), stack(km * ig)], axis=0),
                                                   NT_DIMS), a_n, r_n, beta, kmod, inv_g)
        lvlw = lvlw_ref[...]
        strict = lvlw < nl
        incl = lvlw <= nl
        n_w = [jnp.where(strict, x[0:chunk, 0:hc], 0.0).astype(BF16) for x in gram]
        m_k = [jnp.where(strict, x[0:chunk, hc:2 * hc], 0.0) for x in gram]
        l_b = [jnp.where(incl, x[chunk:2 * chunk, 0:hc], 0.0) for x in gram]
        l_k = [jnp.where(incl, x[chunk:2 * chunk, hc:2 * hc], 0.0) for x in gram]

        lvlbd = lvlbd_ref[...]
        n_bd = each(blockdiag, n_w)
        t_w = each(lambda n: (jnp.where(lvlw == nl, 1.0, 0.0)
                              - jnp.where(lvlw == nl - 1, n, jnp.zeros_like(n)).astype(F32)), n_w)
        blk = 2
        while blk < chunk:
            li = levels.index(blk)
            x_w = each(lambda t, n: _mm(t, jnp.where(lvlbd == li, n, jnp.zeros_like(n))), t_w, n_bd)
            t_w = each(lambda t, x: t - _mm(x, blockdiag(t)), t_w, x_w)
            blk *= 2

        st = st_scr[...]
        for i in range(group):
            rhs = _mm(a_n[i], st, NT_DIMS) + _mm(m_k[i], v_s[i])
            e = _mm(t_w[i], stack(rhs))
            e_hi, e_lo = _split(stack(e))
            n_e = (jnp.dot(n_w[i], e_hi, preferred_element_type=F32)
                   + jnp.dot(n_w[i], e_lo, preferred_element_type=F32))
            e = e + _mm(t_w[i], stack(rhs - e - n_e))
            y = _mm(r_n[i], st, NT_DIMS) + _mm(l_k[i], v_s[i]) - _mm(l_b[i], stack(e))
            upd = _mm(jnp.concatenate([v[i], e], axis=0),
                      jnp.concatenate([kmod[i] * to_end[i], -(beta[i] * to_end[i])], axis=0), TN_DIMS)
            st = st * jnp.exp(lg_end[i]) + jnp.where(row_head == lane_head, upd, 0.0)

            inv_n = 1.0 / HEAD_DIM
            mean = _head_sum(y, ones_bd) * inv_n
            yc = y - mean
            var = _head_sum(yc * yc, ones_bd) * inv_n
            yn = yc * lax.rsqrt(var + RW_GN_EPS) * ln_w + ln_b
            bonus = _head_sum(r[i] * kmod[i] * r_k, ones_bd) * v[i]
            y_ref[rows[i], :] = (yn + bonus) * g[i]
        st_scr[...] = st
        return 0

    lax.fori_loop(0, nchunks // group, chunk_group, 0)

    @pl.when(ti == pl.num_programs(1) - 1)
    def _():
        st_ref[0] = st_scr[...]


def _rwkv(p_rw, prev, st0, mu, vecs, w2p, a2p, g2p, nb, t, chunk, tile):
    n = p_rw.shape[0]
    nt = t // tile
    w = RW_WIDTH
    consts = (jnp.asarray(np.tril(np.ones((chunk, chunk), np.float32)), BF16),
              jnp.asarray(_block_ones_np(w), BF16),
              jnp.asarray(_level_ids_wide(chunk, RW_HEADS)),
              jnp.asarray(_level_ids_blockdiag(chunk, RW_HEADS)))
    return pl.pallas_call(
        functools.partial(_rwkv_kernel, chunk=chunk, nchunks=tile // chunk, group=min(4, tile // chunk)),
        grid=(nb, nt),
        in_specs=[
            pl.BlockSpec((tile, RW_COLS), lambda b, i: (b * nt + i, 0)),
            pl.BlockSpec((1, 1, RW_COLS), lambda b, i: (b, 0, 0)),
            pl.BlockSpec((1, w, w), lambda b, i: (b, 0, 0)),
            _const_spec((1, RW_COLS)), _const_spec((8, w)),
            _const_spec((RW_LORA, w)), _const_spec((RW_LORA, w)), _const_spec((RW_LORA, w)),
        ] + [_const_spec(c.shape) for c in consts],
        out_specs=[pl.BlockSpec((tile, w), lambda b, i: (b * nt + i, 0)),
                   pl.BlockSpec((1, w, w), lambda b, i: (b, 0, 0))],
        out_shape=[jax.ShapeDtypeStruct((n, w), F32), jax.ShapeDtypeStruct((nb, w, w), F32)],
        scratch_shapes=[pltpu.VMEM((w, w), F32), pltpu.VMEM((1, RW_COLS), F32)],
        compiler_params=_params("arbitrary", "arbitrary"),
        name="rwkv7",
    )(p_rw, prev, st0, mu, vecs, w2p, a2p, g2p, *consts)


def _outffn_kernel(yrw_ref, yfox_ref, yhg_ref, x_ref, ga1_ref, sh2_ref, sc2_ref, ga2_ref, g2_ref,
                   worw_ref, wofox_ref, wohg_ref, wg_ref, wu_ref, wo_ref, fg_ref, o_ref, *, final_norm, ff_step):
    mix = (jnp.dot(yrw_ref[...].astype(BF16), worw_ref[...], preferred_element_type=F32)
           + jnp.dot(yfox_ref[...].astype(BF16), wofox_ref[...], preferred_element_type=F32)
           + jnp.dot(yhg_ref[...].astype(BF16), wohg_ref[...], preferred_element_type=F32))
    x1 = x_ref[...] + ga1_ref[0] * mix
    ms = jnp.mean(x1 * x1, axis=-1, keepdims=True)
    h2 = x1 * lax.rsqrt(ms + EPS) * g2_ref[...]
    h2 = (h2 * (1.0 + sc2_ref[0]) + sh2_ref[0]).astype(BF16)
    ffn = jnp.zeros_like(x1)
    for j in range(D_FF // ff_step):
        cols = slice(j * ff_step, (j + 1) * ff_step)
        gate = jnp.dot(h2, wg_ref[:, cols], preferred_element_type=F32)
        up = jnp.dot(h2, wu_ref[:, cols], preferred_element_type=F32)
        act = (gate * _sigmoid(gate) * up).astype(BF16)
        ffn = ffn + jnp.dot(act, wo_ref[cols, :], preferred_element_type=F32)
    x2 = x1 + ga2_ref[0] * ffn
    if final_norm:
        ms2 = jnp.mean(x2 * x2, axis=-1, keepdims=True)
        x2 = x2 * lax.rsqrt(ms2 + EPS) * fg_ref[...]
    o_ref[...] = x2


def _outffn(y_rw, y_fox, y_hg, x, ga1, sh2, sc2, ga2, g2, wts, final_g, final_norm, tm, tiles_per_batch):
    n = x.shape[0]
    row = lambda w: pl.BlockSpec((tm, w), lambda i: (i, 0))
    mods = [_mod_spec(m, tiles_per_batch) for m in (ga1, sh2, sc2, ga2)]
    return pl.pallas_call(
        functools.partial(_outffn_kernel, final_norm=final_norm, ff_step=D_FF // 2),
        grid=(n // tm,),
        in_specs=[row(RW_WIDTH), row(FOX_WIDTH), row(HG_WIDTH), row(D_MODEL)] + mods
        + [_const_spec((1, D_MODEL))] + [_const_spec(w.shape) for w in wts] + [_const_spec((1, D_MODEL))],
        out_specs=row(D_MODEL),
        out_shape=jax.ShapeDtypeStruct((n, D_MODEL), F32),
        compiler_params=_params("arbitrary"),
        name="outffn",
    )(y_rw, y_fox, y_hg, x, ga1, sh2, sc2, ga2, g2, *wts, final_g)


def _block_diag_t(s):
    nb, h, dk, dv = s.shape
    st = jnp.swapaxes(s, 2, 3)
    eye = jnp.eye(h, dtype=s.dtype)
    return jnp.einsum("bhvk,hg->bhvgk", st, eye).reshape(nb, h * dv, h * dk)


def _block_diag(s):
    nb, h, dv, dk = s.shape
    eye = jnp.eye(h, dtype=s.dtype)
    return jnp.einsum("bhvk,hg->bhvgk", s, eye).reshape(nb, h * dv, h * dk)


def _diag_blocks(sbd, h):
    nb, rows, cols = sbd.shape
    x = sbd.reshape(nb, h, rows // h, h, cols // h)
    return jnp.stack([x[:, i, :, i, :] for i in range(h)], axis=1)


def _layer_weights(l, w_in, rw_w2, rw_a2, rw_g2, w_out, w_ffn_in, w_ffn_out):
    wi = w_in[l]
    c0, c1 = RW_COLS, RW_COLS + 3 * FOX_WIDTH
    proj = (wi[:, :c0].astype(BF16), wi[:, c0:c1].astype(BF16),
            wi[:, c0 + FOX_WIDTH:c0 + 2 * FOX_WIDTH].T.astype(BF16),
            wi[:, c1:c1 + FOX_HEADS].T.astype(BF16), wi[:, c1 + FOX_HEADS:].astype(BF16))
    zeros = lambda r: jnp.zeros((r, RW_WIDTH), F32)
    w2p = jnp.concatenate([rw_w2[l], zeros(RW_LORA - RW_W_LORA)], axis=0)
    a2p = jnp.concatenate([zeros(RW_W_LORA), rw_a2[l], zeros(RW_G_LORA)], axis=0)
    g2p = jnp.concatenate([zeros(RW_W_LORA + RW_A_LORA), rw_g2[l]], axis=0)
    wo = w_out[l].astype(BF16)
    ffn = (wo[:RW_WIDTH], wo[RW_WIDTH:RW_WIDTH + FOX_WIDTH], wo[RW_WIDTH + FOX_WIDTH:],
           w_ffn_in[l][:, :D_FF].astype(BF16), w_ffn_in[l][:, D_FF:].astype(BF16), w_ffn_out[l].astype(BF16))
    return proj, (w2p, a2p, g2p), ffn


def _group_layer(x, mods, lw, lp, layer, kv_stack, prev, rw_st0, hg_st0, cache, nb, t, cfg, final_g, final_norm):
    proj_w, lora_w, ffn_w = lw
    sh1, sc1, ga1, sh2, sc2, ga2 = mods
    tm, tpb = cfg["tm"], cfg["tiles_per_batch"]
    p_rw, q, k_stack, v_stack, kt, f_t, p_hg = _normproj(x, sh1, sc1, lp["norm1_g"], proj_w, kv_stack, layer, tm, tpb,
                                                         cache is None)
    f_bht = jnp.swapaxes(f_t.reshape(FOX_HEADS, nb, t), 0, 1)
    if cache is None:
        logf_bht, c_bhl = _gate(f_bht, lp["fox_b_f"], None)
        y_fox = _fox(q, kt, v_stack, layer, c_bhl, nb, t, cfg["tq"])
    else:
        cache_k, cache_v, cache_logf = cache
        past = cache_k.shape[1]
        logf_bht, c_bhl = _gate(f_bht, lp["fox_b_f"], jnp.swapaxes(cache_logf, 1, 2))
        y_fox = _fox_cached(q, k_stack, v_stack, layer, cache_k.reshape(nb * past, FOX_WIDTH),
                            cache_v.reshape(nb * past, FOX_WIDTH), c_bhl, nb, t)
    y_hg, hg_st = _hgrn(p_hg, lp["hg_lb"], lp["hg_norm_g"], hg_st0, nb, t, cfg["hg_chunk"], cfg["hg_tile"])
    y_rw, rw_st = _rwkv(p_rw, prev, rw_st0, lp["rw_mu"], lp["rw_vecs"], *lora_w, nb, t,
                        cfg["rw_chunk"], cfg["rw_tile"])
    x = _outffn(y_rw, y_fox, y_hg, x, ga1, sh2, sc2, ga2, lp["norm2_g"], ffn_w, final_g, final_norm, tm, tpb)
    states = (
        jnp.swapaxes(logf_bht, 1, 2),
        _diag_blocks(rw_st, RW_HEADS),
        p_rw.reshape(nb, t, RW_COLS)[:, t - 1:t, :],
        jnp.swapaxes(_diag_blocks(hg_st, HG_HEADS), 2, 3),
    )
    return x, (k_stack, v_stack), states


def kernel(x_prompt, x_sample, c_prompt, c_sample, cache_fox_k, cache_fox_v, cache_fox_logf, state_rwkv,
           state_rwkv_shift, state_hgrn, norm1_g, w_ada, b_ada, w_in, rw_mu, rw_w0, rw_w2, rw_a0, rw_a2, rw_g2,
           rw_k_k, rw_k_a, rw_r_k, rw_ln_w, rw_ln_b, fox_b_f, hg_lb_logits, hg_norm_g, w_out, norm2_g,
           w_ffn_in, w_ffn_out, final_norm_g):
    dt = x_prompt.dtype
    bp, tp, _ = x_prompt.shape
    bs, ts, _ = x_sample.shape

    mod = _modulation(jnp.concatenate([c_prompt, c_sample], axis=0).astype(F32), w_ada, b_ada)
    lbs = _hg_lower_bounds(hg_lb_logits.astype(F32))

    cfg_p = dict(tm=512, tiles_per_batch=tp // 512, tq=512, hg_chunk=HG_CHUNK, hg_tile=512,
                 rw_chunk=RW_CHUNK, rw_tile=512)
    cfg_s = dict(tm=bs * ts, tiles_per_batch=1, hg_chunk=ts, hg_tile=ts, rw_chunk=ts, rw_tile=ts)

    xp = x_prompt.reshape(bp * tp, D_MODEL).astype(F32)
    xs = x_sample.reshape(bs * ts, D_MODEL).astype(F32)
    zero_prev = jnp.zeros((bp, 1, RW_COLS), F32)
    zero_state = jnp.zeros((bp, RW_WIDTH, RW_WIDTH), F32)
    final_g = final_norm_g.reshape(1, D_MODEL)
    row = lambda a: a.reshape(1, -1)

    kv_p = (jnp.zeros((DEPTH, bp * tp, FOX_WIDTH), F32), jnp.zeros((DEPTH, bp * tp, FOX_WIDTH), F32))
    kv_s = (jnp.zeros((DEPTH, bs * ts, FOX_WIDTH), F32), jnp.zeros((DEPTH, bs * ts, FOX_WIDTH), F32))
    outs_p, outs_s = [], []
    for l in range(DEPTH):
        lw = _layer_weights(l, w_in, rw_w2, rw_a2, rw_g2, w_out, w_ffn_in, w_ffn_out)
        zero_row = jnp.zeros((RW_WIDTH,), F32)
        lp = dict(
            norm1_g=row(norm1_g[l]), norm2_g=row(norm2_g[l]), fox_b_f=fox_b_f[l],
            hg_lb=row(lbs[l]), hg_norm_g=row(hg_norm_g[l]), rw_mu=row(rw_mu[l]),
            rw_vecs=jnp.stack([rw_w0[l], rw_a0[l], rw_k_k[l], rw_k_a[l], rw_r_k[l].reshape(-1),
                               rw_ln_w[l], rw_ln_b[l], zero_row], axis=0),
        )
        chunks_p = [m[:, None, :] for m in jnp.split(mod[l, :bp], 6, axis=-1)]
        chunks_s = [jnp.repeat(m, ts, axis=0)[None] for m in jnp.split(mod[l, bp:], 6, axis=-1)]
        last = l == DEPTH - 1
        xp, kv_p, st_p = _group_layer(xp, chunks_p, lw, lp, l, kv_p, zero_prev, zero_state, zero_state, None,
                                      bp, tp, cfg_p, final_g, last)
        xs, kv_s, st_s = _group_layer(xs, chunks_s, lw, lp, l, kv_s, state_rwkv_shift[l].astype(F32),
                                      _block_diag(state_rwkv[l].astype(F32)),
                                      _block_diag_t(state_hgrn[l].astype(F32)),
                                      (cache_fox_k[l].astype(F32), cache_fox_v[l].astype(F32),
                                       cache_fox_logf[l].astype(F32)),
                                      bs, ts, cfg_s, final_g, last)
        outs_p.append(st_p)
        outs_s.append(st_s)

    stk = lambda outs, j: jnp.stack([o[j] for o in outs], axis=0).astype(dt)
    y_prompt = xp.reshape(bp, tp, D_MODEL).astype(dt)
    y_sample = xs.reshape(bs, ts, D_MODEL).astype(dt)
    heads = lambda a, nb, t: a.reshape(DEPTH, nb, t, FOX_HEADS, HEAD_DIM).astype(dt)
    return (y_prompt, y_sample,
            heads(kv_p[0], bp, tp), heads(kv_p[1], bp, tp),
            stk(outs_p, 0), stk(outs_p, 1), stk(outs_p, 2), stk(outs_p, 3),
            heads(kv_s[0], bs, ts), heads(kv_s[1], bs, ts),
            stk(outs_s, 0), stk(outs_s, 1), stk(outs_s, 2), stk(outs_s, 3))
```

```python
import functools

import numpy as np
import jax
import jax.numpy as jnp
from jax import lax
from jax.experimental import pallas as pl
from jax.experimental.pallas import tpu as pltpu

F32 = jnp.float32
BF16 = jnp.bfloat16
HIGHEST = lax.Precision.HIGHEST

D_MODEL = 1024
DEPTH = 4
HEAD_DIM = 64
RW_HEADS = 4
RW_WIDTH = RW_HEADS * HEAD_DIM
FOX_HEADS = 8
FOX_WIDTH = FOX_HEADS * HEAD_DIM
HG_HEADS = 4
HG_WIDTH = HG_HEADS * HEAD_DIM
RW_W_LORA = 32
RW_A_LORA = 32
RW_G_LORA = 64
RW_LORA = RW_W_LORA + RW_A_LORA + RW_G_LORA
RW_COLS = 3 * RW_WIDTH + RW_LORA
FOX_COLS = 3 * FOX_WIDTH + FOX_HEADS
HG_COLS = 4 * HG_WIDTH
D_FF = 2816
EPS = 1e-6
RW_GN_EPS = 64e-5
FOX_SCALE = HEAD_DIM ** -0.5
HG_CHUNK = 64
RW_CHUNK = 64

VMEM_LIMIT_BYTES = 56 * 1024 * 1024

NT_DIMS = (((1,), (1,)), ((), ()))
TN_DIMS = (((0,), (0,)), ((), ()))
NN_DIMS = (((1,), (0,)), ((), ()))


def _params(*sem):
    return pltpu.CompilerParams(dimension_semantics=sem, vmem_limit_bytes=VMEM_LIMIT_BYTES)


def _const_spec(shape):
    nd = len(shape)
    return pl.BlockSpec(shape, lambda *_: (0,) * nd, pipeline_mode=pl.Buffered(1))


def _mm(a, b, dims=NN_DIMS):
    return lax.dot_general(a.astype(BF16), b.astype(BF16), dims, preferred_element_type=F32)


def _split(a):
    hi = a.astype(BF16)
    lo = (a - hi.astype(F32)).astype(BF16)
    return hi, lo


def _mm3(a, b, dims=NN_DIMS):
    ah, al = _split(a)
    bh, bl = _split(b)
    d = lambda x, y: lax.dot_general(x, y, dims, preferred_element_type=F32)
    return d(ah, bh) + (d(ah, bl) + d(al, bh))


def _head_sum(x, ones_bd):
    hi, lo = _split(x)
    d = lambda u: jnp.dot(u, ones_bd, preferred_element_type=F32)
    return d(hi) + d(lo)


def _block_ones_np(width):
    r = np.arange(width) // HEAD_DIM
    return (r[:, None] == r[None, :]).astype(np.float32)


def _log_sigmoid(z):
    return jnp.minimum(z, 0.0) - jnp.log1p(jnp.exp(-jnp.abs(z)))


def _sigmoid(z):
    return 1.0 / (1.0 + jnp.exp(-z))


def _mod_kernel(c_ref, w_ref, b_ref, o_ref):
    c = c_ref[...]
    s = c * _sigmoid(c)
    o_ref[0] = jnp.dot(s, w_ref[0], precision=HIGHEST, preferred_element_type=F32) + b_ref[0]


def _modulation(c_all, w_ada, b_ada):
    nb = c_all.shape[0]
    tn = 1536
    n_out = w_ada.shape[2]
    return pl.pallas_call(
        _mod_kernel,
        grid=(DEPTH, n_out // tn),
        in_specs=[
            pl.BlockSpec((nb, D_MODEL), lambda l, j: (0, 0)),
            pl.BlockSpec((1, D_MODEL, tn), lambda l, j: (l, 0, j)),
            pl.BlockSpec((1, 1, tn), lambda l, j: (l, 0, j)),
        ],
        out_specs=pl.BlockSpec((1, nb, tn), lambda l, j: (l, 0, j)),
        out_shape=jax.ShapeDtypeStruct((DEPTH, nb, n_out), F32),
        compiler_params=_params("arbitrary", "arbitrary"),
        name="adaln_mod",
    )(c_all, w_ada, b_ada.reshape(DEPTH, 1, n_out))


def _lbs_kernel(x_ref, o_ref):
    x = x_ref[...]
    m = jnp.max(x, axis=0, keepdims=True)
    e = jnp.exp(x - m)
    p = e / jnp.sum(e, axis=0, keepdims=True)
    acc = jnp.zeros_like(p[0:1])
    rows = []
    for l in range(DEPTH):
        acc = acc + p[l:l + 1]
        rows.append(acc)
    cs = jnp.concatenate(rows, axis=0)
    o_ref[...] = cs - cs[0:1]


def _hg_lower_bounds(logits):
    return pl.pallas_call(
        _lbs_kernel,
        out_shape=jax.ShapeDtypeStruct(logits.shape, F32),
        name="hgrn_lower_bounds",
    )(logits)


def _normproj_kernel(x_ref, sh_ref, sc_ref, g_ref, wrw_ref, wqkv_ref, wkt_ref, wft_ref, whg_ref,
                     prw_ref, q_ref, k_ref, v_ref, kt_ref, ft_ref, phg_ref, *, emit_kt):
    x = x_ref[...]
    ms = jnp.mean(x * x, axis=-1, keepdims=True)
    h = x * lax.rsqrt(ms + EPS) * g_ref[...]
    h = h * (1.0 + sc_ref[0]) + sh_ref[0]
    hb = h.astype(BF16)
    prw_ref[...] = jnp.dot(hb, wrw_ref[...], preferred_element_type=F32)
    qkv = jnp.dot(hb, wqkv_ref[...], preferred_element_type=F32)
    q_ref[...] = qkv[:, :FOX_WIDTH]
    k_ref[...] = qkv[:, FOX_WIDTH:2 * FOX_WIDTH]
    v_ref[...] = qkv[:, 2 * FOX_WIDTH:]
    if emit_kt:
        kt_ref[...] = lax.dot_general(wkt_ref[...], hb, NT_DIMS, preferred_element_type=F32).astype(BF16)
    else:
        kt_ref[...] = jnp.zeros(kt_ref.shape, BF16)
    ft_ref[0] = lax.dot_general(wft_ref[...], hb, NT_DIMS, preferred_element_type=F32)
    phg_ref[...] = jnp.dot(hb, whg_ref[...], preferred_element_type=F32)


def _mod_spec(arr, tiles_per_batch):
    _, r, d = arr.shape
    return pl.BlockSpec((1, r, d), lambda i: (i // tiles_per_batch, 0, 0))


def _normproj(x, sh, sc, g, wts, tm, tiles_per_batch, emit_kt):
    n = x.shape[0]
    wrw, wqkv, wkt, wft, whg = wts
    row = lambda w: pl.BlockSpec((tm, w), lambda i: (i, 0))
    nbat = n // (tm * tiles_per_batch)
    kt_shape = (FOX_WIDTH, n) if emit_kt else (8, 128)
    kt_spec = (pl.BlockSpec((FOX_WIDTH, tm), lambda i: (0, i)) if emit_kt
               else pl.BlockSpec((8, 128), lambda i: (0, 0)))
    return pl.pallas_call(
        functools.partial(_normproj_kernel, emit_kt=emit_kt),
        grid=(n // tm,),
        in_specs=[
            row(D_MODEL), _mod_spec(sh, tiles_per_batch), _mod_spec(sc, tiles_per_batch),
            _const_spec((1, D_MODEL)),
            _const_spec(wrw.shape), _const_spec(wqkv.shape), _const_spec(wkt.shape), _const_spec(wft.shape),
            _const_spec(whg.shape),
        ],
        out_specs=[row(RW_COLS), row(FOX_WIDTH), row(FOX_WIDTH), row(FOX_WIDTH), kt_spec,
                   pl.BlockSpec((1, FOX_HEADS, tm), lambda i: (i // tiles_per_batch, 0, i % tiles_per_batch)),
                   row(HG_COLS)],
        out_shape=[
            jax.ShapeDtypeStruct((n, RW_COLS), F32),
            jax.ShapeDtypeStruct((n, FOX_WIDTH), F32),
            jax.ShapeDtypeStruct((n, FOX_WIDTH), F32),
            jax.ShapeDtypeStruct((n, FOX_WIDTH), F32),
            jax.ShapeDtypeStruct(kt_shape, BF16),
            jax.ShapeDtypeStruct((nbat, FOX_HEADS, tm * tiles_per_batch), F32),
            jax.ShapeDtypeStruct((n, HG_COLS), F32),
        ],
        compiler_params=_params("arbitrary"),
        name="normproj",
    )(x, sh, sc, g, wrw, wqkv, wkt, wft, whg)


def _cumsum_lanes(x, carry):
    length = x.shape[1]
    step = min(512, length)
    r = lax.broadcasted_iota(jnp.int32, (step, step), 0)
    c = lax.broadcasted_iota(jnp.int32, (step, step), 1)
    upper = (r <= c).astype(F32)
    outs = []
    for j in range(length // step):
        seg = x[:, j * step:(j + 1) * step]
        cs = jnp.dot(seg, upper, precision=HIGHEST, preferred_element_type=F32) + carry
        carry = cs[:, step - 1:step]
        outs.append(cs)
    return outs, carry, step


def _gate_kernel(*refs, past):
    if past:
        f_ref, bf_ref, cache_ref, logf_ref, c_ref = refs
    else:
        f_ref, bf_ref, logf_ref, c_ref = refs
    logf = _log_sigmoid(f_ref[0] + bf_ref[...])
    logf_ref[0] = logf
    carry = jnp.zeros((FOX_HEADS, 1), F32)
    if past:
        outs, carry, step = _cumsum_lanes(cache_ref[0], carry)
        for j, cs in enumerate(outs):
            c_ref[0, :, j * step:(j + 1) * step] = cs
    outs, carry, step = _cumsum_lanes(logf, carry)
    for j, cs in enumerate(outs):
        c_ref[0, :, past + j * step:past + (j + 1) * step] = cs


def _gate(f_bht, b_f, cache_logf_bhp):
    nb, _, t = f_bht.shape
    past = 0 if cache_logf_bhp is None else cache_logf_bhp.shape[2]
    blk = lambda w: pl.BlockSpec((1, FOX_HEADS, w), lambda b: (b, 0, 0))
    in_specs = [blk(t), _const_spec((FOX_HEADS, 1))]
    args = [f_bht, b_f.reshape(FOX_HEADS, 1)]
    if past:
        in_specs.append(blk(past))
        args.append(cache_logf_bhp)
    return pl.pallas_call(
        functools.partial(_gate_kernel, past=past),
        grid=(nb,),
        in_specs=in_specs,
        out_specs=[blk(t), blk(past + t)],
        out_shape=[jax.ShapeDtypeStruct((nb, FOX_HEADS, t), F32),
                   jax.ShapeDtypeStruct((nb, FOX_HEADS, past + t), F32)],
        compiler_params=_params("arbitrary"),
        name="fox_gate",
    )(*args)


LOG2E = 1.4426950408889634
BIAS_ROWS = 16


def _fox_kernel(q_ref, kt_ref, v_ref, c_ref, o_ref, kaug_scr, vaug_scr, *, tq):
    qi = pl.program_id(2)
    hd = HEAD_DIM
    lane = lax.broadcasted_iota(jnp.int32, (1, 2 * hd), 1)
    bias_lane0 = (hd, 0)
    ones_lane = (hd, 0)

    @pl.when(qi == 0)
    def _():
        kt = kt_ref[...]
        v = v_ref[...]
        t = kt.shape[1]
        for hh in range(2):
            c2 = c_ref[0, hh] * LOG2E
            hi = c2.astype(BF16).astype(F32)
            mid = (c2 - hi).astype(BF16).astype(F32)
            lo = c2 - hi - mid
            bias = jnp.concatenate([-hi, -mid, -lo, jnp.zeros((BIAS_ROWS - 3, t), F32)], axis=0).astype(BF16)
            pad = jnp.zeros((hd - BIAS_ROWS, t), BF16)
            if hh == 0:
                kaug_scr[hh] = jnp.concatenate([kt[0:hd], bias, pad], axis=0)
            else:
                kaug_scr[hh] = jnp.concatenate([bias, pad, kt[hd:2 * hd]], axis=0)
            own = (lane // hd) == hh
            vaug_scr[hh] = jnp.where(own, v, jnp.where(lane == ones_lane[hh], 1.0, 0.0)).astype(BF16)

    row = lax.broadcasted_iota(jnp.int32, (tq, tq), 0)
    col = lax.broadcasted_iota(jnp.int32, (tq, tq), 1)
    q = q_ref[...] * (FOX_SCALE * LOG2E)
    qa = []
    for hh in range(2):
        own = (lane // hd) == hh
        is_bias = (lane >= bias_lane0[hh]) & (lane < bias_lane0[hh] + 3)
        qa.append(jnp.where(own, q, jnp.where(is_bias, 1.0, 0.0)).astype(BF16))

    def chunk(j, carry, diagonal):
        start = pl.multiple_of(j * tq, tq)
        scores = [jnp.dot(qa[hh], kaug_scr[hh, :, pl.ds(start, tq)], preferred_element_type=F32)
                  for hh in range(2)]
        out = []
        for hh in range(2):
            m, acc = carry[hh]
            s = scores[hh]
            if diagonal:
                s = jnp.where(col <= row, s, -jnp.inf)
            m_new = jnp.maximum(m, jnp.max(s, axis=-1, keepdims=True))
            alpha = jnp.exp2(m - m_new)
            p = jnp.exp2(s - m_new).astype(BF16)
            acc = alpha * acc + jnp.dot(p, vaug_scr[hh, pl.ds(start, tq), :], preferred_element_type=F32)
            out.append((m_new, acc))
        return tuple(out)

    init = tuple((jnp.full((tq, 1), -jnp.inf, F32), jnp.zeros((tq, 2 * hd), F32)) for _ in range(2))
    carry = lax.fori_loop(0, qi, lambda j, c: chunk(j, c, False), init)
    (_, acc0), (_, acc1) = chunk(qi, carry, True)
    out0 = acc0 / acc0[:, ones_lane[0]:ones_lane[0] + 1]
    out1 = acc1 / acc1[:, ones_lane[1]:ones_lane[1] + 1]
    o_ref[...] = jnp.where(lane < hd, out0, out1)


def _fox(q, kt, v, c_bht, nb, t, tq):
    n = q.shape[0]
    nq = t // tq
    hw = 2 * HEAD_DIM
    return pl.pallas_call(
        functools.partial(_fox_kernel, tq=tq),
        grid=(nb, FOX_HEADS // 2, nq),
        in_specs=[
            pl.BlockSpec((tq, hw), lambda b, h, i: (b * nq + i, h)),
            pl.BlockSpec((hw, t), lambda b, h, i: (h, b)),
            pl.BlockSpec((t, hw), lambda b, h, i: (b, h)),
            pl.BlockSpec((1, 2, 1, t), lambda b, h, i: (b, h, 0, 0)),
        ],
        out_specs=pl.BlockSpec((tq, hw), lambda b, h, i: (b * nq + i, h)),
        out_shape=jax.ShapeDtypeStruct((n, FOX_WIDTH), F32),
        scratch_shapes=[pltpu.VMEM((2, hw, t), BF16), pltpu.VMEM((2, t, hw), BF16)],
        compiler_params=_params("arbitrary", "arbitrary", "arbitrary"),
        name="fox_attention",
    )(q, kt, v, c_bht.reshape(nb, FOX_HEADS, 1, t))


def _fox_cached_kernel(q_ref, kn_ref, vn_ref, kc_ref, vc_ref, c_ref, o_ref, *, past, t):
    lane = lax.broadcasted_iota(jnp.int32, (1, 2 * HEAD_DIM), 1)
    row = lax.broadcasted_iota(jnp.int32, (t, t), 0)
    col = lax.broadcasted_iota(jnp.int32, (t, t), 1)
    q = q_ref[...] * FOX_SCALE
    kc = kc_ref[...].astype(BF16)
    vc = vc_ref[...].astype(BF16)
    kn = kn_ref[...].astype(BF16)
    vn = vn_ref[...].astype(BF16)
    outs = []
    for hh in range(2):
        qh = jnp.where(lane // HEAD_DIM == hh, q, 0.0).astype(BF16)
        s1 = lax.dot_general(qh, kc, NT_DIMS, preferred_element_type=F32)
        s1 = s1 - c_ref[0, hh, :, 0:past]
        s2 = lax.dot_general(qh, kn, NT_DIMS, preferred_element_type=F32)
        s2 = s2 - c_ref[0, hh, :, past:past + t]
        s2 = jnp.where(col <= row, s2, -jnp.inf)
        m = jnp.maximum(jnp.max(s1, axis=-1, keepdims=True), jnp.max(s2, axis=-1, keepdims=True))
        p1 = jnp.exp(s1 - m)
        p2 = jnp.exp(s2 - m)
        l = jnp.sum(p1, axis=-1, keepdims=True) + jnp.sum(p2, axis=-1, keepdims=True)
        acc = (jnp.dot(p1.astype(BF16), vc, preferred_element_type=F32)
               + jnp.dot(p2.astype(BF16), vn, preferred_element_type=F32))
        outs.append(acc / l)
    o_ref[...] = jnp.where(lane < HEAD_DIM, outs[0], outs[1])


def _fox_cached(q, k, v, cache_k, cache_v, c_bhl, nb, t):
    n = q.shape[0]
    past = cache_k.shape[0] // nb
    hw = 2 * HEAD_DIM
    new = pl.BlockSpec((t, hw), lambda b, h: (b, h))
    old = pl.BlockSpec((past, hw), lambda b, h: (b, h))
    return pl.pallas_call(
        functools.partial(_fox_cached_kernel, past=past, t=t),
        grid=(nb, FOX_HEADS // 2),
        in_specs=[new, new, new, old, old,
                  pl.BlockSpec((1, 2, 1, past + t), lambda b, h: (b, h, 0, 0))],
        out_specs=new,
        out_shape=jax.ShapeDtypeStruct((n, FOX_WIDTH), F32),
        compiler_params=_params("arbitrary", "arbitrary"),
        name="fox_attention_cached",
    )(q, k, v, cache_k, cache_v, c_bhl.reshape(nb, FOX_HEADS, 1, past + t))


def _levels(chunk):
    out, m = [], chunk // 2
    while m >= 1:
        out.append(m)
        m //= 2
    return out


FINE_LEVEL = 4


def _decay_sum_matrix(chunk):
    r = np.arange(chunk)[:, None]
    j = np.arange(chunk)[None, :]
    incl = (j <= r).astype(np.float32)
    blocks = [incl]
    for m in _levels(chunk):
        if m <= FINE_LEVEL:
            mid = (r // (2 * m)) * (2 * m) + m - 1
            blocks.append(incl - (j <= mid).astype(np.float32))
    return np.concatenate(blocks, axis=0)


def _level_ids(chunk):
    levels = _levels(chunk)
    t = np.arange(chunk)
    x = t[:, None] ^ t[None, :]
    m = np.where(x > 0, 2 ** np.floor(np.log2(np.maximum(x, 1))).astype(np.int64), 0)
    lvl = np.full((chunk, chunk), len(levels) + 1, np.int32)
    for li, lm in enumerate(levels):
        lvl[(t[:, None] > t[None, :]) & (m == lm)] = li
    lvl[x == 0] = len(levels)
    return lvl


def _level_ids_wide(chunk, heads):
    return np.tile(_level_ids(chunk), (1, heads))


def _two_pass(mat_bf16, x):
    hi, lo = _split(x)
    d = lambda u: jnp.dot(mat_bf16, u, preferred_element_type=F32)
    return d(hi) + d(lo)


def _hgrn_kernel(p_ref, lb_ref, ng_ref, s0_ref, dmat_ref, lvl_ref, ones_ref, o_ref, st_ref, st_scr, *, chunk,
                 nchunks, group):
    ti = pl.program_id(1)

    @pl.when(ti == 0)
    def _():
        st_scr[...] = s0_ref[0]

    w = HG_WIDTH
    levels = _levels(chunk)
    lb = lb_ref[...]
    log_lb = jnp.log(lb)
    log_1m_lb = jnp.log1p(-lb)
    row = lax.broadcasted_iota(jnp.int32, (chunk, 1), 0)
    lane_head = lax.broadcasted_iota(jnp.int32, (1, w), 1) // HEAD_DIM
    row_head = lax.broadcasted_iota(jnp.int32, (w, 1), 0) // HEAD_DIM

    head_lanes = [(lane_head == h).astype(BF16) for h in range(HG_HEADS)]

    def stack(x):
        xb = x.astype(BF16)
        return jnp.concatenate([xb * m for m in head_lanes], axis=0)

    def each(fn, *lists):
        return [fn(*xs) for xs in zip(*lists)]

    def chunk_group(gi, _):
        base = gi * (group * chunk)
        rows = [pl.ds(pl.multiple_of(base + i * chunk, chunk), chunk) for i in range(group)]
        q = [p_ref[r, 0:w] for r in rows]
        fx = [p_ref[r, w:2 * w] for r in rows]
        val = [p_ref[r, 2 * w:3 * w] for r in rows]
        bv = each(lambda x: log_1m_lb + _log_sigmoid(x), fx)
        logf = each(lambda x: jnp.maximum(log_lb, x) + jnp.log1p(jnp.exp(-jnp.abs(log_lb - x))), bv)
        kd = each(lambda x: (1.0 - lb) * _sigmoid(-x), fx)
        sums = each(lambda x: _two_pass(dmat_ref[...], x), logf)
        b = [x[0:chunk] for x in sums]

        lvl = lvl_ref[...]
        attn = each(lambda qi, ki: jnp.where(lvl == len(levels), _mm(qi, stack(ki), NT_DIMS), 0.0), q, kd)
        fine = 0
        for li, m in enumerate(levels):
            if m > FINE_LEVEL:
                def rel(bi):
                    mids = [jnp.broadcast_to(bi[s0 + m - 1:s0 + m, :], (2 * m, w)) for s0 in range(0, chunk, 2 * m)]
                    return bi - jnp.concatenate(mids, axis=0)
                x = each(rel, b)
            else:
                fine += 1
                x = [sm[fine * chunk:(fine + 1) * chunk] for sm in sums]
            wgt = each(lambda xi: jnp.exp(-jnp.abs(xi)), x)
            second = ((row // m) % 2) == 1
            ql = each(lambda qi, wi: jnp.where(second, qi * wi, 0.0), q, wgt)
            kl = each(lambda ki, wi: jnp.where(second, 0.0, ki * wi), kd, wgt)
            attn = each(lambda at, qi, ki: jnp.where(lvl == li, _mm(qi, stack(ki), NT_DIMS), at), attn, ql, kl)
        o_intra = each(lambda at, vi: _mm(at, stack(vi)), attn, val)
        q_dec = each(lambda qi, bi: qi * jnp.exp(bi), q, b)
        b_last = [bi[chunk - 1:chunk, :] for bi in b]
        upd = each(lambda vi, ki, bl, bi: _mm(vi, ki * jnp.exp(bl - bi), TN_DIMS), val, kd, b_last, b)

        st = st_scr[...]
        for i in range(group):
            o = _mm(q_dec[i], st, NT_DIMS) + o_intra[i]
            st = jnp.exp(b_last[i]) * st + jnp.where(row_head == lane_head, upd[i], 0.0)
            ms = _head_sum(o * o, ones_ref[...]) * (1.0 / HEAD_DIM)
            g = p_ref[rows[i], 3 * w:4 * w]
            o_ref[rows[i], :] = o * lax.rsqrt(ms + EPS) * ng_ref[...] * (g * _sigmoid(g))
        st_scr[...] = st
        return 0

    lax.fori_loop(0, nchunks // group, chunk_group, 0)

    @pl.when(ti == pl.num_programs(1) - 1)
    def _():
        st_ref[0] = st_scr[...]


def _hgrn(p_hg, lb, norm_g, st0, nb, t, chunk, tile):
    n = p_hg.shape[0]
    nt = t // tile
    w = HG_WIDTH
    dmat = jnp.asarray(_decay_sum_matrix(chunk), BF16)
    lvl = jnp.asarray(_level_ids_wide(chunk, HG_HEADS))
    ones = jnp.asarray(_block_ones_np(w), BF16)
    return pl.pallas_call(
        functools.partial(_hgrn_kernel, chunk=chunk, nchunks=tile // chunk, group=min(4, tile // chunk)),
        grid=(nb, nt),
        in_specs=[
            pl.BlockSpec((tile, HG_COLS), lambda b, i: (b * nt + i, 0)),
            _const_spec((1, w)), _const_spec((1, w)),
            pl.BlockSpec((1, w, w), lambda b, i: (b, 0, 0)),
            _const_spec(dmat.shape), _const_spec(lvl.shape), _const_spec(ones.shape),
        ],
        out_specs=[pl.BlockSpec((tile, w), lambda b, i: (b * nt + i, 0)),
                   pl.BlockSpec((1, w, w), lambda b, i: (b, 0, 0))],
        out_shape=[jax.ShapeDtypeStruct((n, w), F32), jax.ShapeDtypeStruct((nb, w, w), F32)],
        scratch_shapes=[pltpu.VMEM((w, w), F32)],
        compiler_params=_params("arbitrary", "arbitrary"),
        name="hgrn2",
    )(p_hg, lb, norm_g, st0, dmat, lvl, ones)


def _rwkv_kernel(p_ref, prev_ref, s0_ref, mu_ref, vec_ref, w2_ref, a2_ref, g2_ref, ltri_ref, ones_ref,
                 lvlw_ref, y_ref, st_ref, st_scr, prev_scr, *, chunk, nchunks, group):
    ti = pl.program_id(1)

    @pl.when(ti == 0)
    def _():
        st_scr[...] = s0_ref[0]
        prev_scr[...] = prev_ref[0]

    w = RW_WIDTH
    hc = RW_HEADS * chunk
    levels = _levels(chunk)
    nl = len(levels)
    w0 = vec_ref[0:1, :]
    a0 = vec_ref[1:2, :]
    k_k = vec_ref[2:3, :]
    k_a = vec_ref[3:4, :]
    r_k = vec_ref[4:5, :]
    ln_w = vec_ref[5:6, :]
    ln_b = vec_ref[6:7, :]
    row = lax.broadcasted_iota(jnp.int32, (chunk, 1), 0)
    lane_head = lax.broadcasted_iota(jnp.int32, (1, w), 1) // HEAD_DIM
    row_id = lax.broadcasted_iota(jnp.int32, (w, 1), 0)
    lane_id = lax.broadcasted_iota(jnp.int32, (1, w), 1)
    row_head = row_id // HEAD_DIM
    wide_head = lax.broadcasted_iota(jnp.int32, (1, hc), 1) // chunk

    head_lanes = [(lane_head == h).astype(BF16) for h in range(RW_HEADS)]
    head_lanes2 = [jnp.concatenate([m, m], axis=1) for m in head_lanes]
    wide_lanes = [(wide_head == h).astype(BF16) for h in range(RW_HEADS)]

    def stack(x):
        xb = x.astype(BF16)
        masks = head_lanes if x.shape[1] == w else head_lanes2
        return jnp.concatenate([xb * m for m in masks], axis=0)

    def blockdiag(xw):
        xb = xw.astype(BF16)
        return jnp.concatenate([xb * m for m in wide_lanes], axis=0)

    def each(fn, *lists):
        return [fn(*xs) for xs in zip(*lists)]

    def chunk_group(gi, _):
        base = gi * (group * chunk)
        rows = [pl.ds(pl.multiple_of(base + i * chunk, chunk), chunk) for i in range(group)]
        ones_bd = ones_ref[...]
        mu = mu_ref[...]
        ps = [p_ref[r, :] for r in rows]
        prevs = [prev_scr[...]] + [p[chunk - 1:chunk, :] for p in ps[:-1]]
        prev_scr[...] = ps[-1][chunk - 1:chunk, :]
        xs = each(lambda p, pv: p + mu * (jnp.where(row == 0, pv, pltpu.roll(p, 1, 0)) - p), ps, prevs)
        r = [x[:, 0:w] for x in xs]
        k = [x[:, w:2 * w] for x in xs]
        v = [x[:, 2 * w:3 * w] for x in xs]
        tail = [x[:, 3 * w:] for x in xs]
        w_lin = each(lambda t: w0 + _mm3(jnp.tanh(t), w2_ref[...]), tail)
        a = each(lambda t: _sigmoid(a0 + _mm(t, a2_ref[...])), tail)
        g = each(lambda t: _mm(_sigmoid(t), g2_ref[...]), tail)
        logw = each(lambda wl: -jnp.exp(-(jnp.maximum(-wl, 0.0) + jnp.log1p(jnp.exp(-jnp.abs(wl)))) - 0.5), w_lin)
        kk = each(lambda x: x * k_k, k)
        kk = each(lambda x: x / jnp.maximum(jnp.sqrt(_head_sum(x * x, ones_bd)), 1e-12), kk)
        kmod = each(lambda x, ai: x * (1.0 + (ai - 1.0) * k_a), k, a)
        beta = each(lambda x, ai: x * ai, kk, a)

        lg = each(lambda x: _two_pass(ltri_ref[...], x), logw)
        lg_end = [x[chunk - 1:chunk, :] for x in lg]
        a_n = each(lambda x, l, lw: x * jnp.exp(l - lw), kk, lg, logw)
        r_n = each(lambda x, l: x * jnp.exp(l), r, lg)
        inv_g = each(lambda l: jnp.exp(-l), lg)
        to_end = each(lambda le, l: jnp.exp(le - l), lg_end, lg)
        v_s = each(stack, v)

        gram = each(lambda an, rn, be, km, ig: _mm(jnp.concatenate([an, rn], axis=0),
                                                   jnp.concatenate([stack(be * ig), stack(km * ig)], axis=0),
                                                   NT_DIMS), a_n, r_n, beta, kmod, inv_g)
        lvlw = lvlw_ref[...]
        strict = lvlw < nl
        incl = lvlw <= nl
        n_w = [jnp.where(strict, x[0:chunk, 0:hc], 0.0).astype(BF16) for x in gram]
        m_k = [jnp.where(strict, x[0:chunk, hc:2 * hc], 0.0) for x in gram]
        l_b = [jnp.where(incl, x[chunk:2 * chunk, 0:hc], 0.0) for x in gram]
        l_k = [jnp.where(incl, x[chunk:2 * chunk, hc:2 * hc], 0.0) for x in gram]

        t_w = each(lambda n: (jnp.where(lvlw == nl, 1.0, 0.0)
                              - jnp.where(lvlw == nl - 1, n, jnp.zeros_like(n)).astype(F32)), n_w)
        blk = 2
        while blk < chunk:
            li = levels.index(blk)
            level = (lvlw == li).astype(BF16)
            x_w = each(lambda t, n: _mm(t, blockdiag(n * level)), t_w, n_w)
            t_w = each(lambda t, x: t - _mm(x, blockdiag(t)), t_w, x_w)
            blk *= 2

        rhs = each(lambda an, mk, vs: jnp.concatenate([an, _mm(mk, vs)], axis=1), a_n, m_k, v_s)
        x0 = each(lambda t, rh: _mm(t, stack(rh)), t_w, rhs)

        def residual(n, rh, x):
            x_hi, x_lo = _split(x)
            return rh - x - (jnp.dot(n, stack(x_hi), preferred_element_type=F32)
                             + jnp.dot(n, stack(x_lo), preferred_element_type=F32))

        res = each(residual, n_w, rhs, x0)
        sol = each(lambda x, t, rs: x + _mm(t, stack(rs)), x0, t_w, res)
        a_hat = [x[:, 0:w] for x in sol]
        e0 = [x[:, w:2 * w] for x in sol]
        lb_x = each(lambda lb, x: _mm(lb, stack(x)), l_b, sol)
        r_hat = each(lambda rn, lx: rn - lx[:, 0:w], r_n, lb_x)
        y0 = each(lambda lk, vs, lx: _mm(lk, vs) - lx[:, w:2 * w], l_k, v_s, lb_x)
        same_head = row_head == lane_head
        bh = each(lambda be, te: be * te, beta, to_end)
        kh = each(lambda km, te: km * te, kmod, to_end)
        trans = each(lambda ah, b_, le: (jnp.where(row_id == lane_id, jnp.exp(le), 0.0)
                                         - jnp.where(same_head, _mm(ah, b_, TN_DIMS), 0.0)), a_hat, bh, lg_end)
        q0 = each(lambda vi, ei, k_, b_: jnp.where(same_head, _mm(jnp.concatenate([vi, ei], axis=0),
                                                                   jnp.concatenate([k_, -b_], axis=0), TN_DIMS), 0.0),
                  v, e0, kh, bh)

        st = st_scr[...]
        for i in range(group):
            y = _mm(r_hat[i], st, NT_DIMS) + y0[i]
            st = _mm(st, trans[i]) + q0[i]

            inv_n = 1.0 / HEAD_DIM
            mean = _head_sum(y, ones_bd) * inv_n
            yc = y - mean
            var = _head_sum(yc * yc, ones_bd) * inv_n
            yn = yc * lax.rsqrt(var + RW_GN_EPS) * ln_w + ln_b
            bonus = _head_sum(r[i] * kmod[i] * r_k, ones_bd) * v[i]
            y_ref[rows[i], :] = (yn + bonus) * g[i]
        st_scr[...] = st
        return 0

    lax.fori_loop(0, nchunks // group, chunk_group, 0)

    @pl.when(ti == pl.num_programs(1) - 1)
    def _():
        st_ref[0] = st_scr[...]


def _rwkv(p_rw, prev, st0, mu, vecs, w2p, a2p, g2p, nb, t, chunk, tile):
    n = p_rw.shape[0]
    nt = t // tile
    w = RW_WIDTH
    consts = (jnp.asarray(np.tril(np.ones((chunk, chunk), np.float32)), BF16),
              jnp.asarray(_block_ones_np(w), BF16),
              jnp.asarray(_level_ids_wide(chunk, RW_HEADS)))
    return pl.pallas_call(
        functools.partial(_rwkv_kernel, chunk=chunk, nchunks=tile // chunk, group=min(8, tile // chunk)),
        grid=(nb, nt),
        in_specs=[
            pl.BlockSpec((tile, RW_COLS), lambda b, i: (b * nt + i, 0)),
            pl.BlockSpec((1, 1, RW_COLS), lambda b, i: (b, 0, 0)),
            pl.BlockSpec((1, w, w), lambda b, i: (b, 0, 0)),
            _const_spec((1, RW_COLS)), _const_spec((8, w)),
            _const_spec((RW_LORA, w)), _const_spec((RW_LORA, w)), _const_spec((RW_LORA, w)),
        ] + [_const_spec(c.shape) for c in consts],
        out_specs=[pl.BlockSpec((tile, w), lambda b, i: (b * nt + i, 0)),
                   pl.BlockSpec((1, w, w), lambda b, i: (b, 0, 0))],
        out_shape=[jax.ShapeDtypeStruct((n, w), F32), jax.ShapeDtypeStruct((nb, w, w), F32)],
        scratch_shapes=[pltpu.VMEM((w, w), F32), pltpu.VMEM((1, RW_COLS), F32)],
        compiler_params=_params("arbitrary", "arbitrary"),
        name="rwkv7",
    )(p_rw, prev, st0, mu, vecs, w2p, a2p, g2p, *consts)


def _outffn_kernel(yrw_ref, yfox_ref, yhg_ref, x_ref, ga1_ref, sh2_ref, sc2_ref, ga2_ref, g2_ref,
                   worw_ref, wofox_ref, wohg_ref, wg_ref, wu_ref, wo_ref, fg_ref, o_ref, *, final_norm, ff_step):
    mix = (jnp.dot(yrw_ref[...].astype(BF16), worw_ref[...], preferred_element_type=F32)
           + jnp.dot(yfox_ref[...].astype(BF16), wofox_ref[...], preferred_element_type=F32)
           + jnp.dot(yhg_ref[...].astype(BF16), wohg_ref[...], preferred_element_type=F32))
    x1 = x_ref[...] + ga1_ref[0] * mix
    ms = jnp.mean(x1 * x1, axis=-1, keepdims=True)
    h2 = x1 * lax.rsqrt(ms + EPS) * g2_ref[...]
    h2 = (h2 * (1.0 + sc2_ref[0]) + sh2_ref[0]).astype(BF16)
    ffn = jnp.zeros_like(x1)
    for j in range(D_FF // ff_step):
        cols = slice(j * ff_step, (j + 1) * ff_step)
        gate = jnp.dot(h2, wg_ref[:, cols], preferred_element_type=F32)
        up = jnp.dot(h2, wu_ref[:, cols], preferred_element_type=F32)
        act = (gate * _sigmoid(gate) * up).astype(BF16)
        ffn = ffn + jnp.dot(act, wo_ref[cols, :], preferred_element_type=F32)
    x2 = x1 + ga2_ref[0] * ffn
    if final_norm:
        ms2 = jnp.mean(x2 * x2, axis=-1, keepdims=True)
        x2 = x2 * lax.rsqrt(ms2 + EPS) * fg_ref[...]
    o_ref[...] = x2


def _outffn(y_rw, y_fox, y_hg, x, ga1, sh2, sc2, ga2, g2, wts, final_g, final_norm, tm, tiles_per_batch):
    n = x.shape[0]
    row = lambda w: pl.BlockSpec((tm, w), lambda i: (i, 0))
    mods = [_mod_spec(m, tiles_per_batch) for m in (ga1, sh2, sc2, ga2)]
    return pl.pallas_call(
        functools.partial(_outffn_kernel, final_norm=final_norm, ff_step=D_FF // 2),
        grid=(n // tm,),
        in_specs=[row(RW_WIDTH), row(FOX_WIDTH), row(HG_WIDTH), row(D_MODEL)] + mods
        + [_const_spec((1, D_MODEL))] + [_const_spec(w.shape) for w in wts] + [_const_spec((1, D_MODEL))],
        out_specs=row(D_MODEL),
        out_shape=jax.ShapeDtypeStruct((n, D_MODEL), F32),
        compiler_params=_params("arbitrary"),
        name="outffn",
    )(y_rw, y_fox, y_hg, x, ga1, sh2, sc2, ga2, g2, *wts, final_g)


def _block_diag_t(s):
    nb, h, dk, dv = s.shape
    st = jnp.swapaxes(s, 2, 3)
    eye = jnp.eye(h, dtype=s.dtype)
    return jnp.einsum("bhvk,hg->bhvgk", st, eye).reshape(nb, h * dv, h * dk)


def _block_diag(s):
    nb, h, dv, dk = s.shape
    eye = jnp.eye(h, dtype=s.dtype)
    return jnp.einsum("bhvk,hg->bhvgk", s, eye).reshape(nb, h * dv, h * dk)


def _diag_blocks(sbd, h):
    nb, rows, cols = sbd.shape
    x = sbd.reshape(nb, h, rows // h, h, cols // h)
    return jnp.stack([x[:, i, :, i, :] for i in range(h)], axis=1)


def _layer_weights(l, w_in, rw_w2, rw_a2, rw_g2, w_out, w_ffn_in, w_ffn_out):
    wi = w_in[l]
    c0, c1 = RW_COLS, RW_COLS + 3 * FOX_WIDTH
    proj = (wi[:, :c0].astype(BF16), wi[:, c0:c1].astype(BF16),
            wi[:, c0 + FOX_WIDTH:c0 + 2 * FOX_WIDTH].T.astype(BF16),
            wi[:, c1:c1 + FOX_HEADS].T.astype(BF16), wi[:, c1 + FOX_HEADS:].astype(BF16))
    zeros = lambda r: jnp.zeros((r, RW_WIDTH), F32)
    w2p = jnp.concatenate([rw_w2[l], zeros(RW_LORA - RW_W_LORA)], axis=0)
    a2p = jnp.concatenate([zeros(RW_W_LORA), rw_a2[l], zeros(RW_G_LORA)], axis=0)
    g2p = jnp.concatenate([zeros(RW_W_LORA + RW_A_LORA), rw_g2[l]], axis=0)
    wo = w_out[l].astype(BF16)
    ffn = (wo[:RW_WIDTH], wo[RW_WIDTH:RW_WIDTH + FOX_WIDTH], wo[RW_WIDTH + FOX_WIDTH:],
           w_ffn_in[l][:, :D_FF].astype(BF16), w_ffn_in[l][:, D_FF:].astype(BF16), w_ffn_out[l].astype(BF16))
    return proj, (w2p, a2p, g2p), ffn


def _group_layer(x, mods, lw, lp, prev, rw_st0, hg_st0, cache, nb, t, cfg, final_g, final_norm):
    proj_w, lora_w, ffn_w = lw
    sh1, sc1, ga1, sh2, sc2, ga2 = mods
    tm, tpb = cfg["tm"], cfg["tiles_per_batch"]
    p_rw, q, k, v, kt, f, p_hg = _normproj(x, sh1, sc1, lp["norm1_g"], proj_w, tm, tpb, cache is None)
    if f.shape[0] == nb:
        f_bht = f
    else:
        f_bht = jnp.swapaxes(f.reshape(FOX_HEADS, nb, t), 0, 1)
    if cache is None:
        logf_bht, c_bhl = _gate(f_bht, lp["fox_b_f"], None)
        y_fox = _fox(q, kt, v, c_bhl, nb, t, cfg["tq"])
    else:
        cache_k, cache_v, cache_logf = cache
        past = cache_k.shape[1]
        logf_bht, c_bhl = _gate(f_bht, lp["fox_b_f"], jnp.swapaxes(cache_logf, 1, 2))
        y_fox = _fox_cached(q, k, v, cache_k.reshape(nb * past, FOX_WIDTH),
                            cache_v.reshape(nb * past, FOX_WIDTH), c_bhl, nb, t)
    y_hg, hg_st = _hgrn(p_hg, lp["hg_lb"], lp["hg_norm_g"], hg_st0, nb, t, cfg["hg_chunk"], cfg["hg_tile"])
    y_rw, rw_st = _rwkv(p_rw, prev, rw_st0, lp["rw_mu"], lp["rw_vecs"], *lora_w, nb, t,
                        cfg["rw_chunk"], cfg["rw_tile"])
    x = _outffn(y_rw, y_fox, y_hg, x, ga1, sh2, sc2, ga2, lp["norm2_g"], ffn_w, final_g, final_norm, tm, tpb)
    states = (
        k.reshape(nb, t, FOX_HEADS, HEAD_DIM), v.reshape(nb, t, FOX_HEADS, HEAD_DIM),
        jnp.swapaxes(logf_bht, 1, 2),
        _diag_blocks(rw_st, RW_HEADS),
        p_rw.reshape(nb, t, RW_COLS)[:, t - 1:t, :],
        jnp.swapaxes(_diag_blocks(hg_st, HG_HEADS), 2, 3),
    )
    return x, states


def kernel(x_prompt, x_sample, c_prompt, c_sample, cache_fox_k, cache_fox_v, cache_fox_logf, state_rwkv,
           state_rwkv_shift, state_hgrn, norm1_g, w_ada, b_ada, w_in, rw_mu, rw_w0, rw_w2, rw_a0, rw_a2, rw_g2,
           rw_k_k, rw_k_a, rw_r_k, rw_ln_w, rw_ln_b, fox_b_f, hg_lb_logits, hg_norm_g, w_out, norm2_g,
           w_ffn_in, w_ffn_out, final_norm_g):
    dt = x_prompt.dtype
    bp, tp, _ = x_prompt.shape
    bs, ts, _ = x_sample.shape

    mod = _modulation(jnp.concatenate([c_prompt, c_sample], axis=0).astype(F32), w_ada, b_ada)
    lbs = _hg_lower_bounds(hg_lb_logits.astype(F32))

    cfg_p = dict(tm=512, tiles_per_batch=tp // 512, tq=512, hg_chunk=HG_CHUNK, hg_tile=512,
                 rw_chunk=RW_CHUNK, rw_tile=512)
    cfg_s = dict(tm=bs * ts, tiles_per_batch=1, hg_chunk=ts, hg_tile=ts, rw_chunk=ts, rw_tile=ts)

    xp = x_prompt.reshape(bp * tp, D_MODEL).astype(F32)
    xs = x_sample.reshape(bs * ts, D_MODEL).astype(F32)
    zero_prev = jnp.zeros((bp, 1, RW_COLS), F32)
    zero_state = jnp.zeros((bp, RW_WIDTH, RW_WIDTH), F32)
    final_g = final_norm_g.reshape(1, D_MODEL)
    row = lambda a: a.reshape(1, -1)

    outs_p, outs_s = [], []
    for l in range(DEPTH):
        lw = _layer_weights(l, w_in, rw_w2, rw_a2, rw_g2, w_out, w_ffn_in, w_ffn_out)
        zero_row = jnp.zeros((RW_WIDTH,), F32)
        lp = dict(
            norm1_g=row(norm1_g[l]), norm2_g=row(norm2_g[l]), fox_b_f=fox_b_f[l],
            hg_lb=row(lbs[l]), hg_norm_g=row(hg_norm_g[l]), rw_mu=row(rw_mu[l]),
            rw_vecs=jnp.stack([rw_w0[l], rw_a0[l], rw_k_k[l], rw_k_a[l], rw_r_k[l].reshape(-1),
                               rw_ln_w[l], rw_ln_b[l], zero_row], axis=0),
        )
        chunks_p = [m[:, None, :] for m in jnp.split(mod[l, :bp], 6, axis=-1)]
        chunks_s = [jnp.repeat(m, ts, axis=0)[None] for m in jnp.split(mod[l, bp:], 6, axis=-1)]
        last = l == DEPTH - 1
        xp, st_p = _group_layer(xp, chunks_p, lw, lp, zero_prev, zero_state, zero_state, None,
                                bp, tp, cfg_p, final_g, last)
        xs, st_s = _group_layer(xs, chunks_s, lw, lp, state_rwkv_shift[l].astype(F32),
                                _block_diag(state_rwkv[l].astype(F32)),
                                _block_diag_t(state_hgrn[l].astype(F32)),
                                (cache_fox_k[l].astype(F32), cache_fox_v[l].astype(F32),
                                 cache_fox_logf[l].astype(F32)),
                                bs, ts, cfg_s, final_g, last)
        outs_p.append(st_p)
        outs_s.append(st_s)

    stk = lambda outs, j: jnp.stack([o[j] for o in outs], axis=0).astype(dt)
    y_prompt = xp.reshape(bp, tp, D_MODEL).astype(dt)
    y_sample = xs.reshape(bs, ts, D_MODEL).astype(dt)
    return (y_prompt, y_sample,
            stk(outs_p, 0), stk(outs_p, 1), stk(outs_p, 2), stk(outs_p, 3), stk(outs_p, 4), stk(outs_p, 5),
            stk(outs_s, 0), stk(outs_s, 1), stk(outs_s, 2), stk(outs_s, 3), stk(outs_s, 4), stk(outs_s, 5))
```

```python
import functools

import numpy as np
import jax
import jax.numpy as jnp
from jax import lax
from jax.experimental import pallas as pl
from jax.experimental.pallas import tpu as pltpu

F32 = jnp.float32
BF16 = jnp.bfloat16
HIGHEST = lax.Precision.HIGHEST

D_MODEL = 1024
DEPTH = 4
HEAD_DIM = 64
RW_HEADS = 4
RW_WIDTH = RW_HEADS * HEAD_DIM
FOX_HEADS = 8
FOX_WIDTH = FOX_HEADS * HEAD_DIM
HG_HEADS = 4
HG_WIDTH = HG_HEADS * HEAD_DIM
RW_W_LORA = 32
RW_A_LORA = 32
RW_G_LORA = 64
RW_LORA = RW_W_LORA + RW_A_LORA + RW_G_LORA
RW_COLS = 3 * RW_WIDTH + RW_LORA
FOX_COLS = 3 * FOX_WIDTH + FOX_HEADS
HG_COLS = 4 * HG_WIDTH
D_FF = 2816
EPS = 1e-6
RW_GN_EPS = 64e-5
FOX_SCALE = HEAD_DIM ** -0.5
HG_CHUNK = 64
RW_CHUNK = 64

VMEM_LIMIT_BYTES = 56 * 1024 * 1024

NT_DIMS = (((1,), (1,)), ((), ()))
TN_DIMS = (((0,), (0,)), ((), ()))
NN_DIMS = (((1,), (0,)), ((), ()))


def _params(*sem):
    return pltpu.CompilerParams(dimension_semantics=sem, vmem_limit_bytes=VMEM_LIMIT_BYTES)


def _const_spec(shape):
    nd = len(shape)
    return pl.BlockSpec(shape, lambda *_: (0,) * nd, pipeline_mode=pl.Buffered(1))


def _mm(a, b, dims=NN_DIMS):
    return lax.dot_general(a.astype(BF16), b.astype(BF16), dims, preferred_element_type=F32)


def _split(a):
    hi = a.astype(BF16)
    lo = (a - hi.astype(F32)).astype(BF16)
    return hi, lo


def _mm3(a, b, dims=NN_DIMS):
    ah, al = _split(a)
    bh, bl = _split(b)
    d = lambda x, y: lax.dot_general(x, y, dims, preferred_element_type=F32)
    return d(ah, bh) + (d(ah, bl) + d(al, bh))


def _head_sum(x, ones_bd):
    hi, lo = _split(x)
    d = lambda u: jnp.dot(u, ones_bd, preferred_element_type=F32)
    return d(hi) + d(lo)


def _block_ones_np(width):
    r = np.arange(width) // HEAD_DIM
    return (r[:, None] == r[None, :]).astype(np.float32)


def _log_sigmoid(z):
    return jnp.minimum(z, 0.0) - jnp.log1p(jnp.exp(-jnp.abs(z)))


def _sigmoid(z):
    return 1.0 / (1.0 + jnp.exp(-z))


def _mod_kernel(c_ref, w_ref, b_ref, o_ref):
    c = c_ref[...]
    s = c * _sigmoid(c)
    o_ref[0] = jnp.dot(s, w_ref[0], precision=HIGHEST, preferred_element_type=F32) + b_ref[0]


def _modulation(c_all, w_ada, b_ada):
    nb = c_all.shape[0]
    tn = 1536
    n_out = w_ada.shape[2]
    return pl.pallas_call(
        _mod_kernel,
        grid=(DEPTH, n_out // tn),
        in_specs=[
            pl.BlockSpec((nb, D_MODEL), lambda l, j: (0, 0)),
            pl.BlockSpec((1, D_MODEL, tn), lambda l, j: (l, 0, j)),
            pl.BlockSpec((1, 1, tn), lambda l, j: (l, 0, j)),
        ],
        out_specs=pl.BlockSpec((1, nb, tn), lambda l, j: (l, 0, j)),
        out_shape=jax.ShapeDtypeStruct((DEPTH, nb, n_out), F32),
        compiler_params=_params("arbitrary", "arbitrary"),
        name="adaln_mod",
    )(c_all, w_ada, b_ada.reshape(DEPTH, 1, n_out))


def _lbs_kernel(x_ref, o_ref):
    x = x_ref[...]
    m = jnp.max(x, axis=0, keepdims=True)
    e = jnp.exp(x - m)
    p = e / jnp.sum(e, axis=0, keepdims=True)
    acc = jnp.zeros_like(p[0:1])
    rows = []
    for l in range(DEPTH):
        acc = acc + p[l:l + 1]
        rows.append(acc)
    cs = jnp.concatenate(rows, axis=0)
    o_ref[...] = cs - cs[0:1]


def _hg_lower_bounds(logits):
    return pl.pallas_call(
        _lbs_kernel,
        out_shape=jax.ShapeDtypeStruct(logits.shape, F32),
        name="hgrn_lower_bounds",
    )(logits)


def _normproj_kernel(x_ref, sh_ref, sc_ref, g_ref, wrw_ref, wqkv_ref, wkt_ref, wft_ref, whg_ref,
                     prw_ref, q_ref, k_ref, v_ref, kt_ref, ft_ref, phg_ref, *, emit_kt):
    x = x_ref[...]
    ms = jnp.mean(x * x, axis=-1, keepdims=True)
    h = x * lax.rsqrt(ms + EPS) * g_ref[...]
    h = h * (1.0 + sc_ref[0]) + sh_ref[0]
    hb = h.astype(BF16)
    prw_ref[...] = jnp.dot(hb, wrw_ref[...], preferred_element_type=F32)
    qkv = jnp.dot(hb, wqkv_ref[...], preferred_element_type=F32)
    q_ref[...] = qkv[:, :FOX_WIDTH]
    k_ref[...] = qkv[:, FOX_WIDTH:2 * FOX_WIDTH]
    v_ref[...] = qkv[:, 2 * FOX_WIDTH:]
    if emit_kt:
        kt_ref[...] = lax.dot_general(wkt_ref[...], hb, NT_DIMS, preferred_element_type=F32).astype(BF16)
    else:
        kt_ref[...] = jnp.zeros(kt_ref.shape, BF16)
    ft_ref[0] = lax.dot_general(wft_ref[...], hb, NT_DIMS, preferred_element_type=F32)
    phg_ref[...] = jnp.dot(hb, whg_ref[...], preferred_element_type=F32)


def _mod_spec(arr, tiles_per_batch):
    _, r, d = arr.shape
    return pl.BlockSpec((1, r, d), lambda i: (i // tiles_per_batch, 0, 0))


def _normproj(x, sh, sc, g, wts, tm, tiles_per_batch, emit_kt):
    n = x.shape[0]
    wrw, wqkv, wkt, wft, whg = wts
    row = lambda w: pl.BlockSpec((tm, w), lambda i: (i, 0))
    nbat = n // (tm * tiles_per_batch)
    kt_shape = (FOX_WIDTH, n) if emit_kt else (8, 128)
    kt_spec = (pl.BlockSpec((FOX_WIDTH, tm), lambda i: (0, i)) if emit_kt
               else pl.BlockSpec((8, 128), lambda i: (0, 0)))
    return pl.pallas_call(
        functools.partial(_normproj_kernel, emit_kt=emit_kt),
        grid=(n // tm,),
        in_specs=[
            row(D_MODEL), _mod_spec(sh, tiles_per_batch), _mod_spec(sc, tiles_per_batch),
            _const_spec((1, D_MODEL)),
            _const_spec(wrw.shape), _const_spec(wqkv.shape), _const_spec(wkt.shape), _const_spec(wft.shape),
            _const_spec(whg.shape),
        ],
        out_specs=[row(RW_COLS), row(FOX_WIDTH), row(FOX_WIDTH), row(FOX_WIDTH), kt_spec,
                   pl.BlockSpec((1, FOX_HEADS, tm), lambda i: (i // tiles_per_batch, 0, i % tiles_per_batch)),
                   row(HG_COLS)],
        out_shape=[
            jax.ShapeDtypeStruct((n, RW_COLS), F32),
            jax.ShapeDtypeStruct((n, FOX_WIDTH), F32),
            jax.ShapeDtypeStruct((n, FOX_WIDTH), F32),
            jax.ShapeDtypeStruct((n, FOX_WIDTH), F32),
            jax.ShapeDtypeStruct(kt_shape, BF16),
            jax.ShapeDtypeStruct((nbat, FOX_HEADS, tm * tiles_per_batch), F32),
            jax.ShapeDtypeStruct((n, HG_COLS), F32),
        ],
        compiler_params=_params("arbitrary"),
        name="normproj",
    )(x, sh, sc, g, wrw, wqkv, wkt, wft, whg)


def _cumsum_lanes(x, carry):
    length = x.shape[1]
    step = min(512, length)
    r = lax.broadcasted_iota(jnp.int32, (step, step), 0)
    c = lax.broadcasted_iota(jnp.int32, (step, step), 1)
    upper = (r <= c).astype(F32)
    outs = []
    for j in range(length // step):
        seg = x[:, j * step:(j + 1) * step]
        cs = jnp.dot(seg, upper, precision=HIGHEST, preferred_element_type=F32) + carry
        carry = cs[:, step - 1:step]
        outs.append(cs)
    return outs, carry, step


def _gate_kernel(*refs, past):
    if past:
        f_ref, bf_ref, cache_ref, logf_ref, c_ref = refs
    else:
        f_ref, bf_ref, logf_ref, c_ref = refs
    logf = _log_sigmoid(f_ref[0] + bf_ref[...])
    logf_ref[0] = logf
    carry = jnp.zeros((FOX_HEADS, 1), F32)
    if past:
        outs, carry, step = _cumsum_lanes(cache_ref[0], carry)
        for j, cs in enumerate(outs):
            c_ref[0, :, j * step:(j + 1) * step] = cs
    outs, carry, step = _cumsum_lanes(logf, carry)
    for j, cs in enumerate(outs):
        c_ref[0, :, past + j * step:past + (j + 1) * step] = cs


def _gate(f_bht, b_f, cache_logf_bhp):
    nb, _, t = f_bht.shape
    past = 0 if cache_logf_bhp is None else cache_logf_bhp.shape[2]
    blk = lambda w: pl.BlockSpec((1, FOX_HEADS, w), lambda b: (b, 0, 0))
    in_specs = [blk(t), _const_spec((FOX_HEADS, 1))]
    args = [f_bht, b_f.reshape(FOX_HEADS, 1)]
    if past:
        in_specs.append(blk(past))
        args.append(cache_logf_bhp)
    return pl.pallas_call(
        functools.partial(_gate_kernel, past=past),
        grid=(nb,),
        in_specs=in_specs,
        out_specs=[blk(t), blk(past + t)],
        out_shape=[jax.ShapeDtypeStruct((nb, FOX_HEADS, t), F32),
                   jax.ShapeDtypeStruct((nb, FOX_HEADS, past + t), F32)],
        compiler_params=_params("arbitrary"),
        name="fox_gate",
    )(*args)


LOG2E = 1.4426950408889634
BIAS_ROWS = 16
DIAG_SPLIT = 4
SKIP_MARGIN = 162.0


def _fox_kernel(q_ref, kt_ref, v_ref, c_ref, o_ref, kaug_scr, vaug_scr, kmax_scr, bmax_scr, *, tq):
    qi = pl.program_id(2)
    hd = HEAD_DIM
    nq = kaug_scr.shape[2] // tq
    lane = lax.broadcasted_iota(jnp.int32, (1, 2 * hd), 1)
    bias_lane0 = (hd, 0)
    ones_lane = (hd, 0)

    @pl.when(qi == 0)
    def _():
        kt = kt_ref[...]
        v = v_ref[...]
        t = kt.shape[1]
        for hh in range(2):
            c2 = c_ref[0, hh] * LOG2E
            hi = c2.astype(BF16).astype(F32)
            mid = (c2 - hi).astype(BF16).astype(F32)
            lo = c2 - hi - mid
            bias = jnp.concatenate([-hi, -mid, -lo, jnp.zeros((BIAS_ROWS - 3, t), F32)], axis=0).astype(BF16)
            pad = jnp.zeros((hd - BIAS_ROWS, t), BF16)
            if hh == 0:
                kaug_scr[hh] = jnp.concatenate([kt[0:hd], bias, pad], axis=0)
            else:
                kaug_scr[hh] = jnp.concatenate([bias, pad, kt[hd:2 * hd]], axis=0)
            own = (lane // hd) == hh
            vaug_scr[hh] = jnp.where(own, v, jnp.where(lane == ones_lane[hh], 1.0, 0.0)).astype(BF16)
            kh = kt[hh * hd:(hh + 1) * hd].astype(F32)
            knorm = jnp.sqrt(jnp.sum(kh * kh, axis=0, keepdims=True))
            bsum = -(hi + mid + lo.astype(BF16).astype(F32))
            for j in range(nq):
                kmax_scr[hh, j] = jnp.max(knorm[:, j * tq:(j + 1) * tq])
                bmax_scr[hh, j] = jnp.max(bsum[:, j * tq:(j + 1) * tq])

    q = q_ref[...] * (FOX_SCALE * LOG2E)
    qa = []
    for hh in range(2):
        own = (lane // hd) == hh
        is_bias = (lane >= bias_lane0[hh]) & (lane < bias_lane0[hh] + 3)
        qa.append(jnp.where(own, q, jnp.where(is_bias, 1.0, 0.0)).astype(BF16))

    def scores(qh, hh, start, nk):
        return jnp.dot(qh, kaug_scr[hh, :, pl.ds(start, nk)], preferred_element_type=F32)

    def update(s, hh, start, m, acc, row0):
        if row0 is not None:
            rows_i = lax.broadcasted_iota(jnp.int32, s.shape, 0) + row0
            cols_i = lax.broadcasted_iota(jnp.int32, s.shape, 1)
            s = jnp.where(cols_i <= rows_i, s, -jnp.inf)
        width = 2 * hd
        part = s
        if s.shape[1] % width == 0:
            part = s[:, 0:width]
            for c0 in range(width, s.shape[1], width):
                part = jnp.maximum(part, s[:, c0:c0 + width])
        m_new = jnp.maximum(m, jnp.max(part, axis=-1, keepdims=True))
        alpha = jnp.exp2(m - m_new)
        p = jnp.exp2(s - m_new).astype(BF16)
        acc = alpha * acc + jnp.dot(p, vaug_scr[hh, pl.ds(start, s.shape[1]), :], preferred_element_type=F32)
        return m_new, acc

    def chunk(j, carry):
        start = pl.multiple_of(j * tq, tq)
        sc = [scores(qa[hh], hh, start, tq) for hh in range(2)]
        return tuple(update(sc[hh], hh, start, *carry[hh], None) for hh in range(2))

    start = pl.multiple_of(qi * tq, tq)
    hr = tq // DIAG_SPLIT
    blocks = [(hh, r) for hh in range(2) for r in range(DIAG_SPLIT)]
    sc = [scores(qa[hh][r * hr:(r + 1) * hr], hh, start, (r + 1) * hr) for hh, r in blocks]
    first = [update(s, hh, start, jnp.full((hr, 1), -jnp.inf, F32), jnp.zeros((hr, 2 * hd), F32), r * hr)
             for s, (hh, r) in zip(sc, blocks)]
    carry = tuple((jnp.max(jnp.concatenate([jnp.broadcast_to(m, (hr, 2 * hd))
                                            for m, _ in first[hh * DIAG_SPLIT:(hh + 1) * DIAG_SPLIT]], axis=0),
                           axis=-1, keepdims=True),
                   jnp.concatenate([a for _, a in first[hh * DIAG_SPLIT:(hh + 1) * DIAG_SPLIT]], axis=0))
                  for hh in range(2))

    thresholds, q_norms = [], []
    for hh in range(2):
        qf = jnp.where((lane // hd) == hh, qa[hh].astype(F32), 0.0)
        q_norms.append(jnp.sqrt(jnp.max(jnp.sum(qf * qf, axis=-1, keepdims=True))))
        thresholds.append(jnp.min(carry[hh][0]) - SKIP_MARGIN)
    first_j = jnp.int32(0)
    for j in range(nq - 1):
        skippable = j < qi
        for hh in range(2):
            skippable = skippable & (q_norms[hh] * kmax_scr[hh, j] + bmax_scr[hh, j] < thresholds[hh])
        first_j = jnp.where(skippable & (first_j == j), j + 1, first_j)

    carry = lax.fori_loop(first_j, qi, chunk, carry)
    outs = []
    for hh in range(2):
        acc = carry[hh][1]
        outs.append(acc / acc[:, ones_lane[hh]:ones_lane[hh] + 1])
    o_ref[...] = jnp.where(lane < hd, outs[0], outs[1])


def _fox(q, kt, v, c_bht, nb, t, tq):
    n = q.shape[0]
    nq = t // tq
    hw = 2 * HEAD_DIM
    return pl.pallas_call(
        functools.partial(_fox_kernel, tq=tq),
        grid=(nb, FOX_HEADS // 2, nq),
        in_specs=[
            pl.BlockSpec((tq, hw), lambda b, h, i: (b * nq + i, h)),
            pl.BlockSpec((hw, t), lambda b, h, i: (h, b)),
            pl.BlockSpec((t, hw), lambda b, h, i: (b, h)),
            pl.BlockSpec((1, 2, 1, t), lambda b, h, i: (b, h, 0, 0)),
        ],
        out_specs=pl.BlockSpec((tq, hw), lambda b, h, i: (b * nq + i, h)),
        out_shape=jax.ShapeDtypeStruct((n, FOX_WIDTH), F32),
        scratch_shapes=[pltpu.VMEM((2, hw, t), BF16), pltpu.VMEM((2, t, hw), BF16),
                        pltpu.SMEM((2, nq), F32), pltpu.SMEM((2, nq), F32)],
        compiler_params=_params("arbitrary", "arbitrary", "arbitrary"),
        name="fox_attention",
    )(q, kt, v, c_bht.reshape(nb, FOX_HEADS, 1, t))


def _fox_cached_kernel(q_ref, kn_ref, vn_ref, kc_ref, vc_ref, c_ref, o_ref, *, past, t):
    lane = lax.broadcasted_iota(jnp.int32, (1, 2 * HEAD_DIM), 1)
    row = lax.broadcasted_iota(jnp.int32, (t, t), 0)
    col = lax.broadcasted_iota(jnp.int32, (t, t), 1)
    q = q_ref[...] * FOX_SCALE
    kc = kc_ref[...].astype(BF16)
    vc = vc_ref[...].astype(BF16)
    kn = kn_ref[...].astype(BF16)
    vn = vn_ref[...].astype(BF16)
    outs = []
    for hh in range(2):
        qh = jnp.where(lane // HEAD_DIM == hh, q, 0.0).astype(BF16)
        s1 = lax.dot_general(qh, kc, NT_DIMS, preferred_element_type=F32)
        s1 = s1 - c_ref[0, hh, :, 0:past]
        s2 = lax.dot_general(qh, kn, NT_DIMS, preferred_element_type=F32)
        s2 = s2 - c_ref[0, hh, :, past:past + t]
        s2 = jnp.where(col <= row, s2, -jnp.inf)
        m = jnp.maximum(jnp.max(s1, axis=-1, keepdims=True), jnp.max(s2, axis=-1, keepdims=True))
        p1 = jnp.exp(s1 - m)
        p2 = jnp.exp(s2 - m)
        l = jnp.sum(p1, axis=-1, keepdims=True) + jnp.sum(p2, axis=-1, keepdims=True)
        acc = (jnp.dot(p1.astype(BF16), vc, preferred_element_type=F32)
               + jnp.dot(p2.astype(BF16), vn, preferred_element_type=F32))
        outs.append(acc / l)
    o_ref[...] = jnp.where(lane < HEAD_DIM, outs[0], outs[1])


def _fox_cached(q, k, v, cache_k, cache_v, c_bhl, nb, t):
    n = q.shape[0]
    past = cache_k.shape[0] // nb
    hw = 2 * HEAD_DIM
    new = pl.BlockSpec((t, hw), lambda b, h: (b, h))
    old = pl.BlockSpec((past, hw), lambda b, h: (b, h))
    return pl.pallas_call(
        functools.partial(_fox_cached_kernel, past=past, t=t),
        grid=(nb, FOX_HEADS // 2),
        in_specs=[new, new, new, old, old,
                  pl.BlockSpec((1, 2, 1, past + t), lambda b, h: (b, h, 0, 0))],
        out_specs=new,
        out_shape=jax.ShapeDtypeStruct((n, FOX_WIDTH), F32),
        compiler_params=_params("arbitrary", "arbitrary"),
        name="fox_attention_cached",
    )(q, k, v, cache_k, cache_v, c_bhl.reshape(nb, FOX_HEADS, 1, past + t))


def _levels(chunk):
    out, m = [], chunk // 2
    while m >= 1:
        out.append(m)
        m //= 2
    return out


FINE_LEVEL = 4


def _decay_sum_matrix(chunk):
    r = np.arange(chunk)[:, None]
    j = np.arange(chunk)[None, :]
    incl = (j <= r).astype(np.float32)
    blocks = [incl]
    for m in _levels(chunk):
        if m <= FINE_LEVEL:
            mid = (r // (2 * m)) * (2 * m) + m - 1
            blocks.append(incl - (j <= mid).astype(np.float32))
    return np.concatenate(blocks, axis=0)


def _level_ids(chunk):
    levels = _levels(chunk)
    t = np.arange(chunk)
    x = t[:, None] ^ t[None, :]
    m = np.where(x > 0, 2 ** np.floor(np.log2(np.maximum(x, 1))).astype(np.int64), 0)
    lvl = np.full((chunk, chunk), len(levels) + 1, np.int32)
    for li, lm in enumerate(levels):
        lvl[(t[:, None] > t[None, :]) & (m == lm)] = li
    lvl[x == 0] = len(levels)
    return lvl


def _level_ids_wide(chunk, heads):
    return np.tile(_level_ids(chunk), (1, heads))


def _two_pass(mat_bf16, x):
    hi, lo = _split(x)
    d = lambda u: jnp.dot(mat_bf16, u, preferred_element_type=F32)
    return d(hi) + d(lo)


def _hgrn_kernel(p_ref, lb_ref, ng_ref, s0_ref, dmat_ref, lvl_ref, ones_ref, o_ref, st_ref, st_scr, *, chunk,
                 nchunks, group):
    ti = pl.program_id(1)

    @pl.when(ti == 0)
    def _():
        st_scr[...] = s0_ref[0]

    w = HG_WIDTH
    levels = _levels(chunk)
    lb = lb_ref[...]
    log_lb = jnp.log(lb)
    log_1m_lb = jnp.log1p(-lb)
    row = lax.broadcasted_iota(jnp.int32, (chunk, 1), 0)
    lane_head = lax.broadcasted_iota(jnp.int32, (1, w), 1) // HEAD_DIM
    row_head = lax.broadcasted_iota(jnp.int32, (w, 1), 0) // HEAD_DIM

    head_lanes = [(lane_head == h).astype(BF16) for h in range(HG_HEADS)]

    def stack(x):
        xb = x.astype(BF16)
        return jnp.concatenate([xb * m for m in head_lanes], axis=0)

    def each(fn, *lists):
        return [fn(*xs) for xs in zip(*lists)]

    def chunk_group(gi, _):
        base = gi * (group * chunk)
        rows = [pl.ds(pl.multiple_of(base + i * chunk, chunk), chunk) for i in range(group)]
        q = [p_ref[r, 0:w] for r in rows]
        fx = [p_ref[r, w:2 * w] for r in rows]
        val = [p_ref[r, 2 * w:3 * w] for r in rows]
        e = each(lambda x: jnp.exp(-jnp.abs(x)), fx)
        bv = each(lambda x, ei: log_1m_lb + (jnp.minimum(x, 0.0) - jnp.log1p(ei)), fx, e)
        logf = each(lambda x: jnp.maximum(log_lb, x) + jnp.log1p(jnp.exp(-jnp.abs(log_lb - x))), bv)
        kd = each(lambda x, ei: (1.0 - lb) * (jnp.where(x >= 0.0, ei, 1.0) / (1.0 + ei)), fx, e)
        sums = each(lambda x: _two_pass(dmat_ref[...], x), logf)
        b = [x[0:chunk] for x in sums]

        lvl = lvl_ref[...]
        attn = each(lambda qi, ki: jnp.where(lvl == len(levels), _mm(qi, stack(ki), NT_DIMS), 0.0), q, kd)
        fine = 0
        for li, m in enumerate(levels):
            if m > FINE_LEVEL:
                def rel(bi):
                    mids = [jnp.broadcast_to(bi[s0 + m - 1:s0 + m, :], (2 * m, w)) for s0 in range(0, chunk, 2 * m)]
                    return bi - jnp.concatenate(mids, axis=0)
                x = each(rel, b)
            else:
                fine += 1
                x = [sm[fine * chunk:(fine + 1) * chunk] for sm in sums]
            wgt = each(lambda xi: jnp.exp(-jnp.abs(xi)), x)
            second = ((row // m) % 2) == 1
            ql = each(lambda qi, wi: jnp.where(second, qi * wi, 0.0), q, wgt)
            kl = each(lambda ki, wi: jnp.where(second, 0.0, ki * wi), kd, wgt)
            attn = each(lambda at, qi, ki: jnp.where(lvl == li, _mm(qi, stack(ki), NT_DIMS), at), attn, ql, kl)
        o_intra = each(lambda at, vi: _mm(at, stack(vi)), attn, val)
        q_dec = each(lambda qi, bi: qi * jnp.exp(bi), q, b)
        b_last = [bi[chunk - 1:chunk, :] for bi in b]
        upd = each(lambda vi, ki, bl, bi: _mm(vi, ki * jnp.exp(bl - bi), TN_DIMS), val, kd, b_last, b)

        st = st_scr[...]
        for i in range(group):
            o = _mm(q_dec[i], st, NT_DIMS) + o_intra[i]
            st = jnp.exp(b_last[i]) * st + jnp.where(row_head == lane_head, upd[i], 0.0)
            ms = _head_sum(o * o, ones_ref[...]) * (1.0 / HEAD_DIM)
            g = p_ref[rows[i], 3 * w:4 * w]
            o_ref[rows[i], :] = o * lax.rsqrt(ms + EPS) * ng_ref[...] * (g * _sigmoid(g))
        st_scr[...] = st
        return 0

    lax.fori_loop(0, nchunks // group, chunk_group, 0)

    @pl.when(ti == pl.num_programs(1) - 1)
    def _():
        st_ref[0] = st_scr[...]


def _hgrn(p_hg, lb, norm_g, st0, nb, t, chunk, tile):
    n = p_hg.shape[0]
    nt = t // tile
    w = HG_WIDTH
    dmat = jnp.asarray(_decay_sum_matrix(chunk), BF16)
    lvl = jnp.asarray(_level_ids_wide(chunk, HG_HEADS))
    ones = jnp.asarray(_block_ones_np(w), BF16)
    return pl.pallas_call(
        functools.partial(_hgrn_kernel, chunk=chunk, nchunks=tile // chunk, group=min(4, tile // chunk)),
        grid=(nb, nt),
        in_specs=[
            pl.BlockSpec((tile, HG_COLS), lambda b, i: (b * nt + i, 0)),
            _const_spec((1, w)), _const_spec((1, w)),
            pl.BlockSpec((1, w, w), lambda b, i: (b, 0, 0)),
            _const_spec(dmat.shape), _const_spec(lvl.shape), _const_spec(ones.shape),
        ],
        out_specs=[pl.BlockSpec((tile, w), lambda b, i: (b * nt + i, 0)),
                   pl.BlockSpec((1, w, w), lambda b, i: (b, 0, 0))],
        out_shape=[jax.ShapeDtypeStruct((n, w), F32), jax.ShapeDtypeStruct((nb, w, w), F32)],
        scratch_shapes=[pltpu.VMEM((w, w), F32)],
        compiler_params=_params("arbitrary", "arbitrary"),
        name="hgrn2",
    )(p_hg, lb, norm_g, st0, dmat, lvl, ones)


def _rwkv_kernel(p_ref, prev_ref, s0_ref, mu_ref, vec_ref, w2_ref, a2_ref, g2_ref, ltri_ref, ones_ref,
                 lvlw_ref, y_ref, st_ref, st_scr, prev_scr, *, chunk, nchunks, group):
    ti = pl.program_id(1)

    @pl.when(ti == 0)
    def _():
        st_scr[...] = s0_ref[0]
        prev_scr[...] = prev_ref[0]

    w = RW_WIDTH
    hc = RW_HEADS * chunk
    levels = _levels(chunk)
    nl = len(levels)
    w0 = vec_ref[0:1, :]
    a0 = vec_ref[1:2, :]
    k_k = vec_ref[2:3, :]
    k_a = vec_ref[3:4, :]
    r_k = vec_ref[4:5, :]
    ln_w = vec_ref[5:6, :]
    ln_b = vec_ref[6:7, :]
    row = lax.broadcasted_iota(jnp.int32, (chunk, 1), 0)
    lane_head = lax.broadcasted_iota(jnp.int32, (1, w), 1) // HEAD_DIM
    row_id = lax.broadcasted_iota(jnp.int32, (w, 1), 0)
    lane_id = lax.broadcasted_iota(jnp.int32, (1, w), 1)
    row_head = row_id // HEAD_DIM
    wide_head = lax.broadcasted_iota(jnp.int32, (1, hc), 1) // chunk

    head_lanes = [(lane_head == h).astype(BF16) for h in range(RW_HEADS)]
    head_lanes2 = [jnp.concatenate([m, m], axis=1) for m in head_lanes]
    wide_lanes = [(wide_head == h).astype(BF16) for h in range(RW_HEADS)]

    def stack(x):
        xb = x.astype(BF16)
        masks = head_lanes if x.shape[1] == w else head_lanes2
        return jnp.concatenate([xb * m for m in masks], axis=0)

    def blockdiag(xw):
        xb = xw.astype(BF16)
        return jnp.concatenate([xb * m for m in wide_lanes], axis=0)

    def each(fn, *lists):
        return [fn(*xs) for xs in zip(*lists)]

    def chunk_group(gi, _):
        base = gi * (group * chunk)
        rows = [pl.ds(pl.multiple_of(base + i * chunk, chunk), chunk) for i in range(group)]
        ones_bd = ones_ref[...]
        mu = mu_ref[...]
        ps = [p_ref[r, :] for r in rows]
        prevs = [prev_scr[...]] + [p[chunk - 1:chunk, :] for p in ps[:-1]]
        prev_scr[...] = ps[-1][chunk - 1:chunk, :]
        xs = each(lambda p, pv: p + mu * (jnp.where(row == 0, pv, pltpu.roll(p, 1, 0)) - p), ps, prevs)
        r = [x[:, 0:w] for x in xs]
        k = [x[:, w:2 * w] for x in xs]
        v = [x[:, 2 * w:3 * w] for x in xs]
        tail = [x[:, 3 * w:] for x in xs]
        w_lin = each(lambda t: w0 + _mm3(jnp.tanh(t), w2_ref[...]), tail)
        a = each(lambda t: _sigmoid(a0 + _mm(t, a2_ref[...])), tail)
        g = each(lambda t: _mm(_sigmoid(t), g2_ref[...]), tail)
        logw = each(lambda wl: -jnp.exp(-(jnp.maximum(-wl, 0.0) + jnp.log1p(jnp.exp(-jnp.abs(wl)))) - 0.5), w_lin)
        kk = each(lambda x: x * k_k, k)
        kk = each(lambda x: x / jnp.maximum(jnp.sqrt(_head_sum(x * x, ones_bd)), 1e-12), kk)
        kmod = each(lambda x, ai: x * (1.0 + (ai - 1.0) * k_a), k, a)
        beta = each(lambda x, ai: x * ai, kk, a)

        lg = each(lambda x: _two_pass(ltri_ref[...], x), logw)
        lg_end = [x[chunk - 1:chunk, :] for x in lg]
        a_n = each(lambda x, l, lw: x * jnp.exp(l - lw), kk, lg, logw)
        r_n = each(lambda x, l: x * jnp.exp(l), r, lg)
        inv_g = each(lambda l: jnp.exp(-l), lg)
        to_end = each(lambda le, l: jnp.exp(le - l), lg_end, lg)
        v_s = each(stack, v)

        gram = each(lambda an, rn, be, km, ig: _mm(jnp.concatenate([an, rn], axis=0),
                                                   jnp.concatenate([stack(be * ig), stack(km * ig)], axis=0),
                                                   NT_DIMS), a_n, r_n, beta, kmod, inv_g)
        lvlw = lvlw_ref[...]
        strict = lvlw < nl
        incl = lvlw <= nl
        n_w = [jnp.where(strict, x[0:chunk, 0:hc], 0.0).astype(BF16) for x in gram]
        m_k = [jnp.where(strict, x[0:chunk, hc:2 * hc], 0.0) for x in gram]
        l_b = [jnp.where(incl, x[chunk:2 * chunk, 0:hc], 0.0) for x in gram]
        l_k = [jnp.where(incl, x[chunk:2 * chunk, hc:2 * hc], 0.0) for x in gram]

        t_w = each(lambda n: (jnp.where(lvlw == nl, 1.0, 0.0)
                              - jnp.where(lvlw == nl - 1, n, jnp.zeros_like(n)).astype(F32)), n_w)
        blk = 2
        while blk < chunk:
            li = levels.index(blk)
            level = (lvlw == li).astype(BF16)
            x_w = each(lambda t, n: _mm(t, blockdiag(n * level)), t_w, n_w)
            t_w = each(lambda t, x: t - _mm(x, blockdiag(t)), t_w, x_w)
            blk *= 2

        rhs = each(lambda an, mk, vs: jnp.concatenate([an, _mm(mk, vs)], axis=1), a_n, m_k, v_s)
        x0 = each(lambda t, rh: _mm(t, stack(rh)), t_w, rhs)

        def residual(n, rh, x):
            x_hi, x_lo = _split(x)
            return rh - x - (jnp.dot(n, stack(x_hi), preferred_element_type=F32)
                             + jnp.dot(n, stack(x_lo), preferred_element_type=F32))

        res = each(residual, n_w, rhs, x0)
        sol = each(lambda x, t, rs: x + _mm(t, stack(rs)), x0, t_w, res)
        a_hat = [x[:, 0:w] for x in sol]
        e0 = [x[:, w:2 * w] for x in sol]
        lb_x = each(lambda lb, x: _mm(lb, stack(x)), l_b, sol)
        r_hat = each(lambda rn, lx: rn - lx[:, 0:w], r_n, lb_x)
        y0 = each(lambda lk, vs, lx: _mm(lk, vs) - lx[:, w:2 * w], l_k, v_s, lb_x)
        same_head = row_head == lane_head
        bh = each(lambda be, te: be * te, beta, to_end)
        kh = each(lambda km, te: km * te, kmod, to_end)
        trans = each(lambda ah, b_, le: (jnp.where(row_id == lane_id, jnp.exp(le), 0.0)
                                         - jnp.where(same_head, _mm(ah, b_, TN_DIMS), 0.0)), a_hat, bh, lg_end)
        q0 = each(lambda vi, ei, k_, b_: jnp.where(same_head, _mm(jnp.concatenate([vi, ei], axis=0),
                                                                   jnp.concatenate([k_, -b_], axis=0), TN_DIMS), 0.0),
                  v, e0, kh, bh)

        st = st_scr[...]
        for i in range(group):
            y = _mm(r_hat[i], st, NT_DIMS) + y0[i]
            st = _mm(st, trans[i]) + q0[i]

            inv_n = 1.0 / HEAD_DIM
            mean = _head_sum(y, ones_bd) * inv_n
            yc = y - mean
            var = _head_sum(yc * yc, ones_bd) * inv_n
            yn = yc * lax.rsqrt(var + RW_GN_EPS) * ln_w + ln_b
            bonus = _head_sum(r[i] * kmod[i] * r_k, ones_bd) * v[i]
            y_ref[rows[i], :] = (yn + bonus) * g[i]
        st_scr[...] = st
        return 0

    lax.fori_loop(0, nchunks // group, chunk_group, 0)

    @pl.when(ti == pl.num_programs(1) - 1)
    def _():
        st_ref[0] = st_scr[...]


def _rwkv(p_rw, prev, st0, mu, vecs, w2p, a2p, g2p, nb, t, chunk, tile):
    n = p_rw.shape[0]
    nt = t // tile
    w = RW_WIDTH
    consts = (jnp.asarray(np.tril(np.ones((chunk, chunk), np.float32)), BF16),
              jnp.asarray(_block_ones_np(w), BF16),
              jnp.asarray(_level_ids_wide(chunk, RW_HEADS)))
    return pl.pallas_call(
        functools.partial(_rwkv_kernel, chunk=chunk, nchunks=tile // chunk, group=min(8, tile // chunk)),
        grid=(nb, nt),
        in_specs=[
            pl.BlockSpec((tile, RW_COLS), lambda b, i: (b * nt + i, 0)),
            pl.BlockSpec((1, 1, RW_COLS), lambda b, i: (b, 0, 0)),
            pl.BlockSpec((1, w, w), lambda b, i: (b, 0, 0)),
            _const_spec((1, RW_COLS)), _const_spec((8, w)),
            _const_spec((RW_LORA, w)), _const_spec((RW_LORA, w)), _const_spec((RW_LORA, w)),
        ] + [_const_spec(c.shape) for c in consts],
        out_specs=[pl.BlockSpec((tile, w), lambda b, i: (b * nt + i, 0)),
                   pl.BlockSpec((1, w, w), lambda b, i: (b, 0, 0))],
        out_shape=[jax.ShapeDtypeStruct((n, w), F32), jax.ShapeDtypeStruct((nb, w, w), F32)],
        scratch_shapes=[pltpu.VMEM((w, w), F32), pltpu.VMEM((1, RW_COLS), F32)],
        compiler_params=_params("arbitrary", "arbitrary"),
        name="rwkv7",
    )(p_rw, prev, st0, mu, vecs, w2p, a2p, g2p, *consts)


def _outffn_kernel(yrw_ref, yfox_ref, yhg_ref, x_ref, ga1_ref, sh2_ref, sc2_ref, ga2_ref, g2_ref,
                   worw_ref, wofox_ref, wohg_ref, wg_ref, wu_ref, wo_ref, fg_ref, o_ref, *, final_norm, ff_step):
    mix = (jnp.dot(yrw_ref[...].astype(BF16), worw_ref[...], preferred_element_type=F32)
           + jnp.dot(yfox_ref[...].astype(BF16), wofox_ref[...], preferred_element_type=F32)
           + jnp.dot(yhg_ref[...].astype(BF16), wohg_ref[...], preferred_element_type=F32))
    x1 = x_ref[...] + ga1_ref[0] * mix
    ms = jnp.mean(x1 * x1, axis=-1, keepdims=True)
    h2 = x1 * lax.rsqrt(ms + EPS) * g2_ref[...]
    h2 = (h2 * (1.0 + sc2_ref[0]) + sh2_ref[0]).astype(BF16)
    ffn = jnp.zeros_like(x1)
    for j in range(D_FF // ff_step):
        cols = slice(j * ff_step, (j + 1) * ff_step)
        gate = jnp.dot(h2, wg_ref[:, cols], preferred_element_type=F32)
        up = jnp.dot(h2, wu_ref[:, cols], preferred_element_type=F32)
        act = (gate * _sigmoid(gate) * up).astype(BF16)
        ffn = ffn + jnp.dot(act, wo_ref[cols, :], preferred_element_type=F32)
    x2 = x1 + ga2_ref[0] * ffn
    if final_norm:
        ms2 = jnp.mean(x2 * x2, axis=-1, keepdims=True)
        x2 = x2 * lax.rsqrt(ms2 + EPS) * fg_ref[...]
    o_ref[...] = x2


def _outffn(y_rw, y_fox, y_hg, x, ga1, sh2, sc2, ga2, g2, wts, final_g, final_norm, tm, tiles_per_batch):
    n = x.shape[0]
    row = lambda w: pl.BlockSpec((tm, w), lambda i: (i, 0))
    mods = [_mod_spec(m, tiles_per_batch) for m in (ga1, sh2, sc2, ga2)]
    return pl.pallas_call(
        functools.partial(_outffn_kernel, final_norm=final_norm, ff_step=D_FF // 2),
        grid=(n // tm,),
        in_specs=[row(RW_WIDTH), row(FOX_WIDTH), row(HG_WIDTH), row(D_MODEL)] + mods
        + [_const_spec((1, D_MODEL))] + [_const_spec(w.shape) for w in wts] + [_const_spec((1, D_MODEL))],
        out_specs=row(D_MODEL),
        out_shape=jax.ShapeDtypeStruct((n, D_MODEL), F32),
        compiler_params=_params("arbitrary"),
        name="outffn",
    )(y_rw, y_fox, y_hg, x, ga1, sh2, sc2, ga2, g2, *wts, final_g)


def _block_diag_t(s):
    nb, h, dk, dv = s.shape
    st = jnp.swapaxes(s, 2, 3)
    eye = jnp.eye(h, dtype=s.dtype)
    return jnp.einsum("bhvk,hg->bhvgk", st, eye).reshape(nb, h * dv, h * dk)


def _block_diag(s):
    nb, h, dv, dk = s.shape
    eye = jnp.eye(h, dtype=s.dtype)
    return jnp.einsum("bhvk,hg->bhvgk", s, eye).reshape(nb, h * dv, h * dk)


def _diag_blocks(sbd, h):
    nb, rows, cols = sbd.shape
    x = sbd.reshape(nb, h, rows // h, h, cols // h)
    return jnp.stack([x[:, i, :, i, :] for i in range(h)], axis=1)


def _layer_weights(l, w_in, rw_w2, rw_a2, rw_g2, w_out, w_ffn_in, w_ffn_out):
    wi = w_in[l]
    c0, c1 = RW_COLS, RW_COLS + 3 * FOX_WIDTH
    proj = (wi[:, :c0].astype(BF16), wi[:, c0:c1].astype(BF16),
            wi[:, c0 + FOX_WIDTH:c0 + 2 * FOX_WIDTH].T.astype(BF16),
            wi[:, c1:c1 + FOX_HEADS].T.astype(BF16), wi[:, c1 + FOX_HEADS:].astype(BF16))
    zeros = lambda r: jnp.zeros((r, RW_WIDTH), F32)
    w2p = jnp.concatenate([rw_w2[l], zeros(RW_LORA - RW_W_LORA)], axis=0)
    a2p = jnp.concatenate([zeros(RW_W_LORA), rw_a2[l], zeros(RW_G_LORA)], axis=0)
    g2p = jnp.concatenate([zeros(RW_W_LORA + RW_A_LORA), rw_g2[l]], axis=0)
    wo = w_out[l].astype(BF16)
    ffn = (wo[:RW_WIDTH], wo[RW_WIDTH:RW_WIDTH + FOX_WIDTH], wo[RW_WIDTH + FOX_WIDTH:],
           w_ffn_in[l][:, :D_FF].astype(BF16), w_ffn_in[l][:, D_FF:].astype(BF16), w_ffn_out[l].astype(BF16))
    return proj, (w2p, a2p, g2p), ffn


def _group_layer(x, mods, lw, lp, prev, rw_st0, hg_st0, cache, nb, t, cfg, final_g, final_norm):
    proj_w, lora_w, ffn_w = lw
    sh1, sc1, ga1, sh2, sc2, ga2 = mods
    tm, tpb = cfg["tm"], cfg["tiles_per_batch"]
    p_rw, q, k, v, kt, f, p_hg = _normproj(x, sh1, sc1, lp["norm1_g"], proj_w, tm, tpb, cache is None)
    if f.shape[0] == nb:
        f_bht = f
    else:
        f_bht = jnp.swapaxes(f.reshape(FOX_HEADS, nb, t), 0, 1)
    if cache is None:
        logf_bht, c_bhl = _gate(f_bht, lp["fox_b_f"], None)
        y_fox = _fox(q, kt, v, c_bhl, nb, t, cfg["tq"])
    else:
        cache_k, cache_v, cache_logf = cache
        past = cache_k.shape[1]
        logf_bht, c_bhl = _gate(f_bht, lp["fox_b_f"], jnp.swapaxes(cache_logf, 1, 2))
        y_fox = _fox_cached(q, k, v, cache_k.reshape(nb * past, FOX_WIDTH),
                            cache_v.reshape(nb * past, FOX_WIDTH), c_bhl, nb, t)
    y_hg, hg_st = _hgrn(p_hg, lp["hg_lb"], lp["hg_norm_g"], hg_st0, nb, t, cfg["hg_chunk"], cfg["hg_tile"])
    y_rw, rw_st = _rwkv(p_rw, prev, rw_st0, lp["rw_mu"], lp["rw_vecs"], *lora_w, nb, t,
                        cfg["rw_chunk"], cfg["rw_tile"])
    x = _outffn(y_rw, y_fox, y_hg, x, ga1, sh2, sc2, ga2, lp["norm2_g"], ffn_w, final_g, final_norm, tm, tpb)
    states = (
        k.reshape(nb, t, FOX_HEADS, HEAD_DIM), v.reshape(nb, t, FOX_HEADS, HEAD_DIM),
        jnp.swapaxes(logf_bht, 1, 2),
        _diag_blocks(rw_st, RW_HEADS),
        p_rw.reshape(nb, t, RW_COLS)[:, t - 1:t, :],
        jnp.swapaxes(_diag_blocks(hg_st, HG_HEADS), 2, 3),
    )
    return x, states


def kernel(x_prompt, x_sample, c_prompt, c_sample, cache_fox_k, cache_fox_v, cache_fox_logf, state_rwkv,
           state_rwkv_shift, state_hgrn, norm1_g, w_ada, b_ada, w_in, rw_mu, rw_w0, rw_w2, rw_a0, rw_a2, rw_g2,
           rw_k_k, rw_k_a, rw_r_k, rw_ln_w, rw_ln_b, fox_b_f, hg_lb_logits, hg_norm_g, w_out, norm2_g,
           w_ffn_in, w_ffn_out, final_norm_g):
    dt = x_prompt.dtype
    bp, tp, _ = x_prompt.shape
    bs, ts, _ = x_sample.shape

    mod = _modulation(jnp.concatenate([c_prompt, c_sample], axis=0).astype(F32), w_ada, b_ada)
    lbs = _hg_lower_bounds(hg_lb_logits.astype(F32))

    cfg_p = dict(tm=512, tiles_per_batch=tp // 512, tq=512, hg_chunk=HG_CHUNK, hg_tile=512,
                 rw_chunk=RW_CHUNK, rw_tile=512)
    cfg_s = dict(tm=bs * ts, tiles_per_batch=1, hg_chunk=ts, hg_tile=ts, rw_chunk=ts, rw_tile=ts)

    xp = x_prompt.reshape(bp * tp, D_MODEL).astype(F32)
    xs = x_sample.reshape(bs * ts, D_MODEL).astype(F32)
    zero_prev = jnp.zeros((bp, 1, RW_COLS), F32)
    zero_state = jnp.zeros((bp, RW_WIDTH, RW_WIDTH), F32)
    final_g = final_norm_g.reshape(1, D_MODEL)
    row = lambda a: a.reshape(1, -1)

    outs_p, outs_s = [], []
    for l in range(DEPTH):
        lw = _layer_weights(l, w_in, rw_w2, rw_a2, rw_g2, w_out, w_ffn_in, w_ffn_out)
        zero_row = jnp.zeros((RW_WIDTH,), F32)
        lp = dict(
            norm1_g=row(norm1_g[l]), norm2_g=row(norm2_g[l]), fox_b_f=fox_b_f[l],
            hg_lb=row(lbs[l]), hg_norm_g=row(hg_norm_g[l]), rw_mu=row(rw_mu[l]),
            rw_vecs=jnp.stack([rw_w0[l], rw_a0[l], rw_k_k[l], rw_k_a[l], rw_r_k[l].reshape(-1),
                               rw_ln_w[l], rw_ln_b[l], zero_row], axis=0),
        )
        chunks_p = [m[:, None, :] for m in jnp.split(mod[l, :bp], 6, axis=-1)]
        chunks_s = [jnp.repeat(m, ts, axis=0)[None] for m in jnp.split(mod[l, bp:], 6, axis=-1)]
        last = l == DEPTH - 1
        xp, st_p = _group_layer(xp, chunks_p, lw, lp, zero_prev, zero_state, zero_state, None,
                                bp, tp, cfg_p, final_g, last)
        xs, st_s = _group_layer(xs, chunks_s, lw, lp, state_rwkv_shift[l].astype(F32),
                                _block_diag(state_rwkv[l].astype(F32)),
                                _block_diag_t(state_hgrn[l].astype(F32)),
                                (cache_fox_k[l].astype(F32), cache_fox_v[l].astype(F32),
                                 cache_fox_logf[l].astype(F32)),
                                bs, ts, cfg_s, final_g, last)
        outs_p.append(st_p)
        outs_s.append(st_s)

    stk = lambda outs, j: jnp.stack([o[j] for o in outs], axis=0).astype(dt)
    y_prompt = xp.reshape(bp, tp, D_MODEL).astype(dt)
    y_sample = xs.reshape(bs, ts, D_MODEL).astype(dt)
    return (y_prompt, y_sample,
            stk(outs_p, 0), stk(outs_p, 1), stk(outs_p, 2), stk(outs_p, 3), stk(outs_p, 4), stk(outs_p, 5),
            stk(outs_s, 0), stk(outs_s, 1), stk(outs_s, 2), stk(outs_s, 3), stk(outs_s, 4), stk(outs_s, 5))
```

```python
import functools

import numpy as np
import jax
import jax.numpy as jnp
from jax import lax
from jax.experimental import pallas as pl
from jax.experimental.pallas import tpu as pltpu

F32 = jnp.float32
BF16 = jnp.bfloat16
HIGHEST = lax.Precision.HIGHEST

D_MODEL = 1024
DEPTH = 4
HEAD_DIM = 64
RW_HEADS = 4
RW_WIDTH = RW_HEADS * HEAD_DIM
FOX_HEADS = 8
FOX_WIDTH = FOX_HEADS * HEAD_DIM
HG_HEADS = 4
HG_WIDTH = HG_HEADS * HEAD_DIM
RW_W_LORA = 32
RW_A_LORA = 32
RW_G_LORA = 64
RW_LORA = RW_W_LORA + RW_A_LORA + RW_G_LORA
RW_COLS = 3 * RW_WIDTH + RW_LORA
FOX_COLS = 3 * FOX_WIDTH + FOX_HEADS
HG_COLS = 4 * HG_WIDTH
D_FF = 2816
EPS = 1e-6
RW_GN_EPS = 64e-5
FOX_SCALE = HEAD_DIM ** -0.5
HG_CHUNK = 64
RW_CHUNK = 64

VMEM_LIMIT_BYTES = 56 * 1024 * 1024

NT_DIMS = (((1,), (1,)), ((), ()))
TN_DIMS = (((0,), (0,)), ((), ()))
NN_DIMS = (((1,), (0,)), ((), ()))


def _params(*sem):
    return pltpu.CompilerParams(dimension_semantics=sem, vmem_limit_bytes=VMEM_LIMIT_BYTES)


def _const_spec(shape):
    nd = len(shape)
    return pl.BlockSpec(shape, lambda *_: (0,) * nd, pipeline_mode=pl.Buffered(1))


def _mm(a, b, dims=NN_DIMS):
    return lax.dot_general(a.astype(BF16), b.astype(BF16), dims, preferred_element_type=F32)


def _split(a):
    hi = a.astype(BF16)
    lo = (a - hi.astype(F32)).astype(BF16)
    return hi, lo


def _mm3(a, b, dims=NN_DIMS):
    ah, al = _split(a)
    bh, bl = _split(b)
    d = lambda x, y: lax.dot_general(x, y, dims, preferred_element_type=F32)
    return d(ah, bh) + (d(ah, bl) + d(al, bh))


def _head_sum(x, ones_bd):
    hi, lo = _split(x)
    d = lambda u: jnp.dot(u, ones_bd, preferred_element_type=F32)
    return d(hi) + d(lo)


def _block_ones_np(width):
    r = np.arange(width) // HEAD_DIM
    return (r[:, None] == r[None, :]).astype(np.float32)


def _log_sigmoid(z):
    return jnp.minimum(z, 0.0) - jnp.log1p(jnp.exp(-jnp.abs(z)))


def _sigmoid(z):
    return 1.0 / (1.0 + jnp.exp(-z))


def _mod_kernel(c_ref, w_ref, b_ref, o_ref):
    c = c_ref[...]
    s = c * _sigmoid(c)
    o_ref[0] = jnp.dot(s, w_ref[0], precision=HIGHEST, preferred_element_type=F32) + b_ref[0]


def _modulation(c_all, w_ada, b_ada):
    nb = c_all.shape[0]
    tn = 1536
    n_out = w_ada.shape[2]
    return pl.pallas_call(
        _mod_kernel,
        grid=(DEPTH, n_out // tn),
        in_specs=[
            pl.BlockSpec((nb, D_MODEL), lambda l, j: (0, 0)),
            pl.BlockSpec((1, D_MODEL, tn), lambda l, j: (l, 0, j)),
            pl.BlockSpec((1, 1, tn), lambda l, j: (l, 0, j)),
        ],
        out_specs=pl.BlockSpec((1, nb, tn), lambda l, j: (l, 0, j)),
        out_shape=jax.ShapeDtypeStruct((DEPTH, nb, n_out), F32),
        compiler_params=_params("arbitrary", "arbitrary"),
        name="adaln_mod",
    )(c_all, w_ada, b_ada.reshape(DEPTH, 1, n_out))


def _lbs_kernel(x_ref, o_ref):
    x = x_ref[...]
    m = jnp.max(x, axis=0, keepdims=True)
    e = jnp.exp(x - m)
    p = e / jnp.sum(e, axis=0, keepdims=True)
    acc = jnp.zeros_like(p[0:1])
    rows = []
    for l in range(DEPTH):
        acc = acc + p[l:l + 1]
        rows.append(acc)
    cs = jnp.concatenate(rows, axis=0)
    o_ref[...] = cs - cs[0:1]


def _hg_lower_bounds(logits):
    return pl.pallas_call(
        _lbs_kernel,
        out_shape=jax.ShapeDtypeStruct(logits.shape, F32),
        name="hgrn_lower_bounds",
    )(logits)


def _normproj_kernel(x_ref, sh_ref, sc_ref, g_ref, wrw_ref, wqkv_ref, wkt_ref, wft_ref, whg_ref,
                     prw_ref, q_ref, k_ref, v_ref, kt_ref, ft_ref, phg_ref, *, emit_kt):
    x = x_ref[...]
    ms = jnp.mean(x * x, axis=-1, keepdims=True)
    h = x * lax.rsqrt(ms + EPS) * g_ref[...]
    h = h * (1.0 + sc_ref[0]) + sh_ref[0]
    hb = h.astype(BF16)
    prw_ref[...] = jnp.dot(hb, wrw_ref[...], preferred_element_type=F32)
    qkv = jnp.dot(hb, wqkv_ref[...], preferred_element_type=F32)
    q_ref[...] = qkv[:, :FOX_WIDTH]
    k_ref[...] = qkv[:, FOX_WIDTH:2 * FOX_WIDTH]
    v_ref[...] = qkv[:, 2 * FOX_WIDTH:]
    if emit_kt:
        kt_ref[...] = lax.dot_general(wkt_ref[...], hb, NT_DIMS, preferred_element_type=F32).astype(BF16)
    else:
        kt_ref[...] = jnp.zeros(kt_ref.shape, BF16)
    ft_ref[0] = lax.dot_general(wft_ref[...], hb, NT_DIMS, preferred_element_type=F32)
    phg_ref[...] = jnp.dot(hb, whg_ref[...], preferred_element_type=F32)


def _mod_spec(arr, tiles_per_batch):
    _, r, d = arr.shape
    return pl.BlockSpec((1, r, d), lambda i: (i // tiles_per_batch, 0, 0))


def _normproj(x, sh, sc, g, wts, tm, tiles_per_batch, emit_kt):
    n = x.shape[0]
    wrw, wqkv, wkt, wft, whg = wts
    row = lambda w: pl.BlockSpec((tm, w), lambda i: (i, 0))
    nbat = n // (tm * tiles_per_batch)
    kt_shape = (FOX_WIDTH, n) if emit_kt else (8, 128)
    kt_spec = (pl.BlockSpec((FOX_WIDTH, tm), lambda i: (0, i)) if emit_kt
               else pl.BlockSpec((8, 128), lambda i: (0, 0)))
    return pl.pallas_call(
        functools.partial(_normproj_kernel, emit_kt=emit_kt),
        grid=(n // tm,),
        in_specs=[
            row(D_MODEL), _mod_spec(sh, tiles_per_batch), _mod_spec(sc, tiles_per_batch),
            _const_spec((1, D_MODEL)),
            _const_spec(wrw.shape), _const_spec(wqkv.shape), _const_spec(wkt.shape), _const_spec(wft.shape),
            _const_spec(whg.shape),
        ],
        out_specs=[row(RW_COLS), row(FOX_WIDTH), row(FOX_WIDTH), row(FOX_WIDTH), kt_spec,
                   pl.BlockSpec((1, FOX_HEADS, tm), lambda i: (i // tiles_per_batch, 0, i % tiles_per_batch)),
                   row(HG_COLS)],
        out_shape=[
            jax.ShapeDtypeStruct((n, RW_COLS), F32),
            jax.ShapeDtypeStruct((n, FOX_WIDTH), F32),
            jax.ShapeDtypeStruct((n, FOX_WIDTH), F32),
            jax.ShapeDtypeStruct((n, FOX_WIDTH), F32),
            jax.ShapeDtypeStruct(kt_shape, BF16),
            jax.ShapeDtypeStruct((nbat, FOX_HEADS, tm * tiles_per_batch), F32),
            jax.ShapeDtypeStruct((n, HG_COLS), F32),
        ],
        compiler_params=_params("arbitrary"),
        name="normproj",
    )(x, sh, sc, g, wrw, wqkv, wkt, wft, whg)


def _cumsum_lanes(x, carry):
    length = x.shape[1]
    step = min(512, length)
    r = lax.broadcasted_iota(jnp.int32, (step, step), 0)
    c = lax.broadcasted_iota(jnp.int32, (step, step), 1)
    upper = (r <= c).astype(F32)
    outs = []
    for j in range(length // step):
        seg = x[:, j * step:(j + 1) * step]
        cs = jnp.dot(seg, upper, precision=HIGHEST, preferred_element_type=F32) + carry
        carry = cs[:, step - 1:step]
        outs.append(cs)
    return outs, carry, step


def _gate_kernel(*refs, past):
    if past:
        f_ref, bf_ref, cache_ref, logf_ref, c_ref = refs
    else:
        f_ref, bf_ref, logf_ref, c_ref = refs
    logf = _log_sigmoid(f_ref[0] + bf_ref[...])
    logf_ref[0] = logf
    carry = jnp.zeros((FOX_HEADS, 1), F32)
    if past:
        outs, carry, step = _cumsum_lanes(cache_ref[0], carry)
        for j, cs in enumerate(outs):
            c_ref[0, :, j * step:(j + 1) * step] = cs
    outs, carry, step = _cumsum_lanes(logf, carry)
    for j, cs in enumerate(outs):
        c_ref[0, :, past + j * step:past + (j + 1) * step] = cs


def _gate(f_bht, b_f, cache_logf_bhp):
    nb, _, t = f_bht.shape
    past = 0 if cache_logf_bhp is None else cache_logf_bhp.shape[2]
    blk = lambda w: pl.BlockSpec((1, FOX_HEADS, w), lambda b: (b, 0, 0))
    in_specs = [blk(t), _const_spec((FOX_HEADS, 1))]
    args = [f_bht, b_f.reshape(FOX_HEADS, 1)]
    if past:
        in_specs.append(blk(past))
        args.append(cache_logf_bhp)
    return pl.pallas_call(
        functools.partial(_gate_kernel, past=past),
        grid=(nb,),
        in_specs=in_specs,
        out_specs=[blk(t), blk(past + t)],
        out_shape=[jax.ShapeDtypeStruct((nb, FOX_HEADS, t), F32),
                   jax.ShapeDtypeStruct((nb, FOX_HEADS, past + t), F32)],
        compiler_params=_params("arbitrary"),
        name="fox_gate",
    )(*args)


LOG2E = 1.4426950408889634
BIAS_ROWS = 16
DIAG_SPLIT = 4
SKIP_MARGIN = 137.0


def _fox_kernel(q_ref, kt_ref, v_ref, c_ref, o_ref, kaug_scr, vaug_scr, kmax_scr, bmax_scr, *, tq, tk):
    qi = pl.program_id(2)
    hd = HEAD_DIM
    nk = kaug_scr.shape[2] // tk
    per_tile = tq // tk
    lane = lax.broadcasted_iota(jnp.int32, (1, 2 * hd), 1)
    bias_lane0 = (hd, 0)
    ones_lane = (hd, 0)

    @pl.when(qi == 0)
    def _():
        kt = kt_ref[...]
        v = v_ref[...]
        t = kt.shape[1]
        for hh in range(2):
            c2 = c_ref[0, hh] * LOG2E
            hi = c2.astype(BF16).astype(F32)
            mid = (c2 - hi).astype(BF16).astype(F32)
            lo = c2 - hi - mid
            bias = jnp.concatenate([-hi, -mid, -lo, jnp.zeros((BIAS_ROWS - 3, t), F32)], axis=0).astype(BF16)
            pad = jnp.zeros((hd - BIAS_ROWS, t), BF16)
            if hh == 0:
                kaug_scr[hh] = jnp.concatenate([kt[0:hd], bias, pad], axis=0)
            else:
                kaug_scr[hh] = jnp.concatenate([bias, pad, kt[hd:2 * hd]], axis=0)
            own = (lane // hd) == hh
            vaug_scr[hh] = jnp.where(own, v, jnp.where(lane == ones_lane[hh], 1.0, 0.0)).astype(BF16)
            kh = kt[hh * hd:(hh + 1) * hd].astype(F32)
            knorm = jnp.sqrt(jnp.sum(kh * kh, axis=0, keepdims=True))
            bsum = -(hi + mid + lo.astype(BF16).astype(F32))
            for j in range(nk):
                kmax_scr[hh, j] = jnp.max(knorm[:, j * tk:(j + 1) * tk])
                bmax_scr[hh, j] = jnp.max(bsum[:, j * tk:(j + 1) * tk])

    q = q_ref[...] * (FOX_SCALE * LOG2E)
    qa = []
    for hh in range(2):
        own = (lane // hd) == hh
        is_bias = (lane >= bias_lane0[hh]) & (lane < bias_lane0[hh] + 3)
        qa.append(jnp.where(own, q, jnp.where(is_bias, 1.0, 0.0)).astype(BF16))

    def scores(qh, hh, start, nk):
        return jnp.dot(qh, kaug_scr[hh, :, pl.ds(start, nk)], preferred_element_type=F32)

    def update(s, hh, start, m, acc, row0):
        if row0 is not None:
            rows_i = lax.broadcasted_iota(jnp.int32, s.shape, 0) + row0
            cols_i = lax.broadcasted_iota(jnp.int32, s.shape, 1)
            s = jnp.where(cols_i <= rows_i, s, -jnp.inf)
        width = 2 * hd
        part = s
        if s.shape[1] % width == 0:
            part = s[:, 0:width]
            for c0 in range(width, s.shape[1], width):
                part = jnp.maximum(part, s[:, c0:c0 + width])
        m_new = jnp.maximum(m, jnp.max(part, axis=-1, keepdims=True))
        alpha = jnp.exp2(m - m_new)
        p = jnp.exp2(s - m_new).astype(BF16)
        acc = alpha * acc + jnp.dot(p, vaug_scr[hh, pl.ds(start, s.shape[1]), :], preferred_element_type=F32)
        return m_new, acc

    def chunk(j, carry):
        start = pl.multiple_of(j * tk, tk)
        sc = [scores(qa[hh], hh, start, tk) for hh in range(2)]
        return tuple(update(sc[hh], hh, start, *carry[hh], None) for hh in range(2))

    start = pl.multiple_of(qi * tq, tq)
    hr = tq // DIAG_SPLIT
    blocks = [(hh, r) for hh in range(2) for r in range(DIAG_SPLIT)]
    sc = [scores(qa[hh][r * hr:(r + 1) * hr], hh, start, (r + 1) * hr) for hh, r in blocks]
    first = [update(s, hh, start, jnp.full((hr, 1), -jnp.inf, F32), jnp.zeros((hr, 2 * hd), F32), r * hr)
             for s, (hh, r) in zip(sc, blocks)]
    carry = tuple((jnp.max(jnp.concatenate([jnp.broadcast_to(m, (hr, 2 * hd))
                                            for m, _ in first[hh * DIAG_SPLIT:(hh + 1) * DIAG_SPLIT]], axis=0),
                           axis=-1, keepdims=True),
                   jnp.concatenate([a for _, a in first[hh * DIAG_SPLIT:(hh + 1) * DIAG_SPLIT]], axis=0))
                  for hh in range(2))

    thresholds, q_norms = [], []
    for hh in range(2):
        qf = jnp.where((lane // hd) == hh, qa[hh].astype(F32), 0.0)
        q_norms.append(jnp.sqrt(jnp.max(jnp.sum(qf * qf, axis=-1, keepdims=True))))
        thresholds.append(jnp.min(carry[hh][0]) - SKIP_MARGIN)
    first_j = jnp.int32(0)
    for j in range(nk - per_tile):
        skippable = j < qi * per_tile
        for hh in range(2):
            skippable = skippable & (q_norms[hh] * kmax_scr[hh, j] + bmax_scr[hh, j] < thresholds[hh])
        first_j = jnp.where(skippable & (first_j == j), j + 1, first_j)

    carry = lax.fori_loop(first_j, qi * per_tile, chunk, carry)
    outs = []
    for hh in range(2):
        acc = carry[hh][1]
        outs.append(acc / acc[:, ones_lane[hh]:ones_lane[hh] + 1])
    o_ref[...] = jnp.where(lane < hd, outs[0], outs[1])


def _fox(q, kt, v, c_bht, nb, t, tq, tk):
    n = q.shape[0]
    nq = t // tq
    hw = 2 * HEAD_DIM
    return pl.pallas_call(
        functools.partial(_fox_kernel, tq=tq, tk=tk),
        grid=(nb, FOX_HEADS // 2, nq),
        in_specs=[
            pl.BlockSpec((tq, hw), lambda b, h, i: (b * nq + i, h)),
            pl.BlockSpec((hw, t), lambda b, h, i: (h, b)),
            pl.BlockSpec((t, hw), lambda b, h, i: (b, h)),
            pl.BlockSpec((1, 2, 1, t), lambda b, h, i: (b, h, 0, 0)),
        ],
        out_specs=pl.BlockSpec((tq, hw), lambda b, h, i: (b * nq + i, h)),
        out_shape=jax.ShapeDtypeStruct((n, FOX_WIDTH), F32),
        scratch_shapes=[pltpu.VMEM((2, hw, t), BF16), pltpu.VMEM((2, t, hw), BF16),
                        pltpu.SMEM((2, t // tk), F32), pltpu.SMEM((2, t // tk), F32)],
        compiler_params=_params("arbitrary", "arbitrary", "arbitrary"),
        name="fox_attention",
    )(q, kt, v, c_bht.reshape(nb, FOX_HEADS, 1, t))


def _fox_cached_kernel(q_ref, kn_ref, vn_ref, kc_ref, vc_ref, c_ref, o_ref, *, past, t):
    row = lax.broadcasted_iota(jnp.int32, (t, t), 0)
    col = lax.broadcasted_iota(jnp.int32, (t, t), 1)
    q = q_ref[...] * FOX_SCALE
    kn_all = kn_ref[...]
    vn_all = vn_ref[...]
    outs = []
    for h in range(FOX_HEADS):
        lanes = slice(h * HEAD_DIM, (h + 1) * HEAD_DIM)
        qh = q[:, lanes].astype(BF16)
        kc = kc_ref[pl.ds(h, past, stride=FOX_HEADS), :].astype(BF16)
        vc = vc_ref[pl.ds(h, past, stride=FOX_HEADS), :].astype(BF16)
        kn = kn_all[:, lanes].astype(BF16)
        vn = vn_all[:, lanes].astype(BF16)
        s1 = lax.dot_general(qh, kc, NT_DIMS, preferred_element_type=F32)
        s1 = s1 - c_ref[0, h, :, 0:past]
        s2 = lax.dot_general(qh, kn, NT_DIMS, preferred_element_type=F32)
        s2 = s2 - c_ref[0, h, :, past:past + t]
        s2 = jnp.where(col <= row, s2, -jnp.inf)
        m = jnp.maximum(jnp.max(s1, axis=-1, keepdims=True), jnp.max(s2, axis=-1, keepdims=True))
        p1 = jnp.exp(s1 - m)
        p2 = jnp.exp(s2 - m)
        l = jnp.sum(p1, axis=-1, keepdims=True) + jnp.sum(p2, axis=-1, keepdims=True)
        acc = (jnp.dot(p1.astype(BF16), vc, preferred_element_type=F32)
               + jnp.dot(p2.astype(BF16), vn, preferred_element_type=F32))
        outs.append(acc / l)
    o_ref[...] = jnp.concatenate(outs, axis=1)


def _fox_cached(q, k, v, cache_k, cache_v, layer, c_bhl, nb, t):
    n = q.shape[0]
    past = cache_k.shape[2]
    rows = past * FOX_HEADS
    cache_k = cache_k.reshape(DEPTH, nb, rows, HEAD_DIM)
    cache_v = cache_v.reshape(DEPTH, nb, rows, HEAD_DIM)
    new = pl.BlockSpec((t, FOX_WIDTH), lambda b: (b, 0))
    old = pl.BlockSpec((None, None, rows, HEAD_DIM), lambda b: (layer, b, 0, 0))
    return pl.pallas_call(
        functools.partial(_fox_cached_kernel, past=past, t=t),
        grid=(nb,),
        in_specs=[new, new, new, old, old,
                  pl.BlockSpec((1, FOX_HEADS, 1, past + t), lambda b: (b, 0, 0, 0))],
        out_specs=new,
        out_shape=jax.ShapeDtypeStruct((n, FOX_WIDTH), F32),
        compiler_params=_params("arbitrary"),
        name="fox_attention_cached",
    )(q, k, v, cache_k, cache_v, c_bhl.reshape(nb, FOX_HEADS, 1, past + t))


def _levels(chunk):
    out, m = [], chunk // 2
    while m >= 1:
        out.append(m)
        m //= 2
    return out


FINE_LEVEL = 4


def _decay_sum_matrix(chunk):
    r = np.arange(chunk)[:, None]
    j = np.arange(chunk)[None, :]
    incl = (j <= r).astype(np.float32)
    blocks = [incl]
    for m in _levels(chunk):
        if m <= FINE_LEVEL:
            mid = (r // (2 * m)) * (2 * m) + m - 1
            blocks.append(incl - (j <= mid).astype(np.float32))
    return np.concatenate(blocks, axis=0)


def _level_ids(chunk):
    levels = _levels(chunk)
    t = np.arange(chunk)
    x = t[:, None] ^ t[None, :]
    m = np.where(x > 0, 2 ** np.floor(np.log2(np.maximum(x, 1))).astype(np.int64), 0)
    lvl = np.full((chunk, chunk), len(levels) + 1, np.int32)
    for li, lm in enumerate(levels):
        lvl[(t[:, None] > t[None, :]) & (m == lm)] = li
    lvl[x == 0] = len(levels)
    return lvl


def _level_ids_wide(chunk, heads):
    return np.tile(_level_ids(chunk), (1, heads))


def _two_pass(mat_bf16, x):
    hi, lo = _split(x)
    d = lambda u: jnp.dot(mat_bf16, u, preferred_element_type=F32)
    return d(hi) + d(lo)


def _hgrn_kernel(p_ref, lb_ref, ng_ref, s0_ref, dmat_ref, lvl_ref, ones_ref, o_ref, st_ref, st_scr, *, chunk,
                 nchunks, group):
    ti = pl.program_id(1)

    @pl.when(ti == 0)
    def _():
        st_scr[...] = s0_ref[0]

    w = HG_WIDTH
    levels = _levels(chunk)
    lb = lb_ref[...]
    log_lb = jnp.log(lb)
    log_1m_lb = jnp.log1p(-lb)
    row = lax.broadcasted_iota(jnp.int32, (chunk, 1), 0)
    lane_head = lax.broadcasted_iota(jnp.int32, (1, w), 1) // HEAD_DIM
    row_head = lax.broadcasted_iota(jnp.int32, (w, 1), 0) // HEAD_DIM

    head_lanes = [(lane_head == h).astype(BF16) for h in range(HG_HEADS)]

    def stack(x):
        xb = x.astype(BF16)
        return jnp.concatenate([xb * m for m in head_lanes], axis=0)

    def each(fn, *lists):
        return [fn(*xs) for xs in zip(*lists)]

    def chunk_group(gi, _):
        base = gi * (group * chunk)
        rows = [pl.ds(pl.multiple_of(base + i * chunk, chunk), chunk) for i in range(group)]
        q = [p_ref[r, 0:w] for r in rows]
        fx = [p_ref[r, w:2 * w] for r in rows]
        val = [p_ref[r, 2 * w:3 * w] for r in rows]
        e = each(lambda x: jnp.exp(-jnp.abs(x)), fx)
        bv = each(lambda x, ei: log_1m_lb + (jnp.minimum(x, 0.0) - jnp.log1p(ei)), fx, e)
        logf = each(lambda x: jnp.maximum(log_lb, x) + jnp.log1p(jnp.exp(-jnp.abs(log_lb - x))), bv)
        kd = each(lambda x, ei: (1.0 - lb) * (jnp.where(x >= 0.0, ei, 1.0) / (1.0 + ei)), fx, e)
        sums = each(lambda x: _two_pass(dmat_ref[...], x), logf)
        b = [x[0:chunk] for x in sums]

        lvl = lvl_ref[...]
        attn = each(lambda qi, ki: jnp.where(lvl == len(levels), _mm(qi, stack(ki), NT_DIMS), 0.0), q, kd)
        fine = 0
        for li, m in enumerate(levels):
            if m > FINE_LEVEL:
                def rel(bi):
                    mids = [jnp.broadcast_to(bi[s0 + m - 1:s0 + m, :], (2 * m, w)) for s0 in range(0, chunk, 2 * m)]
                    return bi - jnp.concatenate(mids, axis=0)
                x = each(rel, b)
            else:
                fine += 1
                x = [sm[fine * chunk:(fine + 1) * chunk] for sm in sums]
            wgt = each(lambda xi: jnp.exp(-jnp.abs(xi)), x)
            second = ((row // m) % 2) == 1
            ql = each(lambda qi, wi: jnp.where(second, qi * wi, 0.0), q, wgt)
            kl = each(lambda ki, wi: jnp.where(second, 0.0, ki * wi), kd, wgt)
            attn = each(lambda at, qi, ki: jnp.where(lvl == li, _mm(qi, stack(ki), NT_DIMS), at), attn, ql, kl)
        o_intra = each(lambda at, vi: _mm(at, stack(vi)), attn, val)
        q_dec = each(lambda qi, bi: qi * jnp.exp(bi), q, b)
        b_last = [bi[chunk - 1:chunk, :] for bi in b]
        upd = each(lambda vi, ki, bl, bi: _mm(vi, ki * jnp.exp(bl - bi), TN_DIMS), val, kd, b_last, b)

        st = st_scr[...]
        for i in range(group):
            o = _mm(q_dec[i], st, NT_DIMS) + o_intra[i]
            st = jnp.exp(b_last[i]) * st + jnp.where(row_head == lane_head, upd[i], 0.0)
            ms = _head_sum(o * o, ones_ref[...]) * (1.0 / HEAD_DIM)
            g = p_ref[rows[i], 3 * w:4 * w]
            o_ref[rows[i], :] = o * lax.rsqrt(ms + EPS) * ng_ref[...] * (g * _sigmoid(g))
        st_scr[...] = st
        return 0

    lax.fori_loop(0, nchunks // group, chunk_group, 0)

    @pl.when(ti == pl.num_programs(1) - 1)
    def _():
        st_ref[0] = st_scr[...]


def _hgrn(p_hg, lb, norm_g, st0, nb, t, chunk, tile):
    n = p_hg.shape[0]
    nt = t // tile
    w = HG_WIDTH
    dmat = jnp.asarray(_decay_sum_matrix(chunk), BF16)
    lvl = jnp.asarray(_level_ids_wide(chunk, HG_HEADS))
    ones = jnp.asarray(_block_ones_np(w), BF16)
    return pl.pallas_call(
        functools.partial(_hgrn_kernel, chunk=chunk, nchunks=tile // chunk, group=min(4, tile // chunk)),
        grid=(nb, nt),
        in_specs=[
            pl.BlockSpec((tile, HG_COLS), lambda b, i: (b * nt + i, 0)),
            _const_spec((1, w)), _const_spec((1, w)),
            pl.BlockSpec((1, w, w), lambda b, i: (b, 0, 0)),
            _const_spec(dmat.shape), _const_spec(lvl.shape), _const_spec(ones.shape),
        ],
        out_specs=[pl.BlockSpec((tile, w), lambda b, i: (b * nt + i, 0)),
                   pl.BlockSpec((1, w, w), lambda b, i: (b, 0, 0))],
        out_shape=[jax.ShapeDtypeStruct((n, w), F32), jax.ShapeDtypeStruct((nb, w, w), F32)],
        scratch_shapes=[pltpu.VMEM((w, w), F32)],
        compiler_params=_params("arbitrary", "arbitrary"),
        name="hgrn2",
    )(p_hg, lb, norm_g, st0, dmat, lvl, ones)


def _rwkv_kernel(p_ref, prev_ref, s0_ref, mu_ref, vec_ref, w2_ref, a2_ref, g2_ref, ltri_ref, ones_ref,
                 lvlw_ref, y_ref, st_ref, st_scr, prev_scr, *, chunk, nchunks, group):
    ti = pl.program_id(1)

    @pl.when(ti == 0)
    def _():
        st_scr[...] = s0_ref[0]
        prev_scr[...] = prev_ref[0]

    w = RW_WIDTH
    hc = RW_HEADS * chunk
    levels = _levels(chunk)
    nl = len(levels)
    w0 = vec_ref[0:1, :]
    a0 = vec_ref[1:2, :]
    k_k = vec_ref[2:3, :]
    k_a = vec_ref[3:4, :]
    r_k = vec_ref[4:5, :]
    ln_w = vec_ref[5:6, :]
    ln_b = vec_ref[6:7, :]
    row = lax.broadcasted_iota(jnp.int32, (chunk, 1), 0)
    lane_head = lax.broadcasted_iota(jnp.int32, (1, w), 1) // HEAD_DIM
    row_id = lax.broadcasted_iota(jnp.int32, (w, 1), 0)
    lane_id = lax.broadcasted_iota(jnp.int32, (1, w), 1)
    row_head = row_id // HEAD_DIM
    wide_head = lax.broadcasted_iota(jnp.int32, (1, hc), 1) // chunk

    head_lanes = [(lane_head == h).astype(BF16) for h in range(RW_HEADS)]
    head_lanes2 = [jnp.concatenate([m, m], axis=1) for m in head_lanes]
    wide_lanes = [(wide_head == h).astype(BF16) for h in range(RW_HEADS)]

    def stack(x):
        xb = x.astype(BF16)
        masks = head_lanes if x.shape[1] == w else head_lanes2
        return jnp.concatenate([xb * m for m in masks], axis=0)

    def blockdiag(xw):
        xb = xw.astype(BF16)
        return jnp.concatenate([xb * m for m in wide_lanes], axis=0)

    def each(fn, *lists):
        return [fn(*xs) for xs in zip(*lists)]

    def chunk_group(gi, _):
        base = gi * (group * chunk)
        rows = [pl.ds(pl.multiple_of(base + i * chunk, chunk), chunk) for i in range(group)]
        ones_bd = ones_ref[...]
        mu = mu_ref[...]
        ps = [p_ref[r, :] for r in rows]
        prevs = [prev_scr[...]] + [p[chunk - 1:chunk, :] for p in ps[:-1]]
        prev_scr[...] = ps[-1][chunk - 1:chunk, :]
        xs = each(lambda p, pv: p + mu * (jnp.where(row == 0, pv, pltpu.roll(p, 1, 0)) - p), ps, prevs)
        r = [x[:, 0:w] for x in xs]
        k = [x[:, w:2 * w] for x in xs]
        v = [x[:, 2 * w:3 * w] for x in xs]
        tail = [x[:, 3 * w:] for x in xs]
        w_lin = each(lambda t: w0 + _mm3(jnp.tanh(t), w2_ref[...]), tail)
        a = each(lambda t: _sigmoid(a0 + _mm(t, a2_ref[...])), tail)
        g = each(lambda t: _mm(_sigmoid(t), g2_ref[...]), tail)
        logw = each(lambda wl: -jnp.exp(-(jnp.maximum(-wl, 0.0) + jnp.log1p(jnp.exp(-jnp.abs(wl)))) - 0.5), w_lin)
        kk = each(lambda x: x * k_k, k)
        kk = each(lambda x: x / jnp.maximum(jnp.sqrt(_head_sum(x * x, ones_bd)), 1e-12), kk)
        kmod = each(lambda x, ai: x * (1.0 + (ai - 1.0) * k_a), k, a)
        beta = each(lambda x, ai: x * ai, kk, a)

        lg = each(lambda x: _two_pass(ltri_ref[...], x), logw)
        lg_end = [x[chunk - 1:chunk, :] for x in lg]
        a_n = each(lambda x, l, lw: x * jnp.exp(l - lw), kk, lg, logw)
        r_n = each(lambda x, l: x * jnp.exp(l), r, lg)
        inv_g = each(lambda l: jnp.exp(-l), lg)
        to_end = each(lambda le, l: jnp.exp(le - l), lg_end, lg)
        v_s = each(stack, v)

        gram = each(lambda an, rn, be, km, ig: _mm(jnp.concatenate([an, rn], axis=0),
                                                   jnp.concatenate([stack(be * ig), stack(km * ig)], axis=0),
                                                   NT_DIMS), a_n, r_n, beta, kmod, inv_g)
        lvlw = lvlw_ref[...]
        strict = lvlw < nl
        incl = lvlw <= nl
        n_w = [jnp.where(strict, x[0:chunk, 0:hc], 0.0).astype(BF16) for x in gram]
        m_k = [jnp.where(strict, x[0:chunk, hc:2 * hc], 0.0) for x in gram]
        l_b = [jnp.where(incl, x[chunk:2 * chunk, 0:hc], 0.0) for x in gram]
        l_k = [jnp.where(incl, x[chunk:2 * chunk, hc:2 * hc], 0.0) for x in gram]

        t_w = each(lambda n: (jnp.where(lvlw == nl, 1.0, 0.0)
                              - jnp.where(lvlw == nl - 1, n, jnp.zeros_like(n)).astype(F32)), n_w)
        blk = 2
        while blk < chunk:
            li = levels.index(blk)
            level = (lvlw == li).astype(BF16)
            x_w = each(lambda t, n: _mm(t, blockdiag(n * level)), t_w, n_w)
            t_w = each(lambda t, x: t - _mm(x, blockdiag(t)), t_w, x_w)
            blk *= 2

        rhs = each(lambda an, mk, vs: jnp.concatenate([an, _mm(mk, vs)], axis=1), a_n, m_k, v_s)
        x0 = each(lambda t, rh: _mm(t, stack(rh)), t_w, rhs)

        def residual(n, rh, x):
            x_hi, x_lo = _split(x)
            return rh - x - (jnp.dot(n, stack(x_hi), preferred_element_type=F32)
                             + jnp.dot(n, stack(x_lo), preferred_element_type=F32))

        res = each(residual, n_w, rhs, x0)
        sol = each(lambda x, t, rs: x + _mm(t, stack(rs)), x0, t_w, res)
        a_hat = [x[:, 0:w] for x in sol]
        e0 = [x[:, w:2 * w] for x in sol]
        lb_x = each(lambda lb, x: _mm(lb, stack(x)), l_b, sol)
        r_hat = each(lambda rn, lx: rn - lx[:, 0:w], r_n, lb_x)
        y0 = each(lambda lk, vs, lx: _mm(lk, vs) - lx[:, w:2 * w], l_k, v_s, lb_x)
        same_head = row_head == lane_head
        bh = each(lambda be, te: be * te, beta, to_end)
        kh = each(lambda km, te: km * te, kmod, to_end)
        trans = each(lambda ah, b_, le: (jnp.where(row_id == lane_id, jnp.exp(le), 0.0)
                                         - jnp.where(same_head, _mm(ah, b_, TN_DIMS), 0.0)), a_hat, bh, lg_end)
        q0 = each(lambda vi, ei, k_, b_: jnp.where(same_head, _mm(jnp.concatenate([vi, ei], axis=0),
                                                                   jnp.concatenate([k_, -b_], axis=0), TN_DIMS), 0.0),
                  v, e0, kh, bh)

        st = st_scr[...]
        for i in range(group):
            y = _mm(r_hat[i], st, NT_DIMS) + y0[i]
            st = _mm(st, trans[i]) + q0[i]

            inv_n = 1.0 / HEAD_DIM
            mean = _head_sum(y, ones_bd) * inv_n
            yc = y - mean
            var = _head_sum(yc * yc, ones_bd) * inv_n
            yn = yc * lax.rsqrt(var + RW_GN_EPS) * ln_w + ln_b
            bonus = _head_sum(r[i] * kmod[i] * r_k, ones_bd) * v[i]
            y_ref[rows[i], :] = (yn + bonus) * g[i]
        st_scr[...] = st
        return 0

    lax.fori_loop(0, nchunks // group, chunk_group, 0)

    @pl.when(ti == pl.num_programs(1) - 1)
    def _():
        st_ref[0] = st_scr[...]


def _rwkv(p_rw, prev, st0, mu, vecs, w2p, a2p, g2p, nb, t, chunk, tile):
    n = p_rw.shape[0]
    nt = t // tile
    w = RW_WIDTH
    consts = (jnp.asarray(np.tril(np.ones((chunk, chunk), np.float32)), BF16),
              jnp.asarray(_block_ones_np(w), BF16),
              jnp.asarray(_level_ids_wide(chunk, RW_HEADS)))
    return pl.pallas_call(
        functools.partial(_rwkv_kernel, chunk=chunk, nchunks=tile // chunk, group=min(8, tile // chunk)),
        grid=(nb, nt),
        in_specs=[
            pl.BlockSpec((tile, RW_COLS), lambda b, i: (b * nt + i, 0)),
            pl.BlockSpec((1, 1, RW_COLS), lambda b, i: (b, 0, 0)),
            pl.BlockSpec((1, w, w), lambda b, i: (b, 0, 0)),
            _const_spec((1, RW_COLS)), _const_spec((8, w)),
            _const_spec((RW_LORA, w)), _const_spec((RW_LORA, w)), _const_spec((RW_LORA, w)),
        ] + [_const_spec(c.shape) for c in consts],
        out_specs=[pl.BlockSpec((tile, w), lambda b, i: (b * nt + i, 0)),
                   pl.BlockSpec((1, w, w), lambda b, i: (b, 0, 0))],
        out_shape=[jax.ShapeDtypeStruct((n, w), F32), jax.ShapeDtypeStruct((nb, w, w), F32)],
        scratch_shapes=[pltpu.VMEM((w, w), F32), pltpu.VMEM((1, RW_COLS), F32)],
        compiler_params=_params("arbitrary", "arbitrary"),
        name="rwkv7",
    )(p_rw, prev, st0, mu, vecs, w2p, a2p, g2p, *consts)


def _outffn_kernel(yrw_ref, yfox_ref, yhg_ref, x_ref, ga1_ref, sh2_ref, sc2_ref, ga2_ref, g2_ref,
                   worw_ref, wofox_ref, wohg_ref, wg_ref, wu_ref, wo_ref, fg_ref, o_ref, *, final_norm, ff_step):
    mix = (jnp.dot(yrw_ref[...].astype(BF16), worw_ref[...], preferred_element_type=F32)
           + jnp.dot(yfox_ref[...].astype(BF16), wofox_ref[...], preferred_element_type=F32)
           + jnp.dot(yhg_ref[...].astype(BF16), wohg_ref[...], preferred_element_type=F32))
    x1 = x_ref[...] + ga1_ref[0] * mix
    ms = jnp.mean(x1 * x1, axis=-1, keepdims=True)
    h2 = x1 * lax.rsqrt(ms + EPS) * g2_ref[...]
    h2 = (h2 * (1.0 + sc2_ref[0]) + sh2_ref[0]).astype(BF16)
    ffn = jnp.zeros_like(x1)
    for j in range(D_FF // ff_step):
        cols = slice(j * ff_step, (j + 1) * ff_step)
        gate = jnp.dot(h2, wg_ref[:, cols], preferred_element_type=F32)
        up = jnp.dot(h2, wu_ref[:, cols], preferred_element_type=F32)
        act = (gate * _sigmoid(gate) * up).astype(BF16)
        ffn = ffn + jnp.dot(act, wo_ref[cols, :], preferred_element_type=F32)
    x2 = x1 + ga2_ref[0] * ffn
    if final_norm:
        ms2 = jnp.mean(x2 * x2, axis=-1, keepdims=True)
        x2 = x2 * lax.rsqrt(ms2 + EPS) * fg_ref[...]
    o_ref[...] = x2


def _outffn(y_rw, y_fox, y_hg, x, ga1, sh2, sc2, ga2, g2, wts, final_g, final_norm, tm, tiles_per_batch):
    n = x.shape[0]
    row = lambda w: pl.BlockSpec((tm, w), lambda i: (i, 0))
    mods = [_mod_spec(m, tiles_per_batch) for m in (ga1, sh2, sc2, ga2)]
    return pl.pallas_call(
        functools.partial(_outffn_kernel, final_norm=final_norm, ff_step=D_FF // 2),
        grid=(n // tm,),
        in_specs=[row(RW_WIDTH), row(FOX_WIDTH), row(HG_WIDTH), row(D_MODEL)] + mods
        + [_const_spec((1, D_MODEL))] + [_const_spec(w.shape) for w in wts] + [_const_spec((1, D_MODEL))],
        out_specs=row(D_MODEL),
        out_shape=jax.ShapeDtypeStruct((n, D_MODEL), F32),
        compiler_params=_params("arbitrary"),
        name="outffn",
    )(y_rw, y_fox, y_hg, x, ga1, sh2, sc2, ga2, g2, *wts, final_g)


def _block_diag_t(s):
    nb, h, dk, dv = s.shape
    st = jnp.swapaxes(s, 2, 3)
    eye = jnp.eye(h, dtype=s.dtype)
    return jnp.einsum("bhvk,hg->bhvgk", st, eye).reshape(nb, h * dv, h * dk)


def _block_diag(s):
    nb, h, dv, dk = s.shape
    eye = jnp.eye(h, dtype=s.dtype)
    return jnp.einsum("bhvk,hg->bhvgk", s, eye).reshape(nb, h * dv, h * dk)


def _diag_blocks(sbd, h):
    nb, rows, cols = sbd.shape
    x = sbd.reshape(nb, h, rows // h, h, cols // h)
    return jnp.stack([x[:, i, :, i, :] for i in range(h)], axis=1)


def _layer_weights(l, w_in, rw_w2, rw_a2, rw_g2, w_out, w_ffn_in, w_ffn_out):
    wi = w_in[l]
    c0, c1 = RW_COLS, RW_COLS + 3 * FOX_WIDTH
    proj = (wi[:, :c0].astype(BF16), wi[:, c0:c1].astype(BF16),
            wi[:, c0 + FOX_WIDTH:c0 + 2 * FOX_WIDTH].T.astype(BF16),
            wi[:, c1:c1 + FOX_HEADS].T.astype(BF16), wi[:, c1 + FOX_HEADS:].astype(BF16))
    zeros = lambda r: jnp.zeros((r, RW_WIDTH), F32)
    w2p = jnp.concatenate([rw_w2[l], zeros(RW_LORA - RW_W_LORA)], axis=0)
    a2p = jnp.concatenate([zeros(RW_W_LORA), rw_a2[l], zeros(RW_G_LORA)], axis=0)
    g2p = jnp.concatenate([zeros(RW_W_LORA + RW_A_LORA), rw_g2[l]], axis=0)
    wo = w_out[l].astype(BF16)
    ffn = (wo[:RW_WIDTH], wo[RW_WIDTH:RW_WIDTH + FOX_WIDTH], wo[RW_WIDTH + FOX_WIDTH:],
           w_ffn_in[l][:, :D_FF].astype(BF16), w_ffn_in[l][:, D_FF:].astype(BF16), w_ffn_out[l].astype(BF16))
    return proj, (w2p, a2p, g2p), ffn


def _group_layer(x, mods, lw, lp, layer, prev, rw_st0, hg_st0, cache, nb, t, cfg, final_g, final_norm):
    proj_w, lora_w, ffn_w = lw
    sh1, sc1, ga1, sh2, sc2, ga2 = mods
    tm, tpb = cfg["tm"], cfg["tiles_per_batch"]
    p_rw, q, k, v, kt, f, p_hg = _normproj(x, sh1, sc1, lp["norm1_g"], proj_w, tm, tpb, cache is None)
    if f.shape[0] == nb:
        f_bht = f
    else:
        f_bht = jnp.swapaxes(f.reshape(FOX_HEADS, nb, t), 0, 1)
    if cache is None:
        logf_bht, c_bhl = _gate(f_bht, lp["fox_b_f"], None)
        y_fox = _fox(q, kt, v, c_bhl, nb, t, cfg["tq"], cfg["tk"])
    else:
        cache_k, cache_v, cache_logf = cache
        logf_bht, c_bhl = _gate(f_bht, lp["fox_b_f"], jnp.swapaxes(cache_logf, 1, 2))
        y_fox = _fox_cached(q, k, v, cache_k, cache_v, layer, c_bhl, nb, t)
    y_hg, hg_st = _hgrn(p_hg, lp["hg_lb"], lp["hg_norm_g"], hg_st0, nb, t, cfg["hg_chunk"], cfg["hg_tile"])
    y_rw, rw_st = _rwkv(p_rw, prev, rw_st0, lp["rw_mu"], lp["rw_vecs"], *lora_w, nb, t,
                        cfg["rw_chunk"], cfg["rw_tile"])
    x = _outffn(y_rw, y_fox, y_hg, x, ga1, sh2, sc2, ga2, lp["norm2_g"], ffn_w, final_g, final_norm, tm, tpb)
    states = (
        k.reshape(nb, t, FOX_HEADS, HEAD_DIM), v.reshape(nb, t, FOX_HEADS, HEAD_DIM),
        jnp.swapaxes(logf_bht, 1, 2),
        _diag_blocks(rw_st, RW_HEADS),
        p_rw.reshape(nb, t, RW_COLS)[:, t - 1:t, :],
        jnp.swapaxes(_diag_blocks(hg_st, HG_HEADS), 2, 3),
    )
    return x, states


def kernel(x_prompt, x_sample, c_prompt, c_sample, cache_fox_k, cache_fox_v, cache_fox_logf, state_rwkv,
           state_rwkv_shift, state_hgrn, norm1_g, w_ada, b_ada, w_in, rw_mu, rw_w0, rw_w2, rw_a0, rw_a2, rw_g2,
           rw_k_k, rw_k_a, rw_r_k, rw_ln_w, rw_ln_b, fox_b_f, hg_lb_logits, hg_norm_g, w_out, norm2_g,
           w_ffn_in, w_ffn_out, final_norm_g):
    dt = x_prompt.dtype
    bp, tp, _ = x_prompt.shape
    bs, ts, _ = x_sample.shape

    mod = _modulation(jnp.concatenate([c_prompt, c_sample], axis=0).astype(F32), w_ada, b_ada)
    lbs = _hg_lower_bounds(hg_lb_logits.astype(F32))

    cfg_p = dict(tm=512, tiles_per_batch=tp // 512, tq=512, tk=512, hg_chunk=HG_CHUNK, hg_tile=512,
                 rw_chunk=RW_CHUNK, rw_tile=512)
    cfg_s = dict(tm=bs * ts, tiles_per_batch=1, hg_chunk=ts, hg_tile=ts, rw_chunk=ts, rw_tile=ts)

    xp = x_prompt.reshape(bp * tp, D_MODEL).astype(F32)
    xs = x_sample.reshape(bs * ts, D_MODEL).astype(F32)
    zero_prev = jnp.zeros((bp, 1, RW_COLS), F32)
    zero_state = jnp.zeros((bp, RW_WIDTH, RW_WIDTH), F32)
    final_g = final_norm_g.reshape(1, D_MODEL)
    row = lambda a: a.reshape(1, -1)

    outs_p, outs_s = [], []
    for l in range(DEPTH):
        lw = _layer_weights(l, w_in, rw_w2, rw_a2, rw_g2, w_out, w_ffn_in, w_ffn_out)
        zero_row = jnp.zeros((RW_WIDTH,), F32)
        lp = dict(
            norm1_g=row(norm1_g[l]), norm2_g=row(norm2_g[l]), fox_b_f=fox_b_f[l],
            hg_lb=row(lbs[l]), hg_norm_g=row(hg_norm_g[l]), rw_mu=row(rw_mu[l]),
            rw_vecs=jnp.stack([rw_w0[l], rw_a0[l], rw_k_k[l], rw_k_a[l], rw_r_k[l].reshape(-1),
                               rw_ln_w[l], rw_ln_b[l], zero_row], axis=0),
        )
        chunks_p = [m[:, None, :] for m in jnp.split(mod[l, :bp], 6, axis=-1)]
        chunks_s = [jnp.repeat(m, ts, axis=0)[None] for m in jnp.split(mod[l, bp:], 6, axis=-1)]
        last = l == DEPTH - 1
        xp, st_p = _group_layer(xp, chunks_p, lw, lp, l, zero_prev, zero_state, zero_state, None,
                                bp, tp, cfg_p, final_g, last)
        xs, st_s = _group_layer(xs, chunks_s, lw, lp, l, state_rwkv_shift[l].astype(F32),
                                _block_diag(state_rwkv[l].astype(F32)),
                                _block_diag_t(state_hgrn[l].astype(F32)),
                                (cache_fox_k.astype(F32), cache_fox_v.astype(F32), cache_fox_logf[l].astype(F32)),
                                bs, ts, cfg_s, final_g, last)
        outs_p.append(st_p)
        outs_s.append(st_s)

    stk = lambda outs, j: jnp.stack([o[j] for o in outs], axis=0).astype(dt)
    y_prompt = xp.reshape(bp, tp, D_MODEL).astype(dt)
    y_sample = xs.reshape(bs, ts, D_MODEL).astype(dt)
    return (y_prompt, y_sample,
            stk(outs_p, 0), stk(outs_p, 1), stk(outs_p, 2), stk(outs_p, 3), stk(outs_p, 4), stk(outs_p, 5),
            stk(outs_s, 0), stk(outs_s, 1), stk(outs_s, 2), stk(outs_s, 3), stk(outs_s, 4), stk(outs_s, 5))
```

```python
import functools

import numpy as np
import jax
import jax.numpy as jnp
from jax import lax
from jax.experimental import pallas as pl
from jax.experimental.pallas import tpu as pltpu

F32 = jnp.float32
BF16 = jnp.bfloat16
HIGHEST = lax.Precision.HIGHEST

D_MODEL = 1024
DEPTH = 4
HEAD_DIM = 64
RW_HEADS = 4
RW_WIDTH = RW_HEADS * HEAD_DIM
FOX_HEADS = 8
FOX_WIDTH = FOX_HEADS * HEAD_DIM
HG_HEADS = 4
HG_WIDTH = HG_HEADS * HEAD_DIM
RW_W_LORA = 32
RW_A_LORA = 32
RW_G_LORA = 64
RW_LORA = RW_W_LORA + RW_A_LORA + RW_G_LORA
RW_COLS = 3 * RW_WIDTH + RW_LORA
FOX_COLS = 3 * FOX_WIDTH + FOX_HEADS
HG_COLS = 4 * HG_WIDTH
D_FF = 2816
EPS = 1e-6
RW_GN_EPS = 64e-5
FOX_SCALE = HEAD_DIM ** -0.5
HG_CHUNK = 64
RW_CHUNK = 64

VMEM_LIMIT_BYTES = 56 * 1024 * 1024

NT_DIMS = (((1,), (1,)), ((), ()))
TN_DIMS = (((0,), (0,)), ((), ()))
NN_DIMS = (((1,), (0,)), ((), ()))


def _params(*sem):
    return pltpu.CompilerParams(dimension_semantics=sem, vmem_limit_bytes=VMEM_LIMIT_BYTES)


def _const_spec(shape):
    nd = len(shape)
    return pl.BlockSpec(shape, lambda *_: (0,) * nd, pipeline_mode=pl.Buffered(1))


def _mm(a, b, dims=NN_DIMS):
    return lax.dot_general(a.astype(BF16), b.astype(BF16), dims, preferred_element_type=F32)


def _split(a):
    hi = a.astype(BF16)
    lo = (a - hi.astype(F32)).astype(BF16)
    return hi, lo


def _mm3(a, b, dims=NN_DIMS):
    ah, al = _split(a)
    bh, bl = _split(b)
    d = lambda x, y: lax.dot_general(x, y, dims, preferred_element_type=F32)
    return d(ah, bh) + (d(ah, bl) + d(al, bh))


def _head_sum(x, ones_bd):
    hi, lo = _split(x)
    d = lambda u: jnp.dot(u, ones_bd, preferred_element_type=F32)
    return d(hi) + d(lo)


def _block_ones_np(width):
    r = np.arange(width) // HEAD_DIM
    return (r[:, None] == r[None, :]).astype(np.float32)


def _log_sigmoid(z):
    return jnp.minimum(z, 0.0) - jnp.log1p(jnp.exp(-jnp.abs(z)))


def _sigmoid(z):
    return 1.0 / (1.0 + jnp.exp(-z))


def _mod_kernel(c_ref, w_ref, b_ref, o_ref):
    c = c_ref[...]
    s = c * _sigmoid(c)
    o_ref[0] = jnp.dot(s, w_ref[0], precision=HIGHEST, preferred_element_type=F32) + b_ref[0]


def _modulation(c_all, w_ada, b_ada):
    nb = c_all.shape[0]
    tn = 1536
    n_out = w_ada.shape[2]
    return pl.pallas_call(
        _mod_kernel,
        grid=(DEPTH, n_out // tn),
        in_specs=[
            pl.BlockSpec((nb, D_MODEL), lambda l, j: (0, 0)),
            pl.BlockSpec((1, D_MODEL, tn), lambda l, j: (l, 0, j)),
            pl.BlockSpec((1, 1, tn), lambda l, j: (l, 0, j)),
        ],
        out_specs=pl.BlockSpec((1, nb, tn), lambda l, j: (l, 0, j)),
        out_shape=jax.ShapeDtypeStruct((DEPTH, nb, n_out), F32),
        compiler_params=_params("arbitrary", "arbitrary"),
        name="adaln_mod",
    )(c_all, w_ada, b_ada.reshape(DEPTH, 1, n_out))


def _lbs_kernel(x_ref, o_ref):
    x = x_ref[...]
    m = jnp.max(x, axis=0, keepdims=True)
    e = jnp.exp(x - m)
    p = e / jnp.sum(e, axis=0, keepdims=True)
    acc = jnp.zeros_like(p[0:1])
    rows = []
    for l in range(DEPTH):
        acc = acc + p[l:l + 1]
        rows.append(acc)
    cs = jnp.concatenate(rows, axis=0)
    o_ref[...] = cs - cs[0:1]


def _hg_lower_bounds(logits):
    return pl.pallas_call(
        _lbs_kernel,
        out_shape=jax.ShapeDtypeStruct(logits.shape, F32),
        name="hgrn_lower_bounds",
    )(logits)


def _normproj_kernel(x_ref, sh_ref, sc_ref, g_ref, wrw_ref, wq_ref, wkv_ref, wkvt_ref, wft_ref, whg_ref,
                     prw_ref, q_ref, k_ref, v_ref, kt_ref, vt_ref, ft_ref, phg_ref, *, transposed_kv):
    x = x_ref[...]
    ms = jnp.mean(x * x, axis=-1, keepdims=True)
    h = x * lax.rsqrt(ms + EPS) * g_ref[...]
    h = h * (1.0 + sc_ref[0]) + sh_ref[0]
    hb = h.astype(BF16)
    prw_ref[...] = jnp.dot(hb, wrw_ref[...], preferred_element_type=F32)
    q_ref[...] = jnp.dot(hb, wq_ref[...], preferred_element_type=F32)
    if transposed_kv:
        kvt = lax.dot_general(wkvt_ref[...], hb, NT_DIMS, preferred_element_type=F32)
        kt_ref[...] = kvt[:FOX_WIDTH]
        vt_ref[...] = kvt[FOX_WIDTH:]
        k_ref[...] = jnp.zeros(k_ref.shape, F32)
        v_ref[...] = jnp.zeros(v_ref.shape, F32)
    else:
        kv = jnp.dot(hb, wkv_ref[...], preferred_element_type=F32)
        k_ref[...] = kv[:, :FOX_WIDTH]
        v_ref[...] = kv[:, FOX_WIDTH:]
        kt_ref[...] = jnp.zeros(kt_ref.shape, F32)
        vt_ref[...] = jnp.zeros(vt_ref.shape, F32)
    ft_ref[0] = lax.dot_general(wft_ref[...], hb, NT_DIMS, preferred_element_type=F32)
    phg_ref[...] = jnp.dot(hb, whg_ref[...], preferred_element_type=F32)


def _mod_spec(arr, tiles_per_batch):
    _, r, d = arr.shape
    return pl.BlockSpec((1, r, d), lambda i: (i // tiles_per_batch, 0, 0))


def _normproj(x, sh, sc, g, wts, tm, tiles_per_batch, transposed_kv):
    n = x.shape[0]
    wrw, wq, wkv, wkvt, wft, whg = wts
    row = lambda w: pl.BlockSpec((tm, w), lambda i: (i, 0))
    dummy = pl.BlockSpec((8, 128), lambda i: (0, 0))
    dummy_shape = jax.ShapeDtypeStruct((8, 128), F32)
    nbat = n // (tm * tiles_per_batch)
    plain = jax.ShapeDtypeStruct((n, FOX_WIDTH), F32)
    flipped = jax.ShapeDtypeStruct((nbat, FOX_WIDTH, tm * tiles_per_batch), F32)
    flipped_spec = pl.BlockSpec((None, FOX_WIDTH, tm), lambda i: (i // tiles_per_batch, 0, i % tiles_per_batch))
    kv_specs = [dummy, dummy, flipped_spec, flipped_spec] if transposed_kv else [row(FOX_WIDTH)] * 2 + [dummy] * 2
    kv_shapes = [dummy_shape, dummy_shape, flipped, flipped] if transposed_kv else [plain, plain, dummy_shape,
                                                                                   dummy_shape]
    return pl.pallas_call(
        functools.partial(_normproj_kernel, transposed_kv=transposed_kv),
        grid=(n // tm,),
        in_specs=[
            row(D_MODEL), _mod_spec(sh, tiles_per_batch), _mod_spec(sc, tiles_per_batch),
            _const_spec((1, D_MODEL)),
            _const_spec(wrw.shape), _const_spec(wq.shape), _const_spec(wkv.shape), _const_spec(wkvt.shape),
            _const_spec(wft.shape), _const_spec(whg.shape),
        ],
        out_specs=[row(RW_COLS), row(FOX_WIDTH)] + kv_specs + [
            pl.BlockSpec((1, FOX_HEADS, tm), lambda i: (i // tiles_per_batch, 0, i % tiles_per_batch)),
            row(HG_COLS)],
        out_shape=[jax.ShapeDtypeStruct((n, RW_COLS), F32), plain] + kv_shapes + [
            jax.ShapeDtypeStruct((nbat, FOX_HEADS, tm * tiles_per_batch), F32),
            jax.ShapeDtypeStruct((n, HG_COLS), F32)],
        compiler_params=_params("arbitrary"),
        name="normproj",
    )(x, sh, sc, g, wrw, wq, wkv, wkvt, wft, whg)


def _cumsum_lanes(x, carry):
    length = x.shape[1]
    step = min(512, length)
    r = lax.broadcasted_iota(jnp.int32, (step, step), 0)
    c = lax.broadcasted_iota(jnp.int32, (step, step), 1)
    upper = (r <= c).astype(F32)
    outs = []
    for j in range(length // step):
        seg = x[:, j * step:(j + 1) * step]
        cs = jnp.dot(seg, upper, precision=HIGHEST, preferred_element_type=F32) + carry
        carry = cs[:, step - 1:step]
        outs.append(cs)
    return outs, carry, step


def _gate_kernel(*refs, past):
    if past:
        f_ref, bf_ref, cache_ref, logf_ref, c_ref = refs
    else:
        f_ref, bf_ref, logf_ref, c_ref = refs
    logf = _log_sigmoid(f_ref[0] + bf_ref[...])
    logf_ref[0] = logf
    carry = jnp.zeros((FOX_HEADS, 1), F32)
    if past:
        outs, carry, step = _cumsum_lanes(cache_ref[0], carry)
        for j, cs in enumerate(outs):
            c_ref[0, :, j * step:(j + 1) * step] = cs
    outs, carry, step = _cumsum_lanes(logf, carry)
    for j, cs in enumerate(outs):
        c_ref[0, :, past + j * step:past + (j + 1) * step] = cs


def _gate(f_bht, b_f, cache_logf_bhp):
    nb, _, t = f_bht.shape
    past = 0 if cache_logf_bhp is None else cache_logf_bhp.shape[2]
    blk = lambda w: pl.BlockSpec((1, FOX_HEADS, w), lambda b: (b, 0, 0))
    in_specs = [blk(t), _const_spec((FOX_HEADS, 1))]
    args = [f_bht, b_f.reshape(FOX_HEADS, 1)]
    if past:
        in_specs.append(blk(past))
        args.append(cache_logf_bhp)
    return pl.pallas_call(
        functools.partial(_gate_kernel, past=past),
        grid=(nb,),
        in_specs=in_specs,
        out_specs=[blk(t), blk(past + t)],
        out_shape=[jax.ShapeDtypeStruct((nb, FOX_HEADS, t), F32),
                   jax.ShapeDtypeStruct((nb, FOX_HEADS, past + t), F32)],
        compiler_params=_params("arbitrary"),
        name="fox_gate",
    )(*args)


LOG2E = 1.4426950408889634
BIAS_ROWS = 16
DIAG_SPLIT = 4
SKIP_MARGIN = 137.0


def _fox_kernel(q_ref, kt_ref, vt_ref, c_ref, o_ref, kaug_scr, vaug_scr, kmax_scr, bmax_scr, *, tq, tk):
    qi = pl.program_id(2)
    hd = HEAD_DIM
    nk = kaug_scr.shape[2] // tk
    per_tile = tq // tk
    lane = lax.broadcasted_iota(jnp.int32, (1, 2 * hd), 1)
    bias_lane0 = (hd, 0)
    ones_lane = (hd, 0)

    @pl.when(qi == 0)
    def _():
        kt = kt_ref[...]
        v = vt_ref[...].T
        t = kt.shape[1]
        for hh in range(2):
            c2 = c_ref[0, hh] * LOG2E
            hi = c2.astype(BF16).astype(F32)
            mid = (c2 - hi).astype(BF16).astype(F32)
            lo = c2 - hi - mid
            bias = jnp.concatenate([-hi, -mid, -lo, jnp.zeros((BIAS_ROWS - 3, t), F32)], axis=0).astype(BF16)
            pad = jnp.zeros((hd - BIAS_ROWS, t), BF16)
            kh = kt[hh * hd:(hh + 1) * hd].astype(BF16)
            if hh == 0:
                kaug_scr[hh] = jnp.concatenate([kh, bias, pad], axis=0)
            else:
                kaug_scr[hh] = jnp.concatenate([bias, pad, kh], axis=0)
            own = (lane // hd) == hh
            vaug_scr[hh] = jnp.where(own, v, jnp.where(lane == ones_lane[hh], 1.0, 0.0)).astype(BF16)
            kf = kh.astype(F32)
            knorm = jnp.sqrt(jnp.sum(kf * kf, axis=0, keepdims=True))
            bsum = -(hi + mid + lo.astype(BF16).astype(F32))
            for j in range(nk):
                kmax_scr[hh, j] = jnp.max(knorm[:, j * tk:(j + 1) * tk])
                bmax_scr[hh, j] = jnp.max(bsum[:, j * tk:(j + 1) * tk])

    q = q_ref[...] * (FOX_SCALE * LOG2E)
    qa = []
    for hh in range(2):
        own = (lane // hd) == hh
        is_bias = (lane >= bias_lane0[hh]) & (lane < bias_lane0[hh] + 3)
        qa.append(jnp.where(own, q, jnp.where(is_bias, 1.0, 0.0)).astype(BF16))

    def scores(qh, hh, start, nk):
        return jnp.dot(qh, kaug_scr[hh, :, pl.ds(start, nk)], preferred_element_type=F32)

    def update(s, hh, start, m, acc, row0):
        if row0 is not None:
            rows_i = lax.broadcasted_iota(jnp.int32, s.shape, 0) + row0
            cols_i = lax.broadcasted_iota(jnp.int32, s.shape, 1)
            s = jnp.where(cols_i <= rows_i, s, -jnp.inf)
        width = 2 * hd
        part = s
        if s.shape[1] % width == 0:
            part = s[:, 0:width]
            for c0 in range(width, s.shape[1], width):
                part = jnp.maximum(part, s[:, c0:c0 + width])
        m_new = jnp.maximum(m, jnp.max(part, axis=-1, keepdims=True))
        alpha = jnp.exp2(m - m_new)
        p = jnp.exp2(s - m_new).astype(BF16)
        acc = alpha * acc + jnp.dot(p, vaug_scr[hh, pl.ds(start, s.shape[1]), :], preferred_element_type=F32)
        return m_new, acc

    def chunk(j, carry):
        start = pl.multiple_of(j * tk, tk)
        sc = [scores(qa[hh], hh, start, tk) for hh in range(2)]
        return tuple(update(sc[hh], hh, start, *carry[hh], None) for hh in range(2))

    start = pl.multiple_of(qi * tq, tq)
    hr = tq // DIAG_SPLIT
    blocks = [(hh, r) for hh in range(2) for r in range(DIAG_SPLIT)]
    sc = [scores(qa[hh][r * hr:(r + 1) * hr], hh, start, (r + 1) * hr) for hh, r in blocks]
    first = [update(s, hh, start, jnp.full((hr, 1), -jnp.inf, F32), jnp.zeros((hr, 2 * hd), F32), r * hr)
             for s, (hh, r) in zip(sc, blocks)]
    carry = tuple((jnp.max(jnp.concatenate([jnp.broadcast_to(m, (hr, 2 * hd))
                                            for m, _ in first[hh * DIAG_SPLIT:(hh + 1) * DIAG_SPLIT]], axis=0),
                           axis=-1, keepdims=True),
                   jnp.concatenate([a for _, a in first[hh * DIAG_SPLIT:(hh + 1) * DIAG_SPLIT]], axis=0))
                  for hh in range(2))

    thresholds, q_norms = [], []
    for hh in range(2):
        qf = jnp.where((lane // hd) == hh, qa[hh].astype(F32), 0.0)
        q_norms.append(jnp.sqrt(jnp.max(jnp.sum(qf * qf, axis=-1, keepdims=True))))
        thresholds.append(jnp.min(carry[hh][0]) - SKIP_MARGIN)
    first_j = jnp.int32(0)
    for j in range(nk - per_tile):
        skippable = j < qi * per_tile
        for hh in range(2):
            skippable = skippable & (q_norms[hh] * kmax_scr[hh, j] + bmax_scr[hh, j] < thresholds[hh])
        first_j = jnp.where(skippable & (first_j == j), j + 1, first_j)

    carry = lax.fori_loop(first_j, qi * per_tile, chunk, carry)
    outs = []
    for hh in range(2):
        acc = carry[hh][1]
        outs.append(acc / acc[:, ones_lane[hh]:ones_lane[hh] + 1])
    o_ref[...] = jnp.where(lane < hd, outs[0], outs[1])


def _fox(q, kt, vt, c_bht, nb, t, tq, tk):
    n = q.shape[0]
    nq = t // tq
    hw = 2 * HEAD_DIM
    return pl.pallas_call(
        functools.partial(_fox_kernel, tq=tq, tk=tk),
        grid=(nb, FOX_HEADS // 2, nq),
        in_specs=[
            pl.BlockSpec((tq, hw), lambda b, h, i: (b * nq + i, h)),
            pl.BlockSpec((None, hw, t), lambda b, h, i: (b, h, 0)),
            pl.BlockSpec((None, hw, t), lambda b, h, i: (b, h, 0)),
            pl.BlockSpec((1, 2, 1, t), lambda b, h, i: (b, h, 0, 0)),
        ],
        out_specs=pl.BlockSpec((tq, hw), lambda b, h, i: (b * nq + i, h)),
        out_shape=jax.ShapeDtypeStruct((n, FOX_WIDTH), F32),
        scratch_shapes=[pltpu.VMEM((2, hw, t), BF16), pltpu.VMEM((2, t, hw), BF16),
                        pltpu.SMEM((2, t // tk), F32), pltpu.SMEM((2, t // tk), F32)],
        compiler_params=_params("arbitrary", "arbitrary", "arbitrary"),
        name="fox_attention",
    )(q, kt, vt, c_bht.reshape(nb, FOX_HEADS, 1, t))


def _fox_cached_kernel(q_ref, kn_ref, vn_ref, kc_ref, vc_ref, c_ref, o_ref, *, past, t):
    row = lax.broadcasted_iota(jnp.int32, (t, t), 0)
    col = lax.broadcasted_iota(jnp.int32, (t, t), 1)
    q = q_ref[...] * FOX_SCALE
    kn_all = kn_ref[...]
    vn_all = vn_ref[...]
    outs = []
    for h in range(FOX_HEADS):
        lanes = slice(h * HEAD_DIM, (h + 1) * HEAD_DIM)
        qh = q[:, lanes].astype(BF16)
        kc = kc_ref[h * HEAD_DIM:(h + 1) * HEAD_DIM, :].astype(BF16)
        vc = vc_ref[h * HEAD_DIM:(h + 1) * HEAD_DIM, :].astype(BF16)
        kn = kn_all[:, lanes].astype(BF16)
        vn = vn_all[:, lanes].astype(BF16)
        s1 = jnp.dot(qh, kc, preferred_element_type=F32)
        s1 = s1 - c_ref[0, h, :, 0:past]
        s2 = lax.dot_general(qh, kn, NT_DIMS, preferred_element_type=F32)
        s2 = s2 - c_ref[0, h, :, past:past + t]
        s2 = jnp.where(col <= row, s2, -jnp.inf)
        m = jnp.maximum(jnp.max(s1, axis=-1, keepdims=True), jnp.max(s2, axis=-1, keepdims=True))
        p1 = jnp.exp(s1 - m)
        p2 = jnp.exp(s2 - m)
        l = jnp.sum(p1, axis=-1, keepdims=True) + jnp.sum(p2, axis=-1, keepdims=True)
        acc = (lax.dot_general(p1.astype(BF16), vc, NT_DIMS, preferred_element_type=F32)
               + jnp.dot(p2.astype(BF16), vn, preferred_element_type=F32))
        outs.append(acc / l)
    o_ref[...] = jnp.concatenate(outs, axis=1)


def _fox_cached(q, k, v, cache_k, cache_v, layer, c_bhl, nb, t):
    n = q.shape[0]
    past = cache_k.shape[2]
    flip = lambda c: jnp.transpose(c, (0, 1, 3, 4, 2)).reshape(DEPTH, nb, FOX_WIDTH, past)
    cache_k, cache_v = flip(cache_k), flip(cache_v)
    new = pl.BlockSpec((t, FOX_WIDTH), lambda b: (b, 0))
    old = pl.BlockSpec((None, None, FOX_WIDTH, past), lambda b: (layer, b, 0, 0))
    return pl.pallas_call(
        functools.partial(_fox_cached_kernel, past=past, t=t),
        grid=(nb,),
        in_specs=[new, new, new, old, old,
                  pl.BlockSpec((1, FOX_HEADS, 1, past + t), lambda b: (b, 0, 0, 0))],
        out_specs=new,
        out_shape=jax.ShapeDtypeStruct((n, FOX_WIDTH), F32),
        compiler_params=_params("arbitrary"),
        name="fox_attention_cached",
    )(q, k, v, cache_k, cache_v, c_bhl.reshape(nb, FOX_HEADS, 1, past + t))


def _levels(chunk):
    out, m = [], chunk // 2
    while m >= 1:
        out.append(m)
        m //= 2
    return out


FINE_LEVEL = 4


def _decay_sum_matrix(chunk):
    r = np.arange(chunk)[:, None]
    j = np.arange(chunk)[None, :]
    incl = (j <= r).astype(np.float32)
    blocks = [incl]
    for m in _levels(chunk):
        if m <= FINE_LEVEL:
            mid = (r // (2 * m)) * (2 * m) + m - 1
            blocks.append(incl - (j <= mid).astype(np.float32))
    return np.concatenate(blocks, axis=0)


def _level_ids(chunk):
    levels = _levels(chunk)
    t = np.arange(chunk)
    x = t[:, None] ^ t[None, :]
    m = np.where(x > 0, 2 ** np.floor(np.log2(np.maximum(x, 1))).astype(np.int64), 0)
    lvl = np.full((chunk, chunk), len(levels) + 1, np.int32)
    for li, lm in enumerate(levels):
        lvl[(t[:, None] > t[None, :]) & (m == lm)] = li
    lvl[x == 0] = len(levels)
    return lvl


def _level_ids_wide(chunk, heads):
    return np.tile(_level_ids(chunk), (1, heads))


def _two_pass(mat_bf16, x):
    hi, lo = _split(x)
    d = lambda u: jnp.dot(mat_bf16, u, preferred_element_type=F32)
    return d(hi) + d(lo)


def _hgrn_kernel(p_ref, lb_ref, ng_ref, s0_ref, dmat_ref, lvl_ref, ones_ref, o_ref, st_ref, st_scr, *, chunk,
                 nchunks, group):
    ti = pl.program_id(1)

    @pl.when(ti == 0)
    def _():
        st_scr[...] = s0_ref[0]

    w = HG_WIDTH
    levels = _levels(chunk)
    lb = lb_ref[...]
    log_lb = jnp.log(lb)
    log_1m_lb = jnp.log1p(-lb)
    row = lax.broadcasted_iota(jnp.int32, (chunk, 1), 0)
    lane_head = lax.broadcasted_iota(jnp.int32, (1, w), 1) // HEAD_DIM
    row_head = lax.broadcasted_iota(jnp.int32, (w, 1), 0) // HEAD_DIM

    head_lanes = [(lane_head == h).astype(BF16) for h in range(HG_HEADS)]

    def stack(x):
        xb = x.astype(BF16)
        return jnp.concatenate([xb * m for m in head_lanes], axis=0)

    def each(fn, *lists):
        return [fn(*xs) for xs in zip(*lists)]

    def chunk_group(gi, _):
        base = gi * (group * chunk)
        rows = [pl.ds(pl.multiple_of(base + i * chunk, chunk), chunk) for i in range(group)]
        q = [p_ref[r, 0:w] for r in rows]
        fx = [p_ref[r, w:2 * w] for r in rows]
        val = [p_ref[r, 2 * w:3 * w] for r in rows]
        e = each(lambda x: jnp.exp(-jnp.abs(x)), fx)
        bv = each(lambda x, ei: log_1m_lb + (jnp.minimum(x, 0.0) - jnp.log1p(ei)), fx, e)
        logf = each(lambda x: jnp.maximum(log_lb, x) + jnp.log1p(jnp.exp(-jnp.abs(log_lb - x))), bv)
        kd = each(lambda x, ei: (1.0 - lb) * (jnp.where(x >= 0.0, ei, 1.0) / (1.0 + ei)), fx, e)
        sums = each(lambda x: _two_pass(dmat_ref[...], x), logf)
        b = [x[0:chunk] for x in sums]

        lvl = lvl_ref[...]
        attn = each(lambda qi, ki: jnp.where(lvl == len(levels), _mm(qi, stack(ki), NT_DIMS), 0.0), q, kd)
        fine = 0
        for li, m in enumerate(levels):
            if m > FINE_LEVEL:
                def rel(bi):
                    mids = [jnp.broadcast_to(bi[s0 + m - 1:s0 + m, :], (2 * m, w)) for s0 in range(0, chunk, 2 * m)]
                    return bi - jnp.concatenate(mids, axis=0)
                x = each(rel, b)
            else:
                fine += 1
                x = [sm[fine * chunk:(fine + 1) * chunk] for sm in sums]
            wgt = each(lambda xi: jnp.exp(-jnp.abs(xi)), x)
            second = ((row // m) % 2) == 1
            ql = each(lambda qi, wi: jnp.where(second, qi * wi, 0.0), q, wgt)
            kl = each(lambda ki, wi: jnp.where(second, 0.0, ki * wi), kd, wgt)
            attn = each(lambda at, qi, ki: jnp.where(lvl == li, _mm(qi, stack(ki), NT_DIMS), at), attn, ql, kl)
        o_intra = each(lambda at, vi: _mm(at, stack(vi)), attn, val)
        q_dec = each(lambda qi, bi: qi * jnp.exp(bi), q, b)
        b_last = [bi[chunk - 1:chunk, :] for bi in b]
        upd = each(lambda vi, ki, bl, bi: _mm(vi, ki * jnp.exp(bl - bi), TN_DIMS), val, kd, b_last, b)

        st = st_scr[...]
        for i in range(group):
            o = _mm(q_dec[i], st, NT_DIMS) + o_intra[i]
            st = jnp.exp(b_last[i]) * st + jnp.where(row_head == lane_head, upd[i], 0.0)
            ms = _head_sum(o * o, ones_ref[...]) * (1.0 / HEAD_DIM)
            g = p_ref[rows[i], 3 * w:4 * w]
            o_ref[rows[i], :] = o * lax.rsqrt(ms + EPS) * ng_ref[...] * (g * _sigmoid(g))
        st_scr[...] = st
        return 0

    lax.fori_loop(0, nchunks // group, chunk_group, 0)

    @pl.when(ti == pl.num_programs(1) - 1)
    def _():
        st_ref[0] = st_scr[...]


def _hgrn(p_hg, lb, norm_g, st0, nb, t, chunk, tile):
    n = p_hg.shape[0]
    nt = t // tile
    w = HG_WIDTH
    dmat = jnp.asarray(_decay_sum_matrix(chunk), BF16)
    lvl = jnp.asarray(_level_ids_wide(chunk, HG_HEADS))
    ones = jnp.asarray(_block_ones_np(w), BF16)
    return pl.pallas_call(
        functools.partial(_hgrn_kernel, chunk=chunk, nchunks=tile // chunk, group=min(4, tile // chunk)),
        grid=(nb, nt),
        in_specs=[
            pl.BlockSpec((tile, HG_COLS), lambda b, i: (b * nt + i, 0)),
            _const_spec((1, w)), _const_spec((1, w)),
            pl.BlockSpec((1, w, w), lambda b, i: (b, 0, 0)),
            _const_spec(dmat.shape), _const_spec(lvl.shape), _const_spec(ones.shape),
        ],
        out_specs=[pl.BlockSpec((tile, w), lambda b, i: (b * nt + i, 0)),
                   pl.BlockSpec((1, w, w), lambda b, i: (b, 0, 0))],
        out_shape=[jax.ShapeDtypeStruct((n, w), F32), jax.ShapeDtypeStruct((nb, w, w), F32)],
        scratch_shapes=[pltpu.VMEM((w, w), F32)],
        compiler_params=_params("arbitrary", "arbitrary"),
        name="hgrn2",
    )(p_hg, lb, norm_g, st0, dmat, lvl, ones)


def _rwkv_kernel(p_ref, prev_ref, s0_ref, mu_ref, vec_ref, w2_ref, a2_ref, g2_ref, ltri_ref, ones_ref,
                 lvlw_ref, y_ref, st_ref, st_scr, prev_scr, *, chunk, nchunks, group):
    ti = pl.program_id(1)

    @pl.when(ti == 0)
    def _():
        st_scr[...] = s0_ref[0]
        prev_scr[...] = prev_ref[0]

    w = RW_WIDTH
    hc = RW_HEADS * chunk
    levels = _levels(chunk)
    nl = len(levels)
    w0 = vec_ref[0:1, :]
    a0 = vec_ref[1:2, :]
    k_k = vec_ref[2:3, :]
    k_a = vec_ref[3:4, :]
    r_k = vec_ref[4:5, :]
    ln_w = vec_ref[5:6, :]
    ln_b = vec_ref[6:7, :]
    row = lax.broadcasted_iota(jnp.int32, (chunk, 1), 0)
    lane_head = lax.broadcasted_iota(jnp.int32, (1, w), 1) // HEAD_DIM
    row_id = lax.broadcasted_iota(jnp.int32, (w, 1), 0)
    lane_id = lax.broadcasted_iota(jnp.int32, (1, w), 1)
    row_head = row_id // HEAD_DIM
    wide_head = lax.broadcasted_iota(jnp.int32, (1, hc), 1) // chunk

    head_lanes = [(lane_head == h).astype(BF16) for h in range(RW_HEADS)]
    head_lanes2 = [jnp.concatenate([m, m], axis=1) for m in head_lanes]
    wide_lanes = [(wide_head == h).astype(BF16) for h in range(RW_HEADS)]

    def stack(x):
        xb = x.astype(BF16)
        masks = head_lanes if x.shape[1] == w else head_lanes2
        return jnp.concatenate([xb * m for m in masks], axis=0)

    def blockdiag(xw):
        xb = xw.astype(BF16)
        return jnp.concatenate([xb * m for m in wide_lanes], axis=0)

    def each(fn, *lists):
        return [fn(*xs) for xs in zip(*lists)]

    def chunk_group(gi, _):
        base = gi * (group * chunk)
        rows = [pl.ds(pl.multiple_of(base + i * chunk, chunk), chunk) for i in range(group)]
        ones_bd = ones_ref[...]
        mu = mu_ref[...]
        ps = [p_ref[r, :] for r in rows]
        prevs = [prev_scr[...]] + [p[chunk - 1:chunk, :] for p in ps[:-1]]
        prev_scr[...] = ps[-1][chunk - 1:chunk, :]
        xs = each(lambda p, pv: p + mu * (jnp.where(row == 0, pv, pltpu.roll(p, 1, 0)) - p), ps, prevs)
        r = [x[:, 0:w] for x in xs]
        k = [x[:, w:2 * w] for x in xs]
        v = [x[:, 2 * w:3 * w] for x in xs]
        tail = [x[:, 3 * w:] for x in xs]
        w_lin = each(lambda t: w0 + _mm3(jnp.tanh(t), w2_ref[...]), tail)
        a = each(lambda t: _sigmoid(a0 + _mm(t, a2_ref[...])), tail)
        g = each(lambda t: _mm(_sigmoid(t), g2_ref[...]), tail)
        logw = each(lambda wl: -jnp.exp(-(jnp.maximum(-wl, 0.0) + jnp.log1p(jnp.exp(-jnp.abs(wl)))) - 0.5), w_lin)
        kk = each(lambda x: x * k_k, k)
        kk = each(lambda x: x / jnp.maximum(jnp.sqrt(_head_sum(x * x, ones_bd)), 1e-12), kk)
        kmod = each(lambda x, ai: x * (1.0 + (ai - 1.0) * k_a), k, a)
        beta = each(lambda x, ai: x * ai, kk, a)

        lg = each(lambda x: _two_pass(ltri_ref[...], x), logw)
        lg_end = [x[chunk - 1:chunk, :] for x in lg]
        a_n = each(lambda x, l, lw: x * jnp.exp(l - lw), kk, lg, logw)
        r_n = each(lambda x, l: x * jnp.exp(l), r, lg)
        inv_g = each(lambda l: jnp.exp(-l), lg)
        to_end = each(lambda le, l: jnp.exp(le - l), lg_end, lg)
        v_s = each(stack, v)

        gram = each(lambda an, rn, be, km, ig: _mm(jnp.concatenate([an, rn], axis=0),
                                                   jnp.concatenate([stack(be * ig), stack(km * ig)], axis=0),
                                                   NT_DIMS), a_n, r_n, beta, kmod, inv_g)
        lvlw = lvlw_ref[...]
        strict = lvlw < nl
        incl = lvlw <= nl
        n_w = [jnp.where(strict, x[0:chunk, 0:hc], 0.0).astype(BF16) for x in gram]
        m_k = [jnp.where(strict, x[0:chunk, hc:2 * hc], 0.0) for x in gram]
        l_b = [jnp.where(incl, x[chunk:2 * chunk, 0:hc], 0.0) for x in gram]
        l_k = [jnp.where(incl, x[chunk:2 * chunk, hc:2 * hc], 0.0) for x in gram]

        t_w = each(lambda n: (jnp.where(lvlw == nl, 1.0, 0.0)
                              - jnp.where(lvlw == nl - 1, n, jnp.zeros_like(n)).astype(F32)), n_w)
        blk = 2
        while blk < chunk:
            li = levels.index(blk)
            level = (lvlw == li).astype(BF16)
            x_w = each(lambda t, n: _mm(t, blockdiag(n * level)), t_w, n_w)
            t_w = each(lambda t, x: t - _mm(x, blockdiag(t)), t_w, x_w)
            blk *= 2

        rhs = each(lambda an, mk, vs: jnp.concatenate([an, _mm(mk, vs)], axis=1), a_n, m_k, v_s)
        x0 = each(lambda t, rh: _mm(t, stack(rh)), t_w, rhs)

        def residual(n, rh, x):
            x_hi, x_lo = _split(x)
            return rh - x - (jnp.dot(n, stack(x_hi), preferred_element_type=F32)
                             + jnp.dot(n, stack(x_lo), preferred_element_type=F32))

        res = each(residual, n_w, rhs, x0)
        sol = each(lambda x, t, rs: x + _mm(t, stack(rs)), x0, t_w, res)
        a_hat = [x[:, 0:w] for x in sol]
        e0 = [x[:, w:2 * w] for x in sol]
        lb_x = each(lambda lb, x: _mm(lb, stack(x)), l_b, sol)
        r_hat = each(lambda rn, lx: rn - lx[:, 0:w], r_n, lb_x)
        y0 = each(lambda lk, vs, lx: _mm(lk, vs) - lx[:, w:2 * w], l_k, v_s, lb_x)
        same_head = row_head == lane_head
        bh = each(lambda be, te: be * te, beta, to_end)
        kh = each(lambda km, te: km * te, kmod, to_end)
        trans = each(lambda ah, b_, le: (jnp.where(row_id == lane_id, jnp.exp(le), 0.0)
                                         - jnp.where(same_head, _mm(ah, b_, TN_DIMS), 0.0)), a_hat, bh, lg_end)
        q0 = each(lambda vi, ei, k_, b_: jnp.where(same_head, _mm(jnp.concatenate([vi, ei], axis=0),
                                                                   jnp.concatenate([k_, -b_], axis=0), TN_DIMS), 0.0),
                  v, e0, kh, bh)

        st = st_scr[...]
        for i in range(group):
            y = _mm(r_hat[i], st, NT_DIMS) + y0[i]
            st = _mm(st, trans[i]) + q0[i]

            inv_n = 1.0 / HEAD_DIM
            mean = _head_sum(y, ones_bd) * inv_n
            yc = y - mean
            var = _head_sum(yc * yc, ones_bd) * inv_n
            yn = yc * lax.rsqrt(var + RW_GN_EPS) * ln_w + ln_b
            bonus = _head_sum(r[i] * kmod[i] * r_k, ones_bd) * v[i]
            y_ref[rows[i], :] = (yn + bonus) * g[i]
        st_scr[...] = st
        return 0

    lax.fori_loop(0, nchunks // group, chunk_group, 0)

    @pl.when(ti == pl.num_programs(1) - 1)
    def _():
        st_ref[0] = st_scr[...]


def _rwkv(p_rw, prev, st0, mu, vecs, w2p, a2p, g2p, nb, t, chunk, tile):
    n = p_rw.shape[0]
    nt = t // tile
    w = RW_WIDTH
    consts = (jnp.asarray(np.tril(np.ones((chunk, chunk), np.float32)), BF16),
              jnp.asarray(_block_ones_np(w), BF16),
              jnp.asarray(_level_ids_wide(chunk, RW_HEADS)))
    return pl.pallas_call(
        functools.partial(_rwkv_kernel, chunk=chunk, nchunks=tile // chunk, group=min(8, tile // chunk)),
        grid=(nb, nt),
        in_specs=[
            pl.BlockSpec((tile, RW_COLS), lambda b, i: (b * nt + i, 0)),
            pl.BlockSpec((1, 1, RW_COLS), lambda b, i: (b, 0, 0)),
            pl.BlockSpec((1, w, w), lambda b, i: (b, 0, 0)),
            _const_spec((1, RW_COLS)), _const_spec((8, w)),
            _const_spec((RW_LORA, w)), _const_spec((RW_LORA, w)), _const_spec((RW_LORA, w)),
        ] + [_const_spec(c.shape) for c in consts],
        out_specs=[pl.BlockSpec((tile, w), lambda b, i: (b * nt + i, 0)),
                   pl.BlockSpec((1, w, w), lambda b, i: (b, 0, 0))],
        out_shape=[jax.ShapeDtypeStruct((n, w), F32), jax.ShapeDtypeStruct((nb, w, w), F32)],
        scratch_shapes=[pltpu.VMEM((w, w), F32), pltpu.VMEM((1, RW_COLS), F32)],
        compiler_params=_params("arbitrary", "arbitrary"),
        name="rwkv7",
    )(p_rw, prev, st0, mu, vecs, w2p, a2p, g2p, *consts)


def _outffn_kernel(yrw_ref, yfox_ref, yhg_ref, x_ref, ga1_ref, sh2_ref, sc2_ref, ga2_ref, g2_ref,
                   worw_ref, wofox_ref, wohg_ref, wg_ref, wu_ref, wo_ref, fg_ref, o_ref, *, final_norm, ff_step):
    mix = (jnp.dot(yrw_ref[...].astype(BF16), worw_ref[...], preferred_element_type=F32)
           + jnp.dot(yfox_ref[...].astype(BF16), wofox_ref[...], preferred_element_type=F32)
           + jnp.dot(yhg_ref[...].astype(BF16), wohg_ref[...], preferred_element_type=F32))
    x1 = x_ref[...] + ga1_ref[0] * mix
    ms = jnp.mean(x1 * x1, axis=-1, keepdims=True)
    h2 = x1 * lax.rsqrt(ms + EPS) * g2_ref[...]
    h2 = (h2 * (1.0 + sc2_ref[0]) + sh2_ref[0]).astype(BF16)
    ffn = jnp.zeros_like(x1)
    for j in range(D_FF // ff_step):
        cols = slice(j * ff_step, (j + 1) * ff_step)
        gate = jnp.dot(h2, wg_ref[:, cols], preferred_element_type=F32)
        up = jnp.dot(h2, wu_ref[:, cols], preferred_element_type=F32)
        act = (gate * _sigmoid(gate) * up).astype(BF16)
        ffn = ffn + jnp.dot(act, wo_ref[cols, :], preferred_element_type=F32)
    x2 = x1 + ga2_ref[0] * ffn
    if final_norm:
        ms2 = jnp.mean(x2 * x2, axis=-1, keepdims=True)
        x2 = x2 * lax.rsqrt(ms2 + EPS) * fg_ref[...]
    o_ref[...] = x2


def _outffn(y_rw, y_fox, y_hg, x, ga1, sh2, sc2, ga2, g2, wts, final_g, final_norm, tm, tiles_per_batch):
    n = x.shape[0]
    row = lambda w: pl.BlockSpec((tm, w), lambda i: (i, 0))
    mods = [_mod_spec(m, tiles_per_batch) for m in (ga1, sh2, sc2, ga2)]
    return pl.pallas_call(
        functools.partial(_outffn_kernel, final_norm=final_norm, ff_step=D_FF // 2),
        grid=(n // tm,),
        in_specs=[row(RW_WIDTH), row(FOX_WIDTH), row(HG_WIDTH), row(D_MODEL)] + mods
        + [_const_spec((1, D_MODEL))] + [_const_spec(w.shape) for w in wts] + [_const_spec((1, D_MODEL))],
        out_specs=row(D_MODEL),
        out_shape=jax.ShapeDtypeStruct((n, D_MODEL), F32),
        compiler_params=_params("arbitrary"),
        name="outffn",
    )(y_rw, y_fox, y_hg, x, ga1, sh2, sc2, ga2, g2, *wts, final_g)


def _block_diag_t(s):
    nb, h, dk, dv = s.shape
    st = jnp.swapaxes(s, 2, 3)
    eye = jnp.eye(h, dtype=s.dtype)
    return jnp.einsum("bhvk,hg->bhvgk", st, eye).reshape(nb, h * dv, h * dk)


def _block_diag(s):
    nb, h, dv, dk = s.shape
    eye = jnp.eye(h, dtype=s.dtype)
    return jnp.einsum("bhvk,hg->bhvgk", s, eye).reshape(nb, h * dv, h * dk)


def _diag_blocks(sbd, h):
    nb, rows, cols = sbd.shape
    x = sbd.reshape(nb, h, rows // h, h, cols // h)
    return jnp.stack([x[:, i, :, i, :] for i in range(h)], axis=1)


def _layer_weights(l, w_in, rw_w2, rw_a2, rw_g2, w_out, w_ffn_in, w_ffn_out):
    wi = w_in[l]
    c0, c1 = RW_COLS, RW_COLS + 3 * FOX_WIDTH
    w_kv = wi[:, c0 + FOX_WIDTH:c1].astype(BF16)
    proj = (wi[:, :c0].astype(BF16), wi[:, c0:c0 + FOX_WIDTH].astype(BF16), w_kv, w_kv.T,
            wi[:, c1:c1 + FOX_HEADS].T.astype(BF16), wi[:, c1 + FOX_HEADS:].astype(BF16))
    zeros = lambda r: jnp.zeros((r, RW_WIDTH), F32)
    w2p = jnp.concatenate([rw_w2[l], zeros(RW_LORA - RW_W_LORA)], axis=0)
    a2p = jnp.concatenate([zeros(RW_W_LORA), rw_a2[l], zeros(RW_G_LORA)], axis=0)
    g2p = jnp.concatenate([zeros(RW_W_LORA + RW_A_LORA), rw_g2[l]], axis=0)
    wo = w_out[l].astype(BF16)
    ffn = (wo[:RW_WIDTH], wo[RW_WIDTH:RW_WIDTH + FOX_WIDTH], wo[RW_WIDTH + FOX_WIDTH:],
           w_ffn_in[l][:, :D_FF].astype(BF16), w_ffn_in[l][:, D_FF:].astype(BF16), w_ffn_out[l].astype(BF16))
    return proj, (w2p, a2p, g2p), ffn


def _group_layer(x, mods, lw, lp, layer, prev, rw_st0, hg_st0, cache, nb, t, cfg, final_g, final_norm):
    proj_w, lora_w, ffn_w = lw
    sh1, sc1, ga1, sh2, sc2, ga2 = mods
    tm, tpb = cfg["tm"], cfg["tiles_per_batch"]
    p_rw, q, k, v, kt, vt, f, p_hg = _normproj(x, sh1, sc1, lp["norm1_g"], proj_w, tm, tpb, cache is None)
    if f.shape[0] == nb:
        f_bht = f
    else:
        f_bht = jnp.swapaxes(f.reshape(FOX_HEADS, nb, t), 0, 1)
    if cache is None:
        logf_bht, c_bhl = _gate(f_bht, lp["fox_b_f"], None)
        y_fox = _fox(q, kt, vt, c_bhl, nb, t, cfg["tq"], cfg["tk"])
        k, v = kt, vt
    else:
        cache_k, cache_v, cache_logf = cache
        logf_bht, c_bhl = _gate(f_bht, lp["fox_b_f"], jnp.swapaxes(cache_logf, 1, 2))
        y_fox = _fox_cached(q, k, v, cache_k, cache_v, layer, c_bhl, nb, t)
    y_hg, hg_st = _hgrn(p_hg, lp["hg_lb"], lp["hg_norm_g"], hg_st0, nb, t, cfg["hg_chunk"], cfg["hg_tile"])
    y_rw, rw_st = _rwkv(p_rw, prev, rw_st0, lp["rw_mu"], lp["rw_vecs"], *lora_w, nb, t,
                        cfg["rw_chunk"], cfg["rw_tile"])
    x = _outffn(y_rw, y_fox, y_hg, x, ga1, sh2, sc2, ga2, lp["norm2_g"], ffn_w, final_g, final_norm, tm, tpb)
    if cache is not None:
        k, v = k.reshape(nb, t, FOX_HEADS, HEAD_DIM), v.reshape(nb, t, FOX_HEADS, HEAD_DIM)
    states = (
        k, v,
        jnp.swapaxes(logf_bht, 1, 2),
        _diag_blocks(rw_st, RW_HEADS),
        p_rw.reshape(nb, t, RW_COLS)[:, t - 1:t, :],
        jnp.swapaxes(_diag_blocks(hg_st, HG_HEADS), 2, 3),
    )
    return x, states


def kernel(x_prompt, x_sample, c_prompt, c_sample, cache_fox_k, cache_fox_v, cache_fox_logf, state_rwkv,
           state_rwkv_shift, state_hgrn, norm1_g, w_ada, b_ada, w_in, rw_mu, rw_w0, rw_w2, rw_a0, rw_a2, rw_g2,
           rw_k_k, rw_k_a, rw_r_k, rw_ln_w, rw_ln_b, fox_b_f, hg_lb_logits, hg_norm_g, w_out, norm2_g,
           w_ffn_in, w_ffn_out, final_norm_g):
    dt = x_prompt.dtype
    bp, tp, _ = x_prompt.shape
    bs, ts, _ = x_sample.shape

    mod = _modulation(jnp.concatenate([c_prompt, c_sample], axis=0).astype(F32), w_ada, b_ada)
    lbs = _hg_lower_bounds(hg_lb_logits.astype(F32))

    cfg_p = dict(tm=512, tiles_per_batch=tp // 512, tq=512, tk=512, hg_chunk=HG_CHUNK, hg_tile=512,
                 rw_chunk=RW_CHUNK, rw_tile=512)
    cfg_s = dict(tm=bs * ts, tiles_per_batch=1, hg_chunk=ts, hg_tile=ts, rw_chunk=ts, rw_tile=ts)

    xp = x_prompt.reshape(bp * tp, D_MODEL).astype(F32)
    xs = x_sample.reshape(bs * ts, D_MODEL).astype(F32)
    zero_prev = jnp.zeros((bp, 1, RW_COLS), F32)
    zero_state = jnp.zeros((bp, RW_WIDTH, RW_WIDTH), F32)
    final_g = final_norm_g.reshape(1, D_MODEL)
    row = lambda a: a.reshape(1, -1)

    outs_p, outs_s = [], []
    for l in range(DEPTH):
        lw = _layer_weights(l, w_in, rw_w2, rw_a2, rw_g2, w_out, w_ffn_in, w_ffn_out)
        zero_row = jnp.zeros((RW_WIDTH,), F32)
        lp = dict(
            norm1_g=row(norm1_g[l]), norm2_g=row(norm2_g[l]), fox_b_f=fox_b_f[l],
            hg_lb=row(lbs[l]), hg_norm_g=row(hg_norm_g[l]), rw_mu=row(rw_mu[l]),
            rw_vecs=jnp.stack([rw_w0[l], rw_a0[l], rw_k_k[l], rw_k_a[l], rw_r_k[l].reshape(-1),
                               rw_ln_w[l], rw_ln_b[l], zero_row], axis=0),
        )
        chunks_p = [m[:, None, :] for m in jnp.split(mod[l, :bp], 6, axis=-1)]
        chunks_s = [jnp.repeat(m, ts, axis=0)[None] for m in jnp.split(mod[l, bp:], 6, axis=-1)]
        last = l == DEPTH - 1
        xp, st_p = _group_layer(xp, chunks_p, lw, lp, l, zero_prev, zero_state, zero_state, None,
                                bp, tp, cfg_p, final_g, last)
        xs, st_s = _group_layer(xs, chunks_s, lw, lp, l, state_rwkv_shift[l].astype(F32),
                                _block_diag(state_rwkv[l].astype(F32)),
                                _block_diag_t(state_hgrn[l].astype(F32)),
                                (cache_fox_k.astype(F32), cache_fox_v.astype(F32), cache_fox_logf[l].astype(F32)),
                                bs, ts, cfg_s, final_g, last)
        outs_p.append(st_p)
        outs_s.append(st_s)

    stk = lambda outs, j: jnp.stack([o[j] for o in outs], axis=0).astype(dt)
    y_prompt = xp.reshape(bp, tp, D_MODEL).astype(dt)
    y_sample = xs.reshape(bs, ts, D_MODEL).astype(dt)
    unflip = lambda a: jnp.transpose(a.reshape(DEPTH, bp, FOX_HEADS, HEAD_DIM, tp), (0, 1, 4, 2, 3))
    return (y_prompt, y_sample,
            unflip(stk(outs_p, 0)), unflip(stk(outs_p, 1)), stk(outs_p, 2), stk(outs_p, 3), stk(outs_p, 4), stk(outs_p, 5),
            stk(outs_s, 0), stk(outs_s, 1), stk(outs_s, 2), stk(outs_s, 3), stk(outs_s, 4), stk(outs_s, 5))
```

```python
import functools

import numpy as np
import jax
import jax.numpy as jnp
from jax import lax
from jax.experimental import pallas as pl
from jax.experimental.pallas import tpu as pltpu

F32 = jnp.float32
BF16 = jnp.bfloat16
HIGHEST = lax.Precision.HIGHEST

D_MODEL = 1024
DEPTH = 4
HEAD_DIM = 64
RW_HEADS = 4
RW_WIDTH = RW_HEADS * HEAD_DIM
FOX_HEADS = 8
FOX_WIDTH = FOX_HEADS * HEAD_DIM
HG_HEADS = 4
HG_WIDTH = HG_HEADS * HEAD_DIM
RW_W_LORA = 32
RW_A_LORA = 32
RW_G_LORA = 64
RW_LORA = RW_W_LORA + RW_A_LORA + RW_G_LORA
RW_COLS = 3 * RW_WIDTH + RW_LORA
FOX_COLS = 3 * FOX_WIDTH + FOX_HEADS
HG_COLS = 4 * HG_WIDTH
D_FF = 2816
EPS = 1e-6
RW_GN_EPS = 64e-5
FOX_SCALE = HEAD_DIM ** -0.5
HG_CHUNK = 64
RW_CHUNK = 64

VMEM_LIMIT_BYTES = 56 * 1024 * 1024

NT_DIMS = (((1,), (1,)), ((), ()))
TN_DIMS = (((0,), (0,)), ((), ()))
NN_DIMS = (((1,), (0,)), ((), ()))


def _params(*sem):
    return pltpu.CompilerParams(dimension_semantics=sem, vmem_limit_bytes=VMEM_LIMIT_BYTES)


def _const_spec(shape):
    nd = len(shape)
    return pl.BlockSpec(shape, lambda *_: (0,) * nd, pipeline_mode=pl.Buffered(1))


def _mm(a, b, dims=NN_DIMS):
    return lax.dot_general(a.astype(BF16), b.astype(BF16), dims, preferred_element_type=F32)


def _split(a):
    hi = a.astype(BF16)
    lo = (a - hi.astype(F32)).astype(BF16)
    return hi, lo


def _mm3(a, b, dims=NN_DIMS):
    ah, al = _split(a)
    bh, bl = _split(b)
    d = lambda x, y: lax.dot_general(x, y, dims, preferred_element_type=F32)
    return d(ah, bh) + (d(ah, bl) + d(al, bh))


def _head_sum(x, ones_bd):
    hi, lo = _split(x)
    d = lambda u: jnp.dot(u, ones_bd, preferred_element_type=F32)
    return d(hi) + d(lo)


def _block_ones_np(width):
    r = np.arange(width) // HEAD_DIM
    return (r[:, None] == r[None, :]).astype(np.float32)


def _log_sigmoid(z):
    return jnp.minimum(z, 0.0) - jnp.log1p(jnp.exp(-jnp.abs(z)))


def _sigmoid(z):
    return 1.0 / (1.0 + jnp.exp(-z))


def _mod_kernel(c_ref, w_ref, b_ref, o_ref):
    c = c_ref[...]
    s = c * _sigmoid(c)
    o_ref[0] = jnp.dot(s, w_ref[0], precision=HIGHEST, preferred_element_type=F32) + b_ref[0]


def _modulation(c_all, w_ada, b_ada):
    nb = c_all.shape[0]
    tn = 1536
    n_out = w_ada.shape[2]
    return pl.pallas_call(
        _mod_kernel,
        grid=(DEPTH, n_out // tn),
        in_specs=[
            pl.BlockSpec((nb, D_MODEL), lambda l, j: (0, 0)),
            pl.BlockSpec((1, D_MODEL, tn), lambda l, j: (l, 0, j)),
            pl.BlockSpec((1, 1, tn), lambda l, j: (l, 0, j)),
        ],
        out_specs=pl.BlockSpec((1, nb, tn), lambda l, j: (l, 0, j)),
        out_shape=jax.ShapeDtypeStruct((DEPTH, nb, n_out), F32),
        compiler_params=_params("arbitrary", "arbitrary"),
        name="adaln_mod",
    )(c_all, w_ada, b_ada.reshape(DEPTH, 1, n_out))


def _lbs_kernel(x_ref, o_ref):
    x = x_ref[...]
    m = jnp.max(x, axis=0, keepdims=True)
    e = jnp.exp(x - m)
    p = e / jnp.sum(e, axis=0, keepdims=True)
    acc = jnp.zeros_like(p[0:1])
    rows = []
    for l in range(DEPTH):
        acc = acc + p[l:l + 1]
        rows.append(acc)
    cs = jnp.concatenate(rows, axis=0)
    o_ref[...] = cs - cs[0:1]


def _hg_lower_bounds(logits):
    return pl.pallas_call(
        _lbs_kernel,
        out_shape=jax.ShapeDtypeStruct(logits.shape, F32),
        name="hgrn_lower_bounds",
    )(logits)


def _normproj_kernel(x_ref, sh_ref, sc_ref, g_ref, wrw_ref, wq_ref, wkv_ref, wkvt_ref, wft_ref, whg_ref, *refs,
                     transposed_kv):
    prw_ref, q_ref, k_ref, v_ref, kt_ref, vt_ref, ft_ref, phg_ref = refs[-8:]
    x = x_ref[...]
    ms = jnp.mean(x * x, axis=-1, keepdims=True)
    h = x * lax.rsqrt(ms + EPS) * g_ref[...]
    h = h * (1.0 + sc_ref[0]) + sh_ref[0]
    hb = h.astype(BF16)
    prw_ref[...] = jnp.dot(hb, wrw_ref[...], preferred_element_type=F32)
    q_ref[...] = jnp.dot(hb, wq_ref[...], preferred_element_type=F32)
    if transposed_kv:
        kvt = lax.dot_general(wkvt_ref[...], hb, NT_DIMS, preferred_element_type=F32)
        kt_ref[...] = kvt[:FOX_WIDTH]
        vt_ref[...] = kvt[FOX_WIDTH:]
        k_ref[...] = jnp.zeros(k_ref.shape, F32)
        v_ref[...] = jnp.zeros(v_ref.shape, F32)
    else:
        kv = jnp.dot(hb, wkv_ref[...], preferred_element_type=F32)
        k_ref[...] = kv[:, :FOX_WIDTH]
        v_ref[...] = kv[:, FOX_WIDTH:]
        kt_ref[...] = jnp.zeros(kt_ref.shape, F32)
        vt_ref[...] = jnp.zeros(vt_ref.shape, F32)
    ft_ref[0] = lax.dot_general(wft_ref[...], hb, NT_DIMS, preferred_element_type=F32)
    phg_ref[...] = jnp.dot(hb, whg_ref[...], preferred_element_type=F32)


def _mod_spec(arr, tiles_per_batch):
    _, r, d = arr.shape
    return pl.BlockSpec((1, r, d), lambda i: (i // tiles_per_batch, 0, 0))


def _normproj(x, sh, sc, g, wts, kv_stack, layer, tm, tiles_per_batch):
    n = x.shape[0]
    transposed_kv = kv_stack is not None
    wrw, wq, wkv, wkvt, wft, whg = wts
    row = lambda w: pl.BlockSpec((tm, w), lambda i: (i, 0))
    dummy = pl.BlockSpec((8, 128), lambda i: (0, 0))
    dummy_shape = jax.ShapeDtypeStruct((8, 128), F32)
    nbat = n // (tm * tiles_per_batch)
    plain = jax.ShapeDtypeStruct((n, FOX_WIDTH), F32)
    flipped = jax.ShapeDtypeStruct((DEPTH, nbat, FOX_WIDTH, tm * tiles_per_batch), F32)
    flipped_spec = pl.BlockSpec((None, None, FOX_WIDTH, tm),
                                lambda i: (layer, i // tiles_per_batch, 0, i % tiles_per_batch))
    stack_args = list(kv_stack) if transposed_kv else []
    stack_specs = [pl.BlockSpec(memory_space=pl.ANY)] * len(stack_args)
    kv_specs = [dummy, dummy, flipped_spec, flipped_spec] if transposed_kv else [row(FOX_WIDTH)] * 2 + [dummy] * 2
    kv_shapes = [dummy_shape, dummy_shape, flipped, flipped] if transposed_kv else [plain, plain, dummy_shape,
                                                                                   dummy_shape]
    return pl.pallas_call(
        functools.partial(_normproj_kernel, transposed_kv=transposed_kv),
        grid=(n // tm,),
        in_specs=[
            row(D_MODEL), _mod_spec(sh, tiles_per_batch), _mod_spec(sc, tiles_per_batch),
            _const_spec((1, D_MODEL)),
            _const_spec(wrw.shape), _const_spec(wq.shape), _const_spec(wkv.shape), _const_spec(wkvt.shape),
            _const_spec(wft.shape), _const_spec(whg.shape),
        ] + stack_specs,
        out_specs=[row(RW_COLS), row(FOX_WIDTH)] + kv_specs + [
            pl.BlockSpec((1, FOX_HEADS, tm), lambda i: (i // tiles_per_batch, 0, i % tiles_per_batch)),
            row(HG_COLS)],
        out_shape=[jax.ShapeDtypeStruct((n, RW_COLS), F32), plain] + kv_shapes + [
            jax.ShapeDtypeStruct((nbat, FOX_HEADS, tm * tiles_per_batch), F32),
            jax.ShapeDtypeStruct((n, HG_COLS), F32)],
        input_output_aliases={10: 4, 11: 5} if transposed_kv else {},
        compiler_params=_params("arbitrary"),
        name="normproj",
    )(x, sh, sc, g, wrw, wq, wkv, wkvt, wft, whg, *stack_args)


def _cumsum_lanes(x, carry):
    length = x.shape[1]
    step = min(512, length)
    r = lax.broadcasted_iota(jnp.int32, (step, step), 0)
    c = lax.broadcasted_iota(jnp.int32, (step, step), 1)
    upper = (r <= c).astype(F32)
    outs = []
    for j in range(length // step):
        seg = x[:, j * step:(j + 1) * step]
        cs = jnp.dot(seg, upper, precision=HIGHEST, preferred_element_type=F32) + carry
        carry = cs[:, step - 1:step]
        outs.append(cs)
    return outs, carry, step


def _gate_kernel(*refs, past):
    if past:
        f_ref, bf_ref, cache_ref, logf_ref, c_ref = refs
    else:
        f_ref, bf_ref, logf_ref, c_ref = refs
    logf = _log_sigmoid(f_ref[0] + bf_ref[...])
    logf_ref[0] = logf
    carry = jnp.zeros((FOX_HEADS, 1), F32)
    if past:
        outs, carry, step = _cumsum_lanes(cache_ref[0], carry)
        for j, cs in enumerate(outs):
            c_ref[0, :, j * step:(j + 1) * step] = cs
    outs, carry, step = _cumsum_lanes(logf, carry)
    for j, cs in enumerate(outs):
        c_ref[0, :, past + j * step:past + (j + 1) * step] = cs


def _gate(f_bht, b_f, cache_logf_bhp):
    nb, _, t = f_bht.shape
    past = 0 if cache_logf_bhp is None else cache_logf_bhp.shape[2]
    blk = lambda w: pl.BlockSpec((1, FOX_HEADS, w), lambda b: (b, 0, 0))
    in_specs = [blk(t), _const_spec((FOX_HEADS, 1))]
    args = [f_bht, b_f.reshape(FOX_HEADS, 1)]
    if past:
        in_specs.append(blk(past))
        args.append(cache_logf_bhp)
    return pl.pallas_call(
        functools.partial(_gate_kernel, past=past),
        grid=(nb,),
        in_specs=in_specs,
        out_specs=[blk(t), blk(past + t)],
        out_shape=[jax.ShapeDtypeStruct((nb, FOX_HEADS, t), F32),
                   jax.ShapeDtypeStruct((nb, FOX_HEADS, past + t), F32)],
        compiler_params=_params("arbitrary"),
        name="fox_gate",
    )(*args)


LOG2E = 1.4426950408889634
BIAS_ROWS = 16
DIAG_SPLIT = 4
SKIP_MARGIN = 137.0


def _fox_kernel(q_ref, kt_ref, vt_ref, c_ref, o_ref, kaug_scr, vaug_scr, kmax_scr, bmax_scr, *, tq, tk):
    qi = pl.program_id(2)
    hd = HEAD_DIM
    nk = kaug_scr.shape[2] // tk
    per_tile = tq // tk
    lane = lax.broadcasted_iota(jnp.int32, (1, 2 * hd), 1)
    bias_lane0 = (hd, 0)
    ones_lane = (hd, 0)

    @pl.when(qi == 0)
    def _():
        kt = kt_ref[...]
        v = vt_ref[...].T
        t = kt.shape[1]
        for hh in range(2):
            c2 = c_ref[0, hh] * LOG2E
            hi = c2.astype(BF16).astype(F32)
            mid = (c2 - hi).astype(BF16).astype(F32)
            lo = c2 - hi - mid
            bias = jnp.concatenate([-hi, -mid, -lo, jnp.zeros((BIAS_ROWS - 3, t), F32)], axis=0).astype(BF16)
            pad = jnp.zeros((hd - BIAS_ROWS, t), BF16)
            kh = kt[hh * hd:(hh + 1) * hd].astype(BF16)
            if hh == 0:
                kaug_scr[hh] = jnp.concatenate([kh, bias, pad], axis=0)
            else:
                kaug_scr[hh] = jnp.concatenate([bias, pad, kh], axis=0)
            own = (lane // hd) == hh
            vaug_scr[hh] = jnp.where(own, v, jnp.where(lane == ones_lane[hh], 1.0, 0.0)).astype(BF16)
            kf = kh.astype(F32)
            knorm = jnp.sqrt(jnp.sum(kf * kf, axis=0, keepdims=True))
            bsum = -(hi + mid + lo.astype(BF16).astype(F32))
            for j in range(nk):
                kmax_scr[hh, j] = jnp.max(knorm[:, j * tk:(j + 1) * tk])
                bmax_scr[hh, j] = jnp.max(bsum[:, j * tk:(j + 1) * tk])

    q = q_ref[...] * (FOX_SCALE * LOG2E)
    qa = []
    for hh in range(2):
        own = (lane // hd) == hh
        is_bias = (lane >= bias_lane0[hh]) & (lane < bias_lane0[hh] + 3)
        qa.append(jnp.where(own, q, jnp.where(is_bias, 1.0, 0.0)).astype(BF16))

    def scores(qh, hh, start, nk):
        return jnp.dot(qh, kaug_scr[hh, :, pl.ds(start, nk)], preferred_element_type=F32)

    def update(s, hh, start, m, acc, row0):
        if row0 is not None:
            rows_i = lax.broadcasted_iota(jnp.int32, s.shape, 0) + row0
            cols_i = lax.broadcasted_iota(jnp.int32, s.shape, 1)
            s = jnp.where(cols_i <= rows_i, s, -jnp.inf)
        width = 2 * hd
        part = s
        if s.shape[1] % width == 0:
            part = s[:, 0:width]
            for c0 in range(width, s.shape[1], width):
                part = jnp.maximum(part, s[:, c0:c0 + width])
        m_new = jnp.maximum(m, jnp.max(part, axis=-1, keepdims=True))
        alpha = jnp.exp2(m - m_new)
        p = jnp.exp2(s - m_new).astype(BF16)
        acc = alpha * acc + jnp.dot(p, vaug_scr[hh, pl.ds(start, s.shape[1]), :], preferred_element_type=F32)
        return m_new, acc

    def chunk(j, carry):
        start = pl.multiple_of(j * tk, tk)
        sc = [scores(qa[hh], hh, start, tk) for hh in range(2)]
        return tuple(update(sc[hh], hh, start, *carry[hh], None) for hh in range(2))

    start = pl.multiple_of(qi * tq, tq)
    hr = tq // DIAG_SPLIT
    blocks = [(hh, r) for hh in range(2) for r in range(DIAG_SPLIT)]
    sc = [scores(qa[hh][r * hr:(r + 1) * hr], hh, start, (r + 1) * hr) for hh, r in blocks]
    first = [update(s, hh, start, jnp.full((hr, 1), -jnp.inf, F32), jnp.zeros((hr, 2 * hd), F32), r * hr)
             for s, (hh, r) in zip(sc, blocks)]
    carry = tuple((jnp.max(jnp.concatenate([jnp.broadcast_to(m, (hr, 2 * hd))
                                            for m, _ in first[hh * DIAG_SPLIT:(hh + 1) * DIAG_SPLIT]], axis=0),
                           axis=-1, keepdims=True),
                   jnp.concatenate([a for _, a in first[hh * DIAG_SPLIT:(hh + 1) * DIAG_SPLIT]], axis=0))
                  for hh in range(2))

    thresholds, q_norms = [], []
    for hh in range(2):
        qf = jnp.where((lane // hd) == hh, qa[hh].astype(F32), 0.0)
        q_norms.append(jnp.sqrt(jnp.max(jnp.sum(qf * qf, axis=-1, keepdims=True))))
        thresholds.append(jnp.min(carry[hh][0]) - SKIP_MARGIN)
    first_j = jnp.int32(0)
    for j in range(nk - per_tile):
        skippable = j < qi * per_tile
        for hh in range(2):
            skippable = skippable & (q_norms[hh] * kmax_scr[hh, j] + bmax_scr[hh, j] < thresholds[hh])
        first_j = jnp.where(skippable & (first_j == j), j + 1, first_j)

    carry = lax.fori_loop(first_j, qi * per_tile, chunk, carry)
    outs = []
    for hh in range(2):
        acc = carry[hh][1]
        outs.append(acc / acc[:, ones_lane[hh]:ones_lane[hh] + 1])
    o_ref[...] = jnp.where(lane < hd, outs[0], outs[1])


def _fox(q, kt_stack, vt_stack, layer, c_bht, nb, t, tq, tk):
    n = q.shape[0]
    nq = t // tq
    hw = 2 * HEAD_DIM
    return pl.pallas_call(
        functools.partial(_fox_kernel, tq=tq, tk=tk),
        grid=(nb, FOX_HEADS // 2, nq),
        in_specs=[
            pl.BlockSpec((tq, hw), lambda b, h, i: (b * nq + i, h)),
            pl.BlockSpec((None, None, hw, t), lambda b, h, i: (layer, b, h, 0)),
            pl.BlockSpec((None, None, hw, t), lambda b, h, i: (layer, b, h, 0)),
            pl.BlockSpec((1, 2, 1, t), lambda b, h, i: (b, h, 0, 0)),
        ],
        out_specs=pl.BlockSpec((tq, hw), lambda b, h, i: (b * nq + i, h)),
        out_shape=jax.ShapeDtypeStruct((n, FOX_WIDTH), F32),
        scratch_shapes=[pltpu.VMEM((2, hw, t), BF16), pltpu.VMEM((2, t, hw), BF16),
                        pltpu.SMEM((2, t // tk), F32), pltpu.SMEM((2, t // tk), F32)],
        compiler_params=_params("arbitrary", "arbitrary", "arbitrary"),
        name="fox_attention",
    )(q, kt_stack, vt_stack, c_bht.reshape(nb, FOX_HEADS, 1, t))


def _fox_cached_kernel(q_ref, kn_ref, vn_ref, kc_ref, vc_ref, c_ref, o_ref, *, past, t):
    row = lax.broadcasted_iota(jnp.int32, (t, t), 0)
    col = lax.broadcasted_iota(jnp.int32, (t, t), 1)
    q = q_ref[...] * FOX_SCALE
    kn_all = kn_ref[...]
    vn_all = vn_ref[...]
    outs = []
    for h in range(FOX_HEADS):
        lanes = slice(h * HEAD_DIM, (h + 1) * HEAD_DIM)
        qh = q[:, lanes].astype(BF16)
        kc = kc_ref[h * HEAD_DIM:(h + 1) * HEAD_DIM, :].astype(BF16)
        vc = vc_ref[h * HEAD_DIM:(h + 1) * HEAD_DIM, :].astype(BF16)
        kn = kn_all[:, lanes].astype(BF16)
        vn = vn_all[:, lanes].astype(BF16)
        s1 = jnp.dot(qh, kc, preferred_element_type=F32)
        s1 = s1 - c_ref[0, h, :, 0:past]
        s2 = lax.dot_general(qh, kn, NT_DIMS, preferred_element_type=F32)
        s2 = s2 - c_ref[0, h, :, past:past + t]
        s2 = jnp.where(col <= row, s2, -jnp.inf)
        m = jnp.maximum(jnp.max(s1, axis=-1, keepdims=True), jnp.max(s2, axis=-1, keepdims=True))
        p1 = jnp.exp(s1 - m)
        p2 = jnp.exp(s2 - m)
        l = jnp.sum(p1, axis=-1, keepdims=True) + jnp.sum(p2, axis=-1, keepdims=True)
        acc = (lax.dot_general(p1.astype(BF16), vc, NT_DIMS, preferred_element_type=F32)
               + jnp.dot(p2.astype(BF16), vn, preferred_element_type=F32))
        outs.append(acc / l)
    o_ref[...] = jnp.concatenate(outs, axis=1)


def _fox_cached(q, k, v, cache_k, cache_v, layer, c_bhl, nb, t):
    n = q.shape[0]
    past = cache_k.shape[2]
    flip = lambda c: jnp.transpose(c, (0, 1, 3, 4, 2)).reshape(DEPTH, nb, FOX_WIDTH, past)
    cache_k, cache_v = flip(cache_k), flip(cache_v)
    new = pl.BlockSpec((t, FOX_WIDTH), lambda b: (b, 0))
    old = pl.BlockSpec((None, None, FOX_WIDTH, past), lambda b: (layer, b, 0, 0))
    return pl.pallas_call(
        functools.partial(_fox_cached_kernel, past=past, t=t),
        grid=(nb,),
        in_specs=[new, new, new, old, old,
                  pl.BlockSpec((1, FOX_HEADS, 1, past + t), lambda b: (b, 0, 0, 0))],
        out_specs=new,
        out_shape=jax.ShapeDtypeStruct((n, FOX_WIDTH), F32),
        compiler_params=_params("arbitrary"),
        name="fox_attention_cached",
    )(q, k, v, cache_k, cache_v, c_bhl.reshape(nb, FOX_HEADS, 1, past + t))


def _levels(chunk):
    out, m = [], chunk // 2
    while m >= 1:
        out.append(m)
        m //= 2
    return out


FINE_LEVEL = 4


def _decay_sum_matrix(chunk):
    r = np.arange(chunk)[:, None]
    j = np.arange(chunk)[None, :]
    incl = (j <= r).astype(np.float32)
    blocks = [incl]
    for m in _levels(chunk):
        if m <= FINE_LEVEL:
            mid = (r // (2 * m)) * (2 * m) + m - 1
            blocks.append(incl - (j <= mid).astype(np.float32))
    return np.concatenate(blocks, axis=0)


def _level_ids(chunk):
    levels = _levels(chunk)
    t = np.arange(chunk)
    x = t[:, None] ^ t[None, :]
    m = np.where(x > 0, 2 ** np.floor(np.log2(np.maximum(x, 1))).astype(np.int64), 0)
    lvl = np.full((chunk, chunk), len(levels) + 1, np.int32)
    for li, lm in enumerate(levels):
        lvl[(t[:, None] > t[None, :]) & (m == lm)] = li
    lvl[x == 0] = len(levels)
    return lvl


def _level_ids_wide(chunk, heads):
    return np.tile(_level_ids(chunk), (1, heads))


def _two_pass(mat_bf16, x):
    hi, lo = _split(x)
    d = lambda u: jnp.dot(mat_bf16, u, preferred_element_type=F32)
    return d(hi) + d(lo)


def _hgrn_kernel(p_ref, lb_ref, ng_ref, s0_ref, dmat_ref, lvl_ref, ones_ref, o_ref, st_ref, st_scr, *, chunk,
                 nchunks, group):
    ti = pl.program_id(1)

    @pl.when(ti == 0)
    def _():
        st_scr[...] = s0_ref[0]

    w = HG_WIDTH
    levels = _levels(chunk)
    lb = lb_ref[...]
    log_lb = jnp.log(lb)
    log_1m_lb = jnp.log1p(-lb)
    row = lax.broadcasted_iota(jnp.int32, (chunk, 1), 0)
    lane_head = lax.broadcasted_iota(jnp.int32, (1, w), 1) // HEAD_DIM
    row_head = lax.broadcasted_iota(jnp.int32, (w, 1), 0) // HEAD_DIM

    head_lanes = [(lane_head == h).astype(BF16) for h in range(HG_HEADS)]

    def stack(x):
        xb = x.astype(BF16)
        return jnp.concatenate([xb * m for m in head_lanes], axis=0)

    def each(fn, *lists):
        return [fn(*xs) for xs in zip(*lists)]

    def chunk_group(gi, _):
        base = gi * (group * chunk)
        rows = [pl.ds(pl.multiple_of(base + i * chunk, chunk), chunk) for i in range(group)]
        q = [p_ref[r, 0:w] for r in rows]
        fx = [p_ref[r, w:2 * w] for r in rows]
        val = [p_ref[r, 2 * w:3 * w] for r in rows]
        e = each(lambda x: jnp.exp(-jnp.abs(x)), fx)
        bv = each(lambda x, ei: log_1m_lb + (jnp.minimum(x, 0.0) - jnp.log1p(ei)), fx, e)
        logf = each(lambda x: jnp.maximum(log_lb, x) + jnp.log1p(jnp.exp(-jnp.abs(log_lb - x))), bv)
        kd = each(lambda x, ei: (1.0 - lb) * (jnp.where(x >= 0.0, ei, 1.0) / (1.0 + ei)), fx, e)
        sums = each(lambda x: _two_pass(dmat_ref[...], x), logf)
        b = [x[0:chunk] for x in sums]

        lvl = lvl_ref[...]
        attn = each(lambda qi, ki: jnp.where(lvl == len(levels), _mm(qi, stack(ki), NT_DIMS), 0.0), q, kd)
        fine = 0
        for li, m in enumerate(levels):
            if m > FINE_LEVEL:
                def rel(bi):
                    mids = [jnp.broadcast_to(bi[s0 + m - 1:s0 + m, :], (2 * m, w)) for s0 in range(0, chunk, 2 * m)]
                    return bi - jnp.concatenate(mids, axis=0)
                x = each(rel, b)
            else:
                fine += 1
                x = [sm[fine * chunk:(fine + 1) * chunk] for sm in sums]
            wgt = each(lambda xi: jnp.exp(-jnp.abs(xi)), x)
            second = ((row // m) % 2) == 1
            ql = each(lambda qi, wi: jnp.where(second, qi * wi, 0.0), q, wgt)
            kl = each(lambda ki, wi: jnp.where(second, 0.0, ki * wi), kd, wgt)
            attn = each(lambda at, qi, ki: jnp.where(lvl == li, _mm(qi, stack(ki), NT_DIMS), at), attn, ql, kl)
        o_intra = each(lambda at, vi: _mm(at, stack(vi)), attn, val)
        q_dec = each(lambda qi, bi: qi * jnp.exp(bi), q, b)
        b_last = [bi[chunk - 1:chunk, :] for bi in b]
        upd = each(lambda vi, ki, bl, bi: _mm(vi, ki * jnp.exp(bl - bi), TN_DIMS), val, kd, b_last, b)

        st = st_scr[...]
        for i in range(group):
            o = _mm(q_dec[i], st, NT_DIMS) + o_intra[i]
            st = jnp.exp(b_last[i]) * st + jnp.where(row_head == lane_head, upd[i], 0.0)
            ms = _head_sum(o * o, ones_ref[...]) * (1.0 / HEAD_DIM)
            g = p_ref[rows[i], 3 * w:4 * w]
            o_ref[rows[i], :] = o * lax.rsqrt(ms + EPS) * ng_ref[...] * (g * _sigmoid(g))
        st_scr[...] = st
        return 0

    lax.fori_loop(0, nchunks // group, chunk_group, 0)

    @pl.when(ti == pl.num_programs(1) - 1)
    def _():
        st_ref[0] = st_scr[...]


def _hgrn(p_hg, lb, norm_g, st0, nb, t, chunk, tile):
    n = p_hg.shape[0]
    nt = t // tile
    w = HG_WIDTH
    dmat = jnp.asarray(_decay_sum_matrix(chunk), BF16)
    lvl = jnp.asarray(_level_ids_wide(chunk, HG_HEADS))
    ones = jnp.asarray(_block_ones_np(w), BF16)
    return pl.pallas_call(
        functools.partial(_hgrn_kernel, chunk=chunk, nchunks=tile // chunk, group=min(4, tile // chunk)),
        grid=(nb, nt),
        in_specs=[
            pl.BlockSpec((tile, HG_COLS), lambda b, i: (b * nt + i, 0)),
            _const_spec((1, w)), _const_spec((1, w)),
            pl.BlockSpec((1, w, w), lambda b, i: (b, 0, 0)),
            _const_spec(dmat.shape), _const_spec(lvl.shape), _const_spec(ones.shape),
        ],
        out_specs=[pl.BlockSpec((tile, w), lambda b, i: (b * nt + i, 0)),
                   pl.BlockSpec((1, w, w), lambda b, i: (b, 0, 0))],
        out_shape=[jax.ShapeDtypeStruct((n, w), F32), jax.ShapeDtypeStruct((nb, w, w), F32)],
        scratch_shapes=[pltpu.VMEM((w, w), F32)],
        compiler_params=_params("arbitrary", "arbitrary"),
        name="hgrn2",
    )(p_hg, lb, norm_g, st0, dmat, lvl, ones)


def _rwkv_kernel(p_ref, prev_ref, s0_ref, mu_ref, vec_ref, w2_ref, a2_ref, g2_ref, ltri_ref, ones_ref,
                 lvlw_ref, y_ref, st_ref, st_scr, prev_scr, *, chunk, nchunks, group):
    ti = pl.program_id(1)

    @pl.when(ti == 0)
    def _():
        st_scr[...] = s0_ref[0]
        prev_scr[...] = prev_ref[0]

    w = RW_WIDTH
    hc = RW_HEADS * chunk
    levels = _levels(chunk)
    nl = len(levels)
    w0 = vec_ref[0:1, :]
    a0 = vec_ref[1:2, :]
    k_k = vec_ref[2:3, :]
    k_a = vec_ref[3:4, :]
    r_k = vec_ref[4:5, :]
    ln_w = vec_ref[5:6, :]
    ln_b = vec_ref[6:7, :]
    row = lax.broadcasted_iota(jnp.int32, (chunk, 1), 0)
    lane_head = lax.broadcasted_iota(jnp.int32, (1, w), 1) // HEAD_DIM
    row_id = lax.broadcasted_iota(jnp.int32, (w, 1), 0)
    lane_id = lax.broadcasted_iota(jnp.int32, (1, w), 1)
    row_head = row_id // HEAD_DIM
    wide_head = lax.broadcasted_iota(jnp.int32, (1, hc), 1) // chunk

    head_lanes = [(lane_head == h).astype(BF16) for h in range(RW_HEADS)]
    head_lanes2 = [jnp.concatenate([m, m], axis=1) for m in head_lanes]
    wide_lanes = [(wide_head == h).astype(BF16) for h in range(RW_HEADS)]

    def stack(x):
        xb = x.astype(BF16)
        masks = head_lanes if x.shape[1] == w else head_lanes2
        return jnp.concatenate([xb * m for m in masks], axis=0)

    def blockdiag(xw):
        xb = xw.astype(BF16)
        return jnp.concatenate([xb * m for m in wide_lanes], axis=0)

    def each(fn, *lists):
        return [fn(*xs) for xs in zip(*lists)]

    def chunk_group(gi, _):
        base = gi * (group * chunk)
        rows = [pl.ds(pl.multiple_of(base + i * chunk, chunk), chunk) for i in range(group)]
        ones_bd = ones_ref[...]
        mu = mu_ref[...]
        ps = [p_ref[r, :] for r in rows]
        prevs = [prev_scr[...]] + [p[chunk - 1:chunk, :] for p in ps[:-1]]
        prev_scr[...] = ps[-1][chunk - 1:chunk, :]
        xs = each(lambda p, pv: p + mu * (jnp.where(row == 0, pv, pltpu.roll(p, 1, 0)) - p), ps, prevs)
        r = [x[:, 0:w] for x in xs]
        k = [x[:, w:2 * w] for x in xs]
        v = [x[:, 2 * w:3 * w] for x in xs]
        tail = [x[:, 3 * w:] for x in xs]
        w_lin = each(lambda t: w0 + _mm3(jnp.tanh(t), w2_ref[...]), tail)
        a = each(lambda t: _sigmoid(a0 + _mm(t, a2_ref[...])), tail)
        g = each(lambda t: _mm(_sigmoid(t), g2_ref[...]), tail)
        logw = each(lambda wl: -jnp.exp(-(jnp.maximum(-wl, 0.0) + jnp.log1p(jnp.exp(-jnp.abs(wl)))) - 0.5), w_lin)
        kk = each(lambda x: x * k_k, k)
        kk = each(lambda x: x / jnp.maximum(jnp.sqrt(_head_sum(x * x, ones_bd)), 1e-12), kk)
        kmod = each(lambda x, ai: x * (1.0 + (ai - 1.0) * k_a), k, a)
        beta = each(lambda x, ai: x * ai, kk, a)

        lg = each(lambda x: _two_pass(ltri_ref[...], x), logw)
        lg_end = [x[chunk - 1:chunk, :] for x in lg]
        a_n = each(lambda x, l, lw: x * jnp.exp(l - lw), kk, lg, logw)
        r_n = each(lambda x, l: x * jnp.exp(l), r, lg)
        inv_g = each(lambda l: jnp.exp(-l), lg)
        to_end = each(lambda le, l: jnp.exp(le - l), lg_end, lg)
        v_s = each(stack, v)

        gram = each(lambda an, rn, be, km, ig: _mm(jnp.concatenate([an, rn], axis=0),
                                                   jnp.concatenate([stack(be * ig), stack(km * ig)], axis=0),
                                                   NT_DIMS), a_n, r_n, beta, kmod, inv_g)
        lvlw = lvlw_ref[...]
        strict = lvlw < nl
        incl = lvlw <= nl
        n_w = [jnp.where(strict, x[0:chunk, 0:hc], 0.0).astype(BF16) for x in gram]
        m_k = [jnp.where(strict, x[0:chunk, hc:2 * hc], 0.0) for x in gram]
        l_b = [jnp.where(incl, x[chunk:2 * chunk, 0:hc], 0.0) for x in gram]
        l_k = [jnp.where(incl, x[chunk:2 * chunk, hc:2 * hc], 0.0) for x in gram]

        t_w = each(lambda n: (jnp.where(lvlw == nl, 1.0, 0.0)
                              - jnp.where(lvlw == nl - 1, n, jnp.zeros_like(n)).astype(F32)), n_w)
        blk = 2
        while blk < chunk:
            li = levels.index(blk)
            level = (lvlw == li).astype(BF16)
            x_w = each(lambda t, n: _mm(t, blockdiag(n * level)), t_w, n_w)
            t_w = each(lambda t, x: t - _mm(x, blockdiag(t)), t_w, x_w)
            blk *= 2

        rhs = each(lambda an, mk, vs: jnp.concatenate([an, _mm(mk, vs)], axis=1), a_n, m_k, v_s)
        x0 = each(lambda t, rh: _mm(t, stack(rh)), t_w, rhs)

        def residual(n, rh, x):
            x_hi, x_lo = _split(x)
            return rh - x - (jnp.dot(n, stack(x_hi), preferred_element_type=F32)
                             + jnp.dot(n, stack(x_lo), preferred_element_type=F32))

        res = each(residual, n_w, rhs, x0)
        sol = each(lambda x, t, rs: x + _mm(t, stack(rs)), x0, t_w, res)
        a_hat = [x[:, 0:w] for x in sol]
        e0 = [x[:, w:2 * w] for x in sol]
        lb_x = each(lambda lb, x: _mm(lb, stack(x)), l_b, sol)
        r_hat = each(lambda rn, lx: rn - lx[:, 0:w], r_n, lb_x)
        y0 = each(lambda lk, vs, lx: _mm(lk, vs) - lx[:, w:2 * w], l_k, v_s, lb_x)
        same_head = row_head == lane_head
        bh = each(lambda be, te: be * te, beta, to_end)
        kh = each(lambda km, te: km * te, kmod, to_end)
        trans = each(lambda ah, b_, le: (jnp.where(row_id == lane_id, jnp.exp(le), 0.0)
                                         - jnp.where(same_head, _mm(ah, b_, TN_DIMS), 0.0)), a_hat, bh, lg_end)
        q0 = each(lambda vi, ei, k_, b_: jnp.where(same_head, _mm(jnp.concatenate([vi, ei], axis=0),
                                                                   jnp.concatenate([k_, -b_], axis=0), TN_DIMS), 0.0),
                  v, e0, kh, bh)

        st = st_scr[...]
        for i in range(group):
            y = _mm(r_hat[i], st, NT_DIMS) + y0[i]
            st = _mm(st, trans[i]) + q0[i]

            inv_n = 1.0 / HEAD_DIM
            mean = _head_sum(y, ones_bd) * inv_n
            yc = y - mean
            var = _head_sum(yc * yc, ones_bd) * inv_n
            yn = yc * lax.rsqrt(var + RW_GN_EPS) * ln_w + ln_b
            bonus = _head_sum(r[i] * kmod[i] * r_k, ones_bd) * v[i]
            y_ref[rows[i], :] = (yn + bonus) * g[i]
        st_scr[...] = st
        return 0

    lax.fori_loop(0, nchunks // group, chunk_group, 0)

    @pl.when(ti == pl.num_programs(1) - 1)
    def _():
        st_ref[0] = st_scr[...]


def _rwkv(p_rw, prev, st0, mu, vecs, w2p, a2p, g2p, nb, t, chunk, tile):
    n = p_rw.shape[0]
    nt = t // tile
    w = RW_WIDTH
    consts = (jnp.asarray(np.tril(np.ones((chunk, chunk), np.float32)), BF16),
              jnp.asarray(_block_ones_np(w), BF16),
              jnp.asarray(_level_ids_wide(chunk, RW_HEADS)))
    return pl.pallas_call(
        functools.partial(_rwkv_kernel, chunk=chunk, nchunks=tile // chunk, group=min(8, tile // chunk)),
        grid=(nb, nt),
        in_specs=[
            pl.BlockSpec((tile, RW_COLS), lambda b, i: (b * nt + i, 0)),
            pl.BlockSpec((1, 1, RW_COLS), lambda b, i: (b, 0, 0)),
            pl.BlockSpec((1, w, w), lambda b, i: (b, 0, 0)),
            _const_spec((1, RW_COLS)), _const_spec((8, w)),
            _const_spec((RW_LORA, w)), _const_spec((RW_LORA, w)), _const_spec((RW_LORA, w)),
        ] + [_const_spec(c.shape) for c in consts],
        out_specs=[pl.BlockSpec((tile, w), lambda b, i: (b * nt + i, 0)),
                   pl.BlockSpec((1, w, w), lambda b, i: (b, 0, 0))],
        out_shape=[jax.ShapeDtypeStruct((n, w), F32), jax.ShapeDtypeStruct((nb, w, w), F32)],
        scratch_shapes=[pltpu.VMEM((w, w), F32), pltpu.VMEM((1, RW_COLS), F32)],
        compiler_params=_params("arbitrary", "arbitrary"),
        name="rwkv7",
    )(p_rw, prev, st0, mu, vecs, w2p, a2p, g2p, *consts)


def _outffn_kernel(yrw_ref, yfox_ref, yhg_ref, x_ref, ga1_ref, sh2_ref, sc2_ref, ga2_ref, g2_ref,
                   worw_ref, wofox_ref, wohg_ref, wg_ref, wu_ref, wo_ref, fg_ref, o_ref, *, final_norm, ff_step):
    mix = (jnp.dot(yrw_ref[...].astype(BF16), worw_ref[...], preferred_element_type=F32)
           + jnp.dot(yfox_ref[...].astype(BF16), wofox_ref[...], preferred_element_type=F32)
           + jnp.dot(yhg_ref[...].astype(BF16), wohg_ref[...], preferred_element_type=F32))
    x1 = x_ref[...] + ga1_ref[0] * mix
    ms = jnp.mean(x1 * x1, axis=-1, keepdims=True)
    h2 = x1 * lax.rsqrt(ms + EPS) * g2_ref[...]
    h2 = (h2 * (1.0 + sc2_ref[0]) + sh2_ref[0]).astype(BF16)
    ffn = jnp.zeros_like(x1)
    for j in range(D_FF // ff_step):
        cols = slice(j * ff_step, (j + 1) * ff_step)
        gate = jnp.dot(h2, wg_ref[:, cols], preferred_element_type=F32)
        up = jnp.dot(h2, wu_ref[:, cols], preferred_element_type=F32)
        act = (gate * _sigmoid(gate) * up).astype(BF16)
        ffn = ffn + jnp.dot(act, wo_ref[cols, :], preferred_element_type=F32)
    x2 = x1 + ga2_ref[0] * ffn
    if final_norm:
        ms2 = jnp.mean(x2 * x2, axis=-1, keepdims=True)
        x2 = x2 * lax.rsqrt(ms2 + EPS) * fg_ref[...]
    o_ref[...] = x2


def _outffn(y_rw, y_fox, y_hg, x, ga1, sh2, sc2, ga2, g2, wts, final_g, final_norm, tm, tiles_per_batch):
    n = x.shape[0]
    row = lambda w: pl.BlockSpec((tm, w), lambda i: (i, 0))
    mods = [_mod_spec(m, tiles_per_batch) for m in (ga1, sh2, sc2, ga2)]
    return pl.pallas_call(
        functools.partial(_outffn_kernel, final_norm=final_norm, ff_step=D_FF // 2),
        grid=(n // tm,),
        in_specs=[row(RW_WIDTH), row(FOX_WIDTH), row(HG_WIDTH), row(D_MODEL)] + mods
        + [_const_spec((1, D_MODEL))] + [_const_spec(w.shape) for w in wts] + [_const_spec((1, D_MODEL))],
        out_specs=row(D_MODEL),
        out_shape=jax.ShapeDtypeStruct((n, D_MODEL), F32),
        compiler_params=_params("arbitrary"),
        name="outffn",
    )(y_rw, y_fox, y_hg, x, ga1, sh2, sc2, ga2, g2, *wts, final_g)


def _block_diag_t(s):
    nb, h, dk, dv = s.shape
    st = jnp.swapaxes(s, 2, 3)
    eye = jnp.eye(h, dtype=s.dtype)
    return jnp.einsum("bhvk,hg->bhvgk", st, eye).reshape(nb, h * dv, h * dk)


def _block_diag(s):
    nb, h, dv, dk = s.shape
    eye = jnp.eye(h, dtype=s.dtype)
    return jnp.einsum("bhvk,hg->bhvgk", s, eye).reshape(nb, h * dv, h * dk)


def _diag_blocks(sbd, h):
    nb, rows, cols = sbd.shape
    x = sbd.reshape(nb, h, rows // h, h, cols // h)
    return jnp.stack([x[:, i, :, i, :] for i in range(h)], axis=1)


def _layer_weights(l, w_in, rw_w2, rw_a2, rw_g2, w_out, w_ffn_in, w_ffn_out):
    wi = w_in[l]
    c0, c1 = RW_COLS, RW_COLS + 3 * FOX_WIDTH
    w_kv = wi[:, c0 + FOX_WIDTH:c1].astype(BF16)
    proj = (wi[:, :c0].astype(BF16), wi[:, c0:c0 + FOX_WIDTH].astype(BF16), w_kv, w_kv.T,
            wi[:, c1:c1 + FOX_HEADS].T.astype(BF16), wi[:, c1 + FOX_HEADS:].astype(BF16))
    zeros = lambda r: jnp.zeros((r, RW_WIDTH), F32)
    w2p = jnp.concatenate([rw_w2[l], zeros(RW_LORA - RW_W_LORA)], axis=0)
    a2p = jnp.concatenate([zeros(RW_W_LORA), rw_a2[l], zeros(RW_G_LORA)], axis=0)
    g2p = jnp.concatenate([zeros(RW_W_LORA + RW_A_LORA), rw_g2[l]], axis=0)
    wo = w_out[l].astype(BF16)
    ffn = (wo[:RW_WIDTH], wo[RW_WIDTH:RW_WIDTH + FOX_WIDTH], wo[RW_WIDTH + FOX_WIDTH:],
           w_ffn_in[l][:, :D_FF].astype(BF16), w_ffn_in[l][:, D_FF:].astype(BF16), w_ffn_out[l].astype(BF16))
    return proj, (w2p, a2p, g2p), ffn


def _group_layer(x, mods, lw, lp, layer, kv_stack, prev, rw_st0, hg_st0, cache, nb, t, cfg, final_g, final_norm):
    proj_w, lora_w, ffn_w = lw
    sh1, sc1, ga1, sh2, sc2, ga2 = mods
    tm, tpb = cfg["tm"], cfg["tiles_per_batch"]
    p_rw, q, k, v, kt, vt, f, p_hg = _normproj(x, sh1, sc1, lp["norm1_g"], proj_w, kv_stack, layer, tm, tpb)
    if f.shape[0] == nb:
        f_bht = f
    else:
        f_bht = jnp.swapaxes(f.reshape(FOX_HEADS, nb, t), 0, 1)
    if cache is None:
        logf_bht, c_bhl = _gate(f_bht, lp["fox_b_f"], None)
        y_fox = _fox(q, kt, vt, layer, c_bhl, nb, t, cfg["tq"], cfg["tk"])
        kv_stack = (kt, vt)
    else:
        cache_k, cache_v, cache_logf = cache
        logf_bht, c_bhl = _gate(f_bht, lp["fox_b_f"], jnp.swapaxes(cache_logf, 1, 2))
        y_fox = _fox_cached(q, k, v, cache_k, cache_v, layer, c_bhl, nb, t)
    y_hg, hg_st = _hgrn(p_hg, lp["hg_lb"], lp["hg_norm_g"], hg_st0, nb, t, cfg["hg_chunk"], cfg["hg_tile"])
    y_rw, rw_st = _rwkv(p_rw, prev, rw_st0, lp["rw_mu"], lp["rw_vecs"], *lora_w, nb, t,
                        cfg["rw_chunk"], cfg["rw_tile"])
    x = _outffn(y_rw, y_fox, y_hg, x, ga1, sh2, sc2, ga2, lp["norm2_g"], ffn_w, final_g, final_norm, tm, tpb)
    new_kv = (None, None) if cache is None else (k.reshape(nb, t, FOX_HEADS, HEAD_DIM),
                                                  v.reshape(nb, t, FOX_HEADS, HEAD_DIM))
    states = new_kv + (
        jnp.swapaxes(logf_bht, 1, 2),
        _diag_blocks(rw_st, RW_HEADS),
        p_rw.reshape(nb, t, RW_COLS)[:, t - 1:t, :],
        jnp.swapaxes(_diag_blocks(hg_st, HG_HEADS), 2, 3),
    )
    return x, kv_stack, states


def kernel(x_prompt, x_sample, c_prompt, c_sample, cache_fox_k, cache_fox_v, cache_fox_logf, state_rwkv,
           state_rwkv_shift, state_hgrn, norm1_g, w_ada, b_ada, w_in, rw_mu, rw_w0, rw_w2, rw_a0, rw_a2, rw_g2,
           rw_k_k, rw_k_a, rw_r_k, rw_ln_w, rw_ln_b, fox_b_f, hg_lb_logits, hg_norm_g, w_out, norm2_g,
           w_ffn_in, w_ffn_out, final_norm_g):
    dt = x_prompt.dtype
    bp, tp, _ = x_prompt.shape
    bs, ts, _ = x_sample.shape

    mod = _modulation(jnp.concatenate([c_prompt, c_sample], axis=0).astype(F32), w_ada, b_ada)
    lbs = _hg_lower_bounds(hg_lb_logits.astype(F32))

    cfg_p = dict(tm=512, tiles_per_batch=tp // 512, tq=512, tk=512, hg_chunk=HG_CHUNK, hg_tile=1024,
                 rw_chunk=RW_CHUNK, rw_tile=1024)
    cfg_s = dict(tm=bs * ts, tiles_per_batch=1, hg_chunk=ts, hg_tile=ts, rw_chunk=ts, rw_tile=ts)

    xp = x_prompt.reshape(bp * tp, D_MODEL).astype(F32)
    xs = x_sample.reshape(bs * ts, D_MODEL).astype(F32)
    zero_prev = jnp.zeros((bp, 1, RW_COLS), F32)
    zero_state = jnp.zeros((bp, RW_WIDTH, RW_WIDTH), F32)
    final_g = final_norm_g.reshape(1, D_MODEL)
    row = lambda a: a.reshape(1, -1)

    kv_p = (jnp.zeros((DEPTH, bp, FOX_WIDTH, tp), F32), jnp.zeros((DEPTH, bp, FOX_WIDTH, tp), F32))
    outs_p, outs_s = [], []
    for l in range(DEPTH):
        lw = _layer_weights(l, w_in, rw_w2, rw_a2, rw_g2, w_out, w_ffn_in, w_ffn_out)
        zero_row = jnp.zeros((RW_WIDTH,), F32)
        lp = dict(
            norm1_g=row(norm1_g[l]), norm2_g=row(norm2_g[l]), fox_b_f=fox_b_f[l],
            hg_lb=row(lbs[l]), hg_norm_g=row(hg_norm_g[l]), rw_mu=row(rw_mu[l]),
            rw_vecs=jnp.stack([rw_w0[l], rw_a0[l], rw_k_k[l], rw_k_a[l], rw_r_k[l].reshape(-1),
                               rw_ln_w[l], rw_ln_b[l], zero_row], axis=0),
        )
        chunks_p = [m[:, None, :] for m in jnp.split(mod[l, :bp], 6, axis=-1)]
        chunks_s = [jnp.repeat(m, ts, axis=0)[None] for m in jnp.split(mod[l, bp:], 6, axis=-1)]
        last = l == DEPTH - 1
        xp, kv_p, st_p = _group_layer(xp, chunks_p, lw, lp, l, kv_p, zero_prev, zero_state, zero_state, None,
                                      bp, tp, cfg_p, final_g, last)
        xs, _, st_s = _group_layer(xs, chunks_s, lw, lp, l, None, state_rwkv_shift[l].astype(F32),
                                _block_diag(state_rwkv[l].astype(F32)),
                                _block_diag_t(state_hgrn[l].astype(F32)),
                                (cache_fox_k.astype(F32), cache_fox_v.astype(F32), cache_fox_logf[l].astype(F32)),
                                bs, ts, cfg_s, final_g, last)
        outs_p.append(st_p)
        outs_s.append(st_s)

    stk = lambda outs, j: jnp.stack([o[j] for o in outs], axis=0).astype(dt)
    y_prompt = xp.reshape(bp, tp, D_MODEL).astype(dt)
    y_sample = xs.reshape(bs, ts, D_MODEL).astype(dt)
    unflip = lambda a: jnp.transpose(a.reshape(DEPTH, bp, FOX_HEADS, HEAD_DIM, tp), (0, 1, 4, 2, 3))
    return (y_prompt, y_sample,
            unflip(kv_p[0]).astype(dt), unflip(kv_p[1]).astype(dt), stk(outs_p, 2), stk(outs_p, 3), stk(outs_p, 4), stk(outs_p, 5),
            stk(outs_s, 0), stk(outs_s, 1), stk(outs_s, 2), stk(outs_s, 3), stk(outs_s, 4), stk(outs_s, 5))
```

```python
import functools

import numpy as np
import jax
import jax.numpy as jnp
from jax import lax
from jax.experimental import pallas as pl
from jax.experimental.pallas import tpu as pltpu

F32 = jnp.float32
BF16 = jnp.bfloat16
HIGHEST = lax.Precision.HIGHEST

D_MODEL = 1024
DEPTH = 4
HEAD_DIM = 64
RW_HEADS = 4
RW_WIDTH = RW_HEADS * HEAD_DIM
FOX_HEADS = 8
FOX_WIDTH = FOX_HEADS * HEAD_DIM
HG_HEADS = 4
HG_WIDTH = HG_HEADS * HEAD_DIM
RW_W_LORA = 32
RW_A_LORA = 32
RW_G_LORA = 64
RW_LORA = RW_W_LORA + RW_A_LORA + RW_G_LORA
RW_COLS = 3 * RW_WIDTH + RW_LORA
FOX_COLS = 3 * FOX_WIDTH + FOX_HEADS
HG_COLS = 4 * HG_WIDTH
D_FF = 2816
EPS = 1e-6
RW_GN_EPS = 64e-5
FOX_SCALE = HEAD_DIM ** -0.5
HG_CHUNK = 64
RW_CHUNK = 64

VMEM_LIMIT_BYTES = 56 * 1024 * 1024

NT_DIMS = (((1,), (1,)), ((), ()))
TN_DIMS = (((0,), (0,)), ((), ()))
NN_DIMS = (((1,), (0,)), ((), ()))


def _params(*sem):
    return pltpu.CompilerParams(dimension_semantics=sem, vmem_limit_bytes=VMEM_LIMIT_BYTES)


def _const_spec(shape):
    nd = len(shape)
    return pl.BlockSpec(shape, lambda *_: (0,) * nd, pipeline_mode=pl.Buffered(1))


def _mm(a, b, dims=NN_DIMS):
    return lax.dot_general(a.astype(BF16), b.astype(BF16), dims, preferred_element_type=F32)


def _split(a):
    hi = a.astype(BF16)
    lo = (a - hi.astype(F32)).astype(BF16)
    return hi, lo


def _mm3(a, b, dims=NN_DIMS):
    ah, al = _split(a)
    bh, bl = _split(b)
    d = lambda x, y: lax.dot_general(x, y, dims, preferred_element_type=F32)
    return d(ah, bh) + (d(ah, bl) + d(al, bh))


def _head_sum(x, ones_bd):
    hi, lo = _split(x)
    d = lambda u: jnp.dot(u, ones_bd, preferred_element_type=F32)
    return d(hi) + d(lo)


def _block_ones_np(width):
    r = np.arange(width) // HEAD_DIM
    return (r[:, None] == r[None, :]).astype(np.float32)


def _log_sigmoid(z):
    return jnp.minimum(z, 0.0) - jnp.log1p(jnp.exp(-jnp.abs(z)))


def _sigmoid(z):
    return 1.0 / (1.0 + jnp.exp(-z))


def _mod_kernel(c_ref, w_ref, b_ref, o_ref):
    c = c_ref[...]
    s = c * _sigmoid(c)
    o_ref[0] = jnp.dot(s, w_ref[0], precision=HIGHEST, preferred_element_type=F32) + b_ref[0]


def _modulation(c_all, w_ada, b_ada):
    nb = c_all.shape[0]
    tn = 1536
    n_out = w_ada.shape[2]
    return pl.pallas_call(
        _mod_kernel,
        grid=(DEPTH, n_out // tn),
        in_specs=[
            pl.BlockSpec((nb, D_MODEL), lambda l, j: (0, 0)),
            pl.BlockSpec((1, D_MODEL, tn), lambda l, j: (l, 0, j)),
            pl.BlockSpec((1, 1, tn), lambda l, j: (l, 0, j)),
        ],
        out_specs=pl.BlockSpec((1, nb, tn), lambda l, j: (l, 0, j)),
        out_shape=jax.ShapeDtypeStruct((DEPTH, nb, n_out), F32),
        compiler_params=_params("arbitrary", "arbitrary"),
        name="adaln_mod",
    )(c_all, w_ada, b_ada.reshape(DEPTH, 1, n_out))


def _lbs_kernel(x_ref, o_ref):
    x = x_ref[...]
    m = jnp.max(x, axis=0, keepdims=True)
    e = jnp.exp(x - m)
    p = e / jnp.sum(e, axis=0, keepdims=True)
    acc = jnp.zeros_like(p[0:1])
    rows = []
    for l in range(DEPTH):
        acc = acc + p[l:l + 1]
        rows.append(acc)
    cs = jnp.concatenate(rows, axis=0)
    o_ref[...] = cs - cs[0:1]


def _hg_lower_bounds(logits):
    return pl.pallas_call(
        _lbs_kernel,
        out_shape=jax.ShapeDtypeStruct(logits.shape, F32),
        name="hgrn_lower_bounds",
    )(logits)


def _normproj_kernel(x_ref, sh_ref, sc_ref, g_ref, wrw_ref, wq_ref, wkv_ref, wkvt_ref, wft_ref, whg_ref, *refs,
                     transposed_kv):
    prw_ref, q_ref, k_ref, v_ref, kt_ref, vt_ref, ft_ref, phg_ref = refs[-8:]
    x = x_ref[...]
    ms = jnp.mean(x * x, axis=-1, keepdims=True)
    h = x * lax.rsqrt(ms + EPS) * g_ref[...]
    h = h * (1.0 + sc_ref[0]) + sh_ref[0]
    hb = h.astype(BF16)
    prw_ref[...] = jnp.dot(hb, wrw_ref[...], preferred_element_type=F32)
    q_ref[...] = jnp.dot(hb, wq_ref[...], preferred_element_type=F32)
    if transposed_kv:
        kvt = lax.dot_general(wkvt_ref[...], hb, NT_DIMS, preferred_element_type=F32)
        kt_ref[...] = kvt[:FOX_WIDTH]
        vt_ref[...] = kvt[FOX_WIDTH:]
        k_ref[...] = jnp.zeros(k_ref.shape, F32)
        v_ref[...] = jnp.zeros(v_ref.shape, F32)
    else:
        kv = jnp.dot(hb, wkv_ref[...], preferred_element_type=F32)
        k_ref[...] = kv[:, :FOX_WIDTH]
        v_ref[...] = kv[:, FOX_WIDTH:]
        kt_ref[...] = jnp.zeros(kt_ref.shape, F32)
        vt_ref[...] = jnp.zeros(vt_ref.shape, F32)
    ft_ref[0] = lax.dot_general(wft_ref[...], hb, NT_DIMS, preferred_element_type=F32)
    phg_ref[...] = jnp.dot(hb, whg_ref[...], preferred_element_type=F32)


def _mod_spec(arr, tiles_per_batch):
    _, r, d = arr.shape
    return pl.BlockSpec((1, r, d), lambda i: (i // tiles_per_batch, 0, 0))


def _normproj(x, sh, sc, g, wts, kv_stack, layer, tm, tiles_per_batch):
    n = x.shape[0]
    transposed_kv = kv_stack is not None
    wrw, wq, wkv, wkvt, wft, whg = wts
    row = lambda w: pl.BlockSpec((tm, w), lambda i: (i, 0))
    dummy = pl.BlockSpec((8, 128), lambda i: (0, 0))
    dummy_shape = jax.ShapeDtypeStruct((8, 128), F32)
    nbat = n // (tm * tiles_per_batch)
    plain = jax.ShapeDtypeStruct((n, FOX_WIDTH), F32)
    flipped = jax.ShapeDtypeStruct((DEPTH, nbat, FOX_WIDTH, tm * tiles_per_batch), F32)
    flipped_spec = pl.BlockSpec((None, None, FOX_WIDTH, tm),
                                lambda i: (layer, i // tiles_per_batch, 0, i % tiles_per_batch))
    stack_args = list(kv_stack) if transposed_kv else []
    stack_specs = [pl.BlockSpec(memory_space=pl.ANY)] * len(stack_args)
    kv_specs = [dummy, dummy, flipped_spec, flipped_spec] if transposed_kv else [row(FOX_WIDTH)] * 2 + [dummy] * 2
    kv_shapes = [dummy_shape, dummy_shape, flipped, flipped] if transposed_kv else [plain, plain, dummy_shape,
                                                                                   dummy_shape]
    return pl.pallas_call(
        functools.partial(_normproj_kernel, transposed_kv=transposed_kv),
        grid=(n // tm,),
        in_specs=[
            row(D_MODEL), _mod_spec(sh, tiles_per_batch), _mod_spec(sc, tiles_per_batch),
            _const_spec((1, D_MODEL)),
            _const_spec(wrw.shape), _const_spec(wq.shape), _const_spec(wkv.shape), _const_spec(wkvt.shape),
            _const_spec(wft.shape), _const_spec(whg.shape),
        ] + stack_specs,
        out_specs=[row(RW_COLS), row(FOX_WIDTH)] + kv_specs + [
            pl.BlockSpec((1, FOX_HEADS, tm), lambda i: (i // tiles_per_batch, 0, i % tiles_per_batch)),
            row(HG_COLS)],
        out_shape=[jax.ShapeDtypeStruct((n, RW_COLS), F32), plain] + kv_shapes + [
            jax.ShapeDtypeStruct((nbat, FOX_HEADS, tm * tiles_per_batch), F32),
            jax.ShapeDtypeStruct((n, HG_COLS), F32)],
        input_output_aliases={10: 4, 11: 5} if transposed_kv else {},
        compiler_params=_params("arbitrary"),
        name="normproj",
    )(x, sh, sc, g, wrw, wq, wkv, wkvt, wft, whg, *stack_args)


def _cumsum_lanes(x, carry):
    length = x.shape[1]
    step = min(512, length)
    r = lax.broadcasted_iota(jnp.int32, (step, step), 0)
    c = lax.broadcasted_iota(jnp.int32, (step, step), 1)
    upper = (r <= c).astype(F32)
    outs = []
    for j in range(length // step):
        seg = x[:, j * step:(j + 1) * step]
        cs = jnp.dot(seg, upper, precision=HIGHEST, preferred_element_type=F32) + carry
        carry = cs[:, step - 1:step]
        outs.append(cs)
    return outs, carry, step


def _gate_kernel(*refs, past):
    if past:
        f_ref, bf_ref, cache_ref, logf_ref, c_ref = refs
    else:
        f_ref, bf_ref, logf_ref, c_ref = refs
    logf = _log_sigmoid(f_ref[0] + bf_ref[...])
    logf_ref[0] = logf
    carry = jnp.zeros((FOX_HEADS, 1), F32)
    if past:
        outs, carry, step = _cumsum_lanes(cache_ref[0], carry)
        for j, cs in enumerate(outs):
            c_ref[0, :, j * step:(j + 1) * step] = cs
    outs, carry, step = _cumsum_lanes(logf, carry)
    for j, cs in enumerate(outs):
        c_ref[0, :, past + j * step:past + (j + 1) * step] = cs


def _gate(f_bht, b_f, cache_logf_bhp):
    nb, _, t = f_bht.shape
    past = 0 if cache_logf_bhp is None else cache_logf_bhp.shape[2]
    blk = lambda w: pl.BlockSpec((1, FOX_HEADS, w), lambda b: (b, 0, 0))
    in_specs = [blk(t), _const_spec((FOX_HEADS, 1))]
    args = [f_bht, b_f.reshape(FOX_HEADS, 1)]
    if past:
        in_specs.append(blk(past))
        args.append(cache_logf_bhp)
    return pl.pallas_call(
        functools.partial(_gate_kernel, past=past),
        grid=(nb,),
        in_specs=in_specs,
        out_specs=[blk(t), blk(past + t)],
        out_shape=[jax.ShapeDtypeStruct((nb, FOX_HEADS, t), F32),
                   jax.ShapeDtypeStruct((nb, FOX_HEADS, past + t), F32)],
        compiler_params=_params("arbitrary"),
        name="fox_gate",
    )(*args)


LOG2E = 1.4426950408889634
BIAS_ROWS = 16
DIAG_SPLIT = 4
SKIP_MARGIN = 137.0


def _fox_kernel(q_ref, kt_ref, vt_ref, c_ref, o_ref, kaug_scr, vaug_scr, kmax_scr, bmax_scr, *, tq, tk):
    qi = pl.program_id(2)
    hd = HEAD_DIM
    nk = kaug_scr.shape[2] // tk
    per_tile = tq // tk
    lane = lax.broadcasted_iota(jnp.int32, (1, 2 * hd), 1)
    bias_lane0 = (hd, 0)
    ones_lane = (hd, 0)

    @pl.when(qi == 0)
    def _():
        kt = kt_ref[...]
        v = vt_ref[...].T
        t = kt.shape[1]
        for hh in range(2):
            c2 = c_ref[0, hh] * LOG2E
            hi = c2.astype(BF16).astype(F32)
            mid = (c2 - hi).astype(BF16).astype(F32)
            lo = c2 - hi - mid
            bias = jnp.concatenate([-hi, -mid, -lo, jnp.zeros((BIAS_ROWS - 3, t), F32)], axis=0).astype(BF16)
            pad = jnp.zeros((hd - BIAS_ROWS, t), BF16)
            kh = kt[hh * hd:(hh + 1) * hd].astype(BF16)
            if hh == 0:
                kaug_scr[hh] = jnp.concatenate([kh, bias, pad], axis=0)
            else:
                kaug_scr[hh] = jnp.concatenate([bias, pad, kh], axis=0)
            own = (lane // hd) == hh
            vaug_scr[hh] = jnp.where(own, v, jnp.where(lane == ones_lane[hh], 1.0, 0.0)).astype(BF16)
            kf = kh.astype(F32)
            knorm = jnp.sqrt(jnp.sum(kf * kf, axis=0, keepdims=True))
            bsum = -(hi + mid + lo.astype(BF16).astype(F32))
            for j in range(nk):
                kmax_scr[hh, j] = jnp.max(knorm[:, j * tk:(j + 1) * tk])
                bmax_scr[hh, j] = jnp.max(bsum[:, j * tk:(j + 1) * tk])

    q = q_ref[...] * (FOX_SCALE * LOG2E)
    qa = []
    for hh in range(2):
        own = (lane // hd) == hh
        is_bias = (lane >= bias_lane0[hh]) & (lane < bias_lane0[hh] + 3)
        qa.append(jnp.where(own, q, jnp.where(is_bias, 1.0, 0.0)).astype(BF16))

    def scores(qh, hh, start, nk):
        return jnp.dot(qh, kaug_scr[hh, :, pl.ds(start, nk)], preferred_element_type=F32)

    def update(s, hh, start, m, acc, row0):
        if row0 is not None:
            rows_i = lax.broadcasted_iota(jnp.int32, s.shape, 0) + row0
            cols_i = lax.broadcasted_iota(jnp.int32, s.shape, 1)
            s = jnp.where(cols_i <= rows_i, s, -jnp.inf)
        width = 2 * hd
        part = s
        if s.shape[1] % width == 0:
            part = s[:, 0:width]
            for c0 in range(width, s.shape[1], width):
                part = jnp.maximum(part, s[:, c0:c0 + width])
        m_new = jnp.maximum(m, jnp.max(part, axis=-1, keepdims=True))
        alpha = jnp.exp2(m - m_new)
        p = jnp.exp2(s - m_new).astype(BF16)
        acc = alpha * acc + jnp.dot(p, vaug_scr[hh, pl.ds(start, s.shape[1]), :], preferred_element_type=F32)
        return m_new, acc

    def chunk(j, carry):
        start = pl.multiple_of(j * tk, tk)
        sc = [scores(qa[hh], hh, start, tk) for hh in range(2)]
        return tuple(update(sc[hh], hh, start, *carry[hh], None) for hh in range(2))

    start = pl.multiple_of(qi * tq, tq)
    hr = tq // DIAG_SPLIT
    blocks = [(hh, r) for hh in range(2) for r in range(DIAG_SPLIT)]
    sc = [scores(qa[hh][r * hr:(r + 1) * hr], hh, start, (r + 1) * hr) for hh, r in blocks]
    first = [update(s, hh, start, jnp.full((hr, 1), -jnp.inf, F32), jnp.zeros((hr, 2 * hd), F32), r * hr)
             for s, (hh, r) in zip(sc, blocks)]
    carry = tuple((jnp.max(jnp.concatenate([jnp.broadcast_to(m, (hr, 2 * hd))
                                            for m, _ in first[hh * DIAG_SPLIT:(hh + 1) * DIAG_SPLIT]], axis=0),
                           axis=-1, keepdims=True),
                   jnp.concatenate([a for _, a in first[hh * DIAG_SPLIT:(hh + 1) * DIAG_SPLIT]], axis=0))
                  for hh in range(2))

    thresholds, q_norms = [], []
    for hh in range(2):
        qf = jnp.where((lane // hd) == hh, qa[hh].astype(F32), 0.0)
        q_norms.append(jnp.sqrt(jnp.max(jnp.sum(qf * qf, axis=-1, keepdims=True))))
        thresholds.append(jnp.min(carry[hh][0]) - SKIP_MARGIN)
    first_j = []
    for hh in range(2):
        fj = jnp.int32(0)
        for j in range(nk - per_tile):
            skippable = (j < qi * per_tile) & (q_norms[hh] * kmax_scr[hh, j] + bmax_scr[hh, j] < thresholds[hh])
            fj = jnp.where(skippable & (fj == j), j + 1, fj)
        first_j.append(fj)
    shared_j = jnp.maximum(first_j[0], first_j[1])

    carry = list(carry)
    for hh in range(2):
        def alone(j, c, hh=hh):
            start = pl.multiple_of(j * tk, tk)
            return update(scores(qa[hh], hh, start, tk), hh, start, *c, None)
        carry[hh] = lax.fori_loop(first_j[hh], shared_j, alone, carry[hh])
    carry = lax.fori_loop(shared_j, qi * per_tile, chunk, tuple(carry))
    outs = []
    for hh in range(2):
        acc = carry[hh][1]
        outs.append(acc / acc[:, ones_lane[hh]:ones_lane[hh] + 1])
    o_ref[...] = jnp.where(lane < hd, outs[0], outs[1])


def _fox(q, kt_stack, vt_stack, layer, c_bht, nb, t, tq, tk):
    n = q.shape[0]
    nq = t // tq
    hw = 2 * HEAD_DIM
    return pl.pallas_call(
        functools.partial(_fox_kernel, tq=tq, tk=tk),
        grid=(nb, FOX_HEADS // 2, nq),
        in_specs=[
            pl.BlockSpec((tq, hw), lambda b, h, i: (b * nq + i, h)),
            pl.BlockSpec((None, None, hw, t), lambda b, h, i: (layer, b, h, 0)),
            pl.BlockSpec((None, None, hw, t), lambda b, h, i: (layer, b, h, 0)),
            pl.BlockSpec((1, 2, 1, t), lambda b, h, i: (b, h, 0, 0)),
        ],
        out_specs=pl.BlockSpec((tq, hw), lambda b, h, i: (b * nq + i, h)),
        out_shape=jax.ShapeDtypeStruct((n, FOX_WIDTH), F32),
        scratch_shapes=[pltpu.VMEM((2, hw, t), BF16), pltpu.VMEM((2, t, hw), BF16),
                        pltpu.SMEM((2, t // tk), F32), pltpu.SMEM((2, t // tk), F32)],
        compiler_params=_params("arbitrary", "arbitrary", "arbitrary"),
        name="fox_attention",
    )(q, kt_stack, vt_stack, c_bht.reshape(nb, FOX_HEADS, 1, t))


def _fox_cached_kernel(q_ref, kn_ref, vn_ref, kc_ref, vc_ref, c_ref, o_ref, *, past, t):
    row = lax.broadcasted_iota(jnp.int32, (t, t), 0)
    col = lax.broadcasted_iota(jnp.int32, (t, t), 1)
    q = q_ref[...] * FOX_SCALE
    kn_all = kn_ref[...]
    vn_all = vn_ref[...]
    outs = []
    for h in range(FOX_HEADS):
        lanes = slice(h * HEAD_DIM, (h + 1) * HEAD_DIM)
        qh = q[:, lanes].astype(BF16)
        kc = kc_ref[h * HEAD_DIM:(h + 1) * HEAD_DIM, :].astype(BF16)
        vc = vc_ref[h * HEAD_DIM:(h + 1) * HEAD_DIM, :].astype(BF16)
        kn = kn_all[:, lanes].astype(BF16)
        vn = vn_all[:, lanes].astype(BF16)
        s1 = jnp.dot(qh, kc, preferred_element_type=F32)
        s1 = s1 - c_ref[0, h, :, 0:past]
        s2 = lax.dot_general(qh, kn, NT_DIMS, preferred_element_type=F32)
        s2 = s2 - c_ref[0, h, :, past:past + t]
        s2 = jnp.where(col <= row, s2, -jnp.inf)
        m = jnp.maximum(jnp.max(s1, axis=-1, keepdims=True), jnp.max(s2, axis=-1, keepdims=True))
        p1 = jnp.exp(s1 - m)
        p2 = jnp.exp(s2 - m)
        l = jnp.sum(p1, axis=-1, keepdims=True) + jnp.sum(p2, axis=-1, keepdims=True)
        acc = (lax.dot_general(p1.astype(BF16), vc, NT_DIMS, preferred_element_type=F32)
               + jnp.dot(p2.astype(BF16), vn, preferred_element_type=F32))
        outs.append(acc / l)
    o_ref[...] = jnp.concatenate(outs, axis=1)


def _fox_cached(q, k, v, cache_k, cache_v, layer, c_bhl, nb, t):
    n = q.shape[0]
    past = cache_k.shape[2]
    flip = lambda c: jnp.transpose(c, (0, 1, 3, 4, 2)).reshape(DEPTH, nb, FOX_WIDTH, past)
    cache_k, cache_v = flip(cache_k), flip(cache_v)
    new = pl.BlockSpec((t, FOX_WIDTH), lambda b: (b, 0))
    old = pl.BlockSpec((None, None, FOX_WIDTH, past), lambda b: (layer, b, 0, 0))
    return pl.pallas_call(
        functools.partial(_fox_cached_kernel, past=past, t=t),
        grid=(nb,),
        in_specs=[new, new, new, old, old,
                  pl.BlockSpec((1, FOX_HEADS, 1, past + t), lambda b: (b, 0, 0, 0))],
        out_specs=new,
        out_shape=jax.ShapeDtypeStruct((n, FOX_WIDTH), F32),
        compiler_params=_params("arbitrary"),
        name="fox_attention_cached",
    )(q, k, v, cache_k, cache_v, c_bhl.reshape(nb, FOX_HEADS, 1, past + t))


def _levels(chunk):
    out, m = [], chunk // 2
    while m >= 1:
        out.append(m)
        m //= 2
    return out


FINE_LEVEL = 4


def _decay_sum_matrix(chunk):
    r = np.arange(chunk)[:, None]
    j = np.arange(chunk)[None, :]
    incl = (j <= r).astype(np.float32)
    blocks = [incl]
    for m in _levels(chunk):
        if m <= FINE_LEVEL:
            mid = (r // (2 * m)) * (2 * m) + m - 1
            blocks.append(incl - (j <= mid).astype(np.float32))
    return np.concatenate(blocks, axis=0)


def _level_ids(chunk):
    levels = _levels(chunk)
    t = np.arange(chunk)
    x = t[:, None] ^ t[None, :]
    m = np.where(x > 0, 2 ** np.floor(np.log2(np.maximum(x, 1))).astype(np.int64), 0)
    lvl = np.full((chunk, chunk), len(levels) + 1, np.int32)
    for li, lm in enumerate(levels):
        lvl[(t[:, None] > t[None, :]) & (m == lm)] = li
    lvl[x == 0] = len(levels)
    return lvl


def _level_ids_wide(chunk, heads):
    return np.tile(_level_ids(chunk), (1, heads))


def _two_pass(mat_bf16, x):
    hi, lo = _split(x)
    d = lambda u: jnp.dot(mat_bf16, u, preferred_element_type=F32)
    return d(hi) + d(lo)


def _hgrn_kernel(p_ref, lb_ref, ng_ref, s0_ref, dmat_ref, lvl_ref, ones_ref, o_ref, st_ref, st_scr, *, chunk,
                 nchunks, group):
    ti = pl.program_id(1)

    @pl.when(ti == 0)
    def _():
        st_scr[...] = s0_ref[0]

    w = HG_WIDTH
    levels = _levels(chunk)
    lb = lb_ref[...]
    log_lb = jnp.log(lb)
    log_1m_lb = jnp.log1p(-lb)
    row = lax.broadcasted_iota(jnp.int32, (chunk, 1), 0)
    lane_head = lax.broadcasted_iota(jnp.int32, (1, w), 1) // HEAD_DIM
    row_head = lax.broadcasted_iota(jnp.int32, (w, 1), 0) // HEAD_DIM

    head_lanes = [(lane_head == h).astype(BF16) for h in range(HG_HEADS)]

    def stack(x):
        xb = x.astype(BF16)
        return jnp.concatenate([xb * m for m in head_lanes], axis=0)

    def each(fn, *lists):
        return [fn(*xs) for xs in zip(*lists)]

    def chunk_group(gi, _):
        base = gi * (group * chunk)
        rows = [pl.ds(pl.multiple_of(base + i * chunk, chunk), chunk) for i in range(group)]
        q = [p_ref[r, 0:w] for r in rows]
        fx = [p_ref[r, w:2 * w] for r in rows]
        val = [p_ref[r, 2 * w:3 * w] for r in rows]
        e = each(lambda x: jnp.exp(-jnp.abs(x)), fx)
        bv = each(lambda x, ei: log_1m_lb + (jnp.minimum(x, 0.0) - jnp.log1p(ei)), fx, e)
        logf = each(lambda x: jnp.maximum(log_lb, x) + jnp.log1p(jnp.exp(-jnp.abs(log_lb - x))), bv)
        kd = each(lambda x, ei: (1.0 - lb) * (jnp.where(x >= 0.0, ei, 1.0) / (1.0 + ei)), fx, e)
        sums = each(lambda x: _two_pass(dmat_ref[...], x), logf)
        b = [x[0:chunk] for x in sums]

        lvl = lvl_ref[...]
        attn = each(lambda qi, ki: jnp.where(lvl == len(levels), _mm(qi, stack(ki), NT_DIMS), 0.0), q, kd)
        fine = 0
        for li, m in enumerate(levels):
            if m > FINE_LEVEL:
                def rel(bi):
                    mids = [jnp.broadcast_to(bi[s0 + m - 1:s0 + m, :], (2 * m, w)) for s0 in range(0, chunk, 2 * m)]
                    return bi - jnp.concatenate(mids, axis=0)
                x = each(rel, b)
            else:
                fine += 1
                x = [sm[fine * chunk:(fine + 1) * chunk] for sm in sums]
            wgt = each(lambda xi: jnp.exp(-jnp.abs(xi)), x)
            second = ((row // m) % 2) == 1
            ql = each(lambda qi, wi: jnp.where(second, qi * wi, 0.0), q, wgt)
            kl = each(lambda ki, wi: jnp.where(second, 0.0, ki * wi), kd, wgt)
            attn = each(lambda at, qi, ki: jnp.where(lvl == li, _mm(qi, stack(ki), NT_DIMS), at), attn, ql, kl)
        o_intra = each(lambda at, vi: _mm(at, stack(vi)), attn, val)
        q_dec = each(lambda qi, bi: qi * jnp.exp(bi), q, b)
        b_last = [bi[chunk - 1:chunk, :] for bi in b]
        upd = each(lambda vi, ki, bl, bi: _mm(vi, ki * jnp.exp(bl - bi), TN_DIMS), val, kd, b_last, b)

        st = st_scr[...]
        for i in range(group):
            o = _mm(q_dec[i], st, NT_DIMS) + o_intra[i]
            st = jnp.exp(b_last[i]) * st + jnp.where(row_head == lane_head, upd[i], 0.0)
            ms = _head_sum(o * o, ones_ref[...]) * (1.0 / HEAD_DIM)
            g = p_ref[rows[i], 3 * w:4 * w]
            o_ref[rows[i], :] = o * lax.rsqrt(ms + EPS) * ng_ref[...] * (g * _sigmoid(g))
        st_scr[...] = st
        return 0

    lax.fori_loop(0, nchunks // group, chunk_group, 0)

    @pl.when(ti == pl.num_programs(1) - 1)
    def _():
        st_ref[0] = st_scr[...]


def _hgrn(p_hg, lb, norm_g, st0, nb, t, chunk, tile):
    n = p_hg.shape[0]
    nt = t // tile
    w = HG_WIDTH
    dmat = jnp.asarray(_decay_sum_matrix(chunk), BF16)
    lvl = jnp.asarray(_level_ids_wide(chunk, HG_HEADS))
    ones = jnp.asarray(_block_ones_np(w), BF16)
    return pl.pallas_call(
        functools.partial(_hgrn_kernel, chunk=chunk, nchunks=tile // chunk, group=min(4, tile // chunk)),
        grid=(nb, nt),
        in_specs=[
            pl.BlockSpec((tile, HG_COLS), lambda b, i: (b * nt + i, 0)),
            _const_spec((1, w)), _const_spec((1, w)),
            pl.BlockSpec((1, w, w), lambda b, i: (b, 0, 0)),
            _const_spec(dmat.shape), _const_spec(lvl.shape), _const_spec(ones.shape),
        ],
        out_specs=[pl.BlockSpec((tile, w), lambda b, i: (b * nt + i, 0)),
                   pl.BlockSpec((1, w, w), lambda b, i: (b, 0, 0))],
        out_shape=[jax.ShapeDtypeStruct((n, w), F32), jax.ShapeDtypeStruct((nb, w, w), F32)],
        scratch_shapes=[pltpu.VMEM((w, w), F32)],
        compiler_params=_params("arbitrary", "arbitrary"),
        name="hgrn2",
    )(p_hg, lb, norm_g, st0, dmat, lvl, ones)


def _rwkv_kernel(p_ref, prev_ref, s0_ref, mu_ref, vec_ref, w2_ref, a2_ref, g2_ref, ltri_ref, ones_ref,
                 lvlw_ref, y_ref, st_ref, st_scr, prev_scr, *, chunk, nchunks, group):
    ti = pl.program_id(1)

    @pl.when(ti == 0)
    def _():
        st_scr[...] = s0_ref[0]
        prev_scr[...] = prev_ref[0]

    w = RW_WIDTH
    hc = RW_HEADS * chunk
    levels = _levels(chunk)
    nl = len(levels)
    w0 = vec_ref[0:1, :]
    a0 = vec_ref[1:2, :]
    k_k = vec_ref[2:3, :]
    k_a = vec_ref[3:4, :]
    r_k = vec_ref[4:5, :]
    ln_w = vec_ref[5:6, :]
    ln_b = vec_ref[6:7, :]
    row = lax.broadcasted_iota(jnp.int32, (chunk, 1), 0)
    lane_head = lax.broadcasted_iota(jnp.int32, (1, w), 1) // HEAD_DIM
    row_id = lax.broadcasted_iota(jnp.int32, (w, 1), 0)
    lane_id = lax.broadcasted_iota(jnp.int32, (1, w), 1)
    row_head = row_id // HEAD_DIM
    wide_head = lax.broadcasted_iota(jnp.int32, (1, hc), 1) // chunk

    head_lanes = [(lane_head == h).astype(BF16) for h in range(RW_HEADS)]
    head_lanes2 = [jnp.concatenate([m, m], axis=1) for m in head_lanes]
    wide_lanes = [(wide_head == h).astype(BF16) for h in range(RW_HEADS)]

    def stack(x):
        xb = x.astype(BF16)
        masks = head_lanes if x.shape[1] == w else head_lanes2
        return jnp.concatenate([xb * m for m in masks], axis=0)

    def blockdiag(xw):
        xb = xw.astype(BF16)
        return jnp.concatenate([xb * m for m in wide_lanes], axis=0)

    def each(fn, *lists):
        return [fn(*xs) for xs in zip(*lists)]

    def chunk_group(gi, _):
        base = gi * (group * chunk)
        rows = [pl.ds(pl.multiple_of(base + i * chunk, chunk), chunk) for i in range(group)]
        ones_bd = ones_ref[...]
        mu = mu_ref[...]
        ps = [p_ref[r, :] for r in rows]
        prevs = [prev_scr[...]] + [p[chunk - 1:chunk, :] for p in ps[:-1]]
        prev_scr[...] = ps[-1][chunk - 1:chunk, :]
        xs = each(lambda p, pv: p + mu * (jnp.where(row == 0, pv, pltpu.roll(p, 1, 0)) - p), ps, prevs)
        r = [x[:, 0:w] for x in xs]
        k = [x[:, w:2 * w] for x in xs]
        v = [x[:, 2 * w:3 * w] for x in xs]
        tail = [x[:, 3 * w:] for x in xs]
        w_lin = each(lambda t: w0 + _mm3(jnp.tanh(t), w2_ref[...]), tail)
        a = each(lambda t: _sigmoid(a0 + _mm(t, a2_ref[...])), tail)
        g = each(lambda t: _mm(_sigmoid(t), g2_ref[...]), tail)
        logw = each(lambda wl: -jnp.exp(-(jnp.maximum(-wl, 0.0) + jnp.log1p(jnp.exp(-jnp.abs(wl)))) - 0.5), w_lin)
        kk = each(lambda x: x * k_k, k)
        kk = each(lambda x: x / jnp.maximum(jnp.sqrt(_head_sum(x * x, ones_bd)), 1e-12), kk)
        kmod = each(lambda x, ai: x * (1.0 + (ai - 1.0) * k_a), k, a)
        beta = each(lambda x, ai: x * ai, kk, a)

        lg = each(lambda x: _two_pass(ltri_ref[...], x), logw)
        lg_end = [x[chunk - 1:chunk, :] for x in lg]
        a_n = each(lambda x, l, lw: x * jnp.exp(l - lw), kk, lg, logw)
        r_n = each(lambda x, l: x * jnp.exp(l), r, lg)
        inv_g = each(lambda l: jnp.exp(-l), lg)
        to_end = each(lambda le, l: jnp.exp(le - l), lg_end, lg)
        v_s = each(stack, v)

        gram = each(lambda an, rn, be, km, ig: _mm(jnp.concatenate([an, rn], axis=0),
                                                   jnp.concatenate([stack(be * ig), stack(km * ig)], axis=0),
                                                   NT_DIMS), a_n, r_n, beta, kmod, inv_g)
        lvlw = lvlw_ref[...]
        strict = lvlw < nl
        incl = lvlw <= nl
        n_w = [jnp.where(strict, x[0:chunk, 0:hc], 0.0).astype(BF16) for x in gram]
        m_k = [jnp.where(strict, x[0:chunk, hc:2 * hc], 0.0) for x in gram]
        l_b = [jnp.where(incl, x[chunk:2 * chunk, 0:hc], 0.0) for x in gram]
        l_k = [jnp.where(incl, x[chunk:2 * chunk, hc:2 * hc], 0.0) for x in gram]

        t_w = each(lambda n: (jnp.where(lvlw == nl, 1.0, 0.0)
                              - jnp.where(lvlw == nl - 1, n, jnp.zeros_like(n)).astype(F32)), n_w)
        blk = 2
        while blk < chunk:
            li = levels.index(blk)
            level = (lvlw == li).astype(BF16)
            x_w = each(lambda t, n: _mm(t, blockdiag(n * level)), t_w, n_w)
            t_w = each(lambda t, x: t - _mm(x, blockdiag(t)), t_w, x_w)
            blk *= 2

        rhs = each(lambda an, mk, vs: jnp.concatenate([an, _mm(mk, vs)], axis=1), a_n, m_k, v_s)
        x0 = each(lambda t, rh: _mm(t, stack(rh)), t_w, rhs)

        def residual(n, rh, x):
            x_hi, x_lo = _split(x)
            return rh - x - (jnp.dot(n, stack(x_hi), preferred_element_type=F32)
                             + jnp.dot(n, stack(x_lo), preferred_element_type=F32))

        res = each(residual, n_w, rhs, x0)
        sol = each(lambda x, t, rs: x + _mm(t, stack(rs)), x0, t_w, res)
        a_hat = [x[:, 0:w] for x in sol]
        e0 = [x[:, w:2 * w] for x in sol]
        lb_x = each(lambda lb, x: _mm(lb, stack(x)), l_b, sol)
        r_hat = each(lambda rn, lx: rn - lx[:, 0:w], r_n, lb_x)
        y0 = each(lambda lk, vs, lx: _mm(lk, vs) - lx[:, w:2 * w], l_k, v_s, lb_x)
        same_head = row_head == lane_head
        bh = each(lambda be, te: be * te, beta, to_end)
        kh = each(lambda km, te: km * te, kmod, to_end)
        trans = each(lambda ah, b_, le: (jnp.where(row_id == lane_id, jnp.exp(le), 0.0)
                                         - jnp.where(same_head, _mm(ah, b_, TN_DIMS), 0.0)), a_hat, bh, lg_end)
        q0 = each(lambda vi, ei, k_, b_: jnp.where(same_head, _mm(jnp.concatenate([vi, ei], axis=0),
                                                                   jnp.concatenate([k_, -b_], axis=0), TN_DIMS), 0.0),
                  v, e0, kh, bh)

        st = st_scr[...]
        for i in range(group):
            y = _mm(r_hat[i], st, NT_DIMS) + y0[i]
            st = _mm(st, trans[i]) + q0[i]

            inv_n = 1.0 / HEAD_DIM
            mean = _head_sum(y, ones_bd) * inv_n
            yc = y - mean
            var = _head_sum(yc * yc, ones_bd) * inv_n
            yn = yc * lax.rsqrt(var + RW_GN_EPS) * ln_w + ln_b
            bonus = _head_sum(r[i] * kmod[i] * r_k, ones_bd) * v[i]
            y_ref[rows[i], :] = (yn + bonus) * g[i]
        st_scr[...] = st
        return 0

    lax.fori_loop(0, nchunks // group, chunk_group, 0)

    @pl.when(ti == pl.num_programs(1) - 1)
    def _():
        st_ref[0] = st_scr[...]


def _rwkv(p_rw, prev, st0, mu, vecs, w2p, a2p, g2p, nb, t, chunk, tile):
    n = p_rw.shape[0]
    nt = t // tile
    w = RW_WIDTH
    consts = (jnp.asarray(np.tril(np.ones((chunk, chunk), np.float32)), BF16),
              jnp.asarray(_block_ones_np(w), BF16),
              jnp.asarray(_level_ids_wide(chunk, RW_HEADS)))
    return pl.pallas_call(
        functools.partial(_rwkv_kernel, chunk=chunk, nchunks=tile // chunk, group=min(8, tile // chunk)),
        grid=(nb, nt),
        in_specs=[
            pl.BlockSpec((tile, RW_COLS), lambda b, i: (b * nt + i, 0)),
            pl.BlockSpec((1, 1, RW_COLS), lambda b, i: (b, 0, 0)),
            pl.BlockSpec((1, w, w), lambda b, i: (b, 0, 0)),
            _const_spec((1, RW_COLS)), _const_spec((8, w)),
            _const_spec((RW_LORA, w)), _const_spec((RW_LORA, w)), _const_spec((RW_LORA, w)),
        ] + [_const_spec(c.shape) for c in consts],
        out_specs=[pl.BlockSpec((tile, w), lambda b, i: (b * nt + i, 0)),
                   pl.BlockSpec((1, w, w), lambda b, i: (b, 0, 0))],
        out_shape=[jax.ShapeDtypeStruct((n, w), F32), jax.ShapeDtypeStruct((nb, w, w), F32)],
        scratch_shapes=[pltpu.VMEM((w, w), F32), pltpu.VMEM((1, RW_COLS), F32)],
        compiler_params=_params("arbitrary", "arbitrary"),
        name="rwkv7",
    )(p_rw, prev, st0, mu, vecs, w2p, a2p, g2p, *consts)


def _outffn_kernel(yrw_ref, yfox_ref, yhg_ref, x_ref, ga1_ref, sh2_ref, sc2_ref, ga2_ref, g2_ref,
                   worw_ref, wofox_ref, wohg_ref, wg_ref, wu_ref, wo_ref, fg_ref, o_ref, *, final_norm, ff_step):
    mix = (jnp.dot(yrw_ref[...].astype(BF16), worw_ref[...], preferred_element_type=F32)
           + jnp.dot(yfox_ref[...].astype(BF16), wofox_ref[...], preferred_element_type=F32)
           + jnp.dot(yhg_ref[...].astype(BF16), wohg_ref[...], preferred_element_type=F32))
    x1 = x_ref[...] + ga1_ref[0] * mix
    ms = jnp.mean(x1 * x1, axis=-1, keepdims=True)
    h2 = x1 * lax.rsqrt(ms + EPS) * g2_ref[...]
    h2 = (h2 * (1.0 + sc2_ref[0]) + sh2_ref[0]).astype(BF16)
    ffn = jnp.zeros_like(x1)
    for j in range(D_FF // ff_step):
        cols = slice(j * ff_step, (j + 1) * ff_step)
        gate = jnp.dot(h2, wg_ref[:, cols], preferred_element_type=F32)
        up = jnp.dot(h2, wu_ref[:, cols], preferred_element_type=F32)
        act = (gate * _sigmoid(gate) * up).astype(BF16)
        ffn = ffn + jnp.dot(act, wo_ref[cols, :], preferred_element_type=F32)
    x2 = x1 + ga2_ref[0] * ffn
    if final_norm:
        ms2 = jnp.mean(x2 * x2, axis=-1, keepdims=True)
        x2 = x2 * lax.rsqrt(ms2 + EPS) * fg_ref[...]
    o_ref[...] = x2


def _outffn(y_rw, y_fox, y_hg, x, ga1, sh2, sc2, ga2, g2, wts, final_g, final_norm, tm, tiles_per_batch):
    n = x.shape[0]
    row = lambda w: pl.BlockSpec((tm, w), lambda i: (i, 0))
    mods = [_mod_spec(m, tiles_per_batch) for m in (ga1, sh2, sc2, ga2)]
    return pl.pallas_call(
        functools.partial(_outffn_kernel, final_norm=final_norm, ff_step=D_FF // 2),
        grid=(n // tm,),
        in_specs=[row(RW_WIDTH), row(FOX_WIDTH), row(HG_WIDTH), row(D_MODEL)] + mods
        + [_const_spec((1, D_MODEL))] + [_const_spec(w.shape) for w in wts] + [_const_spec((1, D_MODEL))],
        out_specs=row(D_MODEL),
        out_shape=jax.ShapeDtypeStruct((n, D_MODEL), F32),
        compiler_params=_params("arbitrary"),
        name="outffn",
    )(y_rw, y_fox, y_hg, x, ga1, sh2, sc2, ga2, g2, *wts, final_g)


def _block_diag_t(s):
    nb, h, dk, dv = s.shape
    st = jnp.swapaxes(s, 2, 3)
    eye = jnp.eye(h, dtype=s.dtype)
    return jnp.einsum("bhvk,hg->bhvgk", st, eye).reshape(nb, h * dv, h * dk)


def _block_diag(s):
    nb, h, dv, dk = s.shape
    eye = jnp.eye(h, dtype=s.dtype)
    return jnp.einsum("bhvk,hg->bhvgk", s, eye).reshape(nb, h * dv, h * dk)


def _diag_blocks(sbd, h):
    nb, rows, cols = sbd.shape
    x = sbd.reshape(nb, h, rows // h, h, cols // h)
    return jnp.stack([x[:, i, :, i, :] for i in range(h)], axis=1)


def _layer_weights(l, w_in, rw_w2, rw_a2, rw_g2, w_out, w_ffn_in, w_ffn_out):
    wi = w_in[l]
    c0, c1 = RW_COLS, RW_COLS + 3 * FOX_WIDTH
    w_kv = wi[:, c0 + FOX_WIDTH:c1].astype(BF16)
    proj = (wi[:, :c0].astype(BF16), wi[:, c0:c0 + FOX_WIDTH].astype(BF16), w_kv, w_kv.T,
            wi[:, c1:c1 + FOX_HEADS].T.astype(BF16), wi[:, c1 + FOX_HEADS:].astype(BF16))
    zeros = lambda r: jnp.zeros((r, RW_WIDTH), F32)
    w2p = jnp.concatenate([rw_w2[l], zeros(RW_LORA - RW_W_LORA)], axis=0)
    a2p = jnp.concatenate([zeros(RW_W_LORA), rw_a2[l], zeros(RW_G_LORA)], axis=0)
    g2p = jnp.concatenate([zeros(RW_W_LORA + RW_A_LORA), rw_g2[l]], axis=0)
    wo = w_out[l].astype(BF16)
    ffn = (wo[:RW_WIDTH], wo[RW_WIDTH:RW_WIDTH + FOX_WIDTH], wo[RW_WIDTH + FOX_WIDTH:],
           w_ffn_in[l][:, :D_FF].astype(BF16), w_ffn_in[l][:, D_FF:].astype(BF16), w_ffn_out[l].astype(BF16))
    return proj, (w2p, a2p, g2p), ffn


def _group_layer(x, mods, lw, lp, layer, kv_stack, prev, rw_st0, hg_st0, cache, nb, t, cfg, final_g, final_norm):
    proj_w, lora_w, ffn_w = lw
    sh1, sc1, ga1, sh2, sc2, ga2 = mods
    tm, tpb = cfg["tm"], cfg["tiles_per_batch"]
    p_rw, q, k, v, kt, vt, f, p_hg = _normproj(x, sh1, sc1, lp["norm1_g"], proj_w, kv_stack, layer, tm, tpb)
    if f.shape[0] == nb:
        f_bht = f
    else:
        f_bht = jnp.swapaxes(f.reshape(FOX_HEADS, nb, t), 0, 1)
    if cache is None:
        logf_bht, c_bhl = _gate(f_bht, lp["fox_b_f"], None)
        y_fox = _fox(q, kt, vt, layer, c_bhl, nb, t, cfg["tq"], cfg["tk"])
        kv_stack = (kt, vt)
    else:
        cache_k, cache_v, cache_logf = cache
        logf_bht, c_bhl = _gate(f_bht, lp["fox_b_f"], jnp.swapaxes(cache_logf, 1, 2))
        y_fox = _fox_cached(q, k, v, cache_k, cache_v, layer, c_bhl, nb, t)
    y_hg, hg_st = _hgrn(p_hg, lp["hg_lb"], lp["hg_norm_g"], hg_st0, nb, t, cfg["hg_chunk"], cfg["hg_tile"])
    y_rw, rw_st = _rwkv(p_rw, prev, rw_st0, lp["rw_mu"], lp["rw_vecs"], *lora_w, nb, t,
                        cfg["rw_chunk"], cfg["rw_tile"])
    x = _outffn(y_rw, y_fox, y_hg, x, ga1, sh2, sc2, ga2, lp["norm2_g"], ffn_w, final_g, final_norm, tm, tpb)
    new_kv = (None, None) if cache is None else (k.reshape(nb, t, FOX_HEADS, HEAD_DIM),
                                                  v.reshape(nb, t, FOX_HEADS, HEAD_DIM))
    states = new_kv + (
        jnp.swapaxes(logf_bht, 1, 2),
        _diag_blocks(rw_st, RW_HEADS),
        p_rw.reshape(nb, t, RW_COLS)[:, t - 1:t, :],
        jnp.swapaxes(_diag_blocks(hg_st, HG_HEADS), 2, 3),
    )
    return x, kv_stack, states


def kernel(x_prompt, x_sample, c_prompt, c_sample, cache_fox_k, cache_fox_v, cache_fox_logf, state_rwkv,
           state_rwkv_shift, state_hgrn, norm1_g, w_ada, b_ada, w_in, rw_mu, rw_w0, rw_w2, rw_a0, rw_a2, rw_g2,
           rw_k_k, rw_k_a, rw_r_k, rw_ln_w, rw_ln_b, fox_b_f, hg_lb_logits, hg_norm_g, w_out, norm2_g,
           w_ffn_in, w_ffn_out, final_norm_g):
    dt = x_prompt.dtype
    bp, tp, _ = x_prompt.shape
    bs, ts, _ = x_sample.shape

    mod = _modulation(jnp.concatenate([c_prompt, c_sample], axis=0).astype(F32), w_ada, b_ada)
    lbs = _hg_lower_bounds(hg_lb_logits.astype(F32))

    cfg_p = dict(tm=512, tiles_per_batch=tp // 512, tq=512, tk=512, hg_chunk=HG_CHUNK, hg_tile=1024,
                 rw_chunk=RW_CHUNK, rw_tile=1024)
    cfg_s = dict(tm=bs * ts, tiles_per_batch=1, hg_chunk=ts, hg_tile=ts, rw_chunk=ts, rw_tile=ts)

    xp = x_prompt.reshape(bp * tp, D_MODEL).astype(F32)
    xs = x_sample.reshape(bs * ts, D_MODEL).astype(F32)
    zero_prev = jnp.zeros((bp, 1, RW_COLS), F32)
    zero_state = jnp.zeros((bp, RW_WIDTH, RW_WIDTH), F32)
    final_g = final_norm_g.reshape(1, D_MODEL)
    row = lambda a: a.reshape(1, -1)

    kv_p = (jnp.zeros((DEPTH, bp, FOX_WIDTH, tp), F32), jnp.zeros((DEPTH, bp, FOX_WIDTH, tp), F32))
    outs_p, outs_s = [], []
    for l in range(DEPTH):
        lw = _layer_weights(l, w_in, rw_w2, rw_a2, rw_g2, w_out, w_ffn_in, w_ffn_out)
        zero_row = jnp.zeros((RW_WIDTH,), F32)
        lp = dict(
            norm1_g=row(norm1_g[l]), norm2_g=row(norm2_g[l]), fox_b_f=fox_b_f[l],
            hg_lb=row(lbs[l]), hg_norm_g=row(hg_norm_g[l]), rw_mu=row(rw_mu[l]),
            rw_vecs=jnp.stack([rw_w0[l], rw_a0[l], rw_k_k[l], rw_k_a[l], rw_r_k[l].reshape(-1),
                               rw_ln_w[l], rw_ln_b[l], zero_row], axis=0),
        )
        chunks_p = [m[:, None, :] for m in jnp.split(mod[l, :bp], 6, axis=-1)]
        chunks_s = [jnp.repeat(m, ts, axis=0)[None] for m in jnp.split(mod[l, bp:], 6, axis=-1)]
        last = l == DEPTH - 1
        xp, kv_p, st_p = _group_layer(xp, chunks_p, lw, lp, l, kv_p, zero_prev, zero_state, zero_state, None,
                                      bp, tp, cfg_p, final_g, last)
        xs, _, st_s = _group_layer(xs, chunks_s, lw, lp, l, None, state_rwkv_shift[l].astype(F32),
                                _block_diag(state_rwkv[l].astype(F32)),
                                _block_diag_t(state_hgrn[l].astype(F32)),
                                (cache_fox_k.astype(F32), cache_fox_v.astype(F32), cache_fox_logf[l].astype(F32)),
                                bs, ts, cfg_s, final_g, last)
        outs_p.append(st_p)
        outs_s.append(st_s)

    stk = lambda outs, j: jnp.stack([o[j] for o in outs], axis=0).astype(dt)
    y_prompt = xp.reshape(bp, tp, D_MODEL).astype(dt)
    y_sample = xs.reshape(bs, ts, D_MODEL).astype(dt)
    unflip = lambda a: jnp.transpose(a.reshape(DEPTH, bp, FOX_HEADS, HEAD_DIM, tp), (0, 1, 4, 2, 3))
    return (y_prompt, y_sample,
            unflip(kv_p[0]).astype(dt), unflip(kv_p[1]).astype(dt), stk(outs_p, 2), stk(outs_p, 3), stk(outs_p, 4), stk(outs_p, 5),
            stk(outs_s, 0), stk(outs_s, 1), stk(outs_s, 2), stk(outs_s, 3), stk(outs_s, 4), stk(outs_s, 5))
```

```python
import functools

import numpy as np
import jax
import jax.numpy as jnp
from jax import lax
from jax.experimental import pallas as pl
from jax.experimental.pallas import tpu as pltpu

F32 = jnp.float32
BF16 = jnp.bfloat16
HIGHEST = lax.Precision.HIGHEST

D_MODEL = 1024
DEPTH = 4
HEAD_DIM = 64
RW_HEADS = 4
RW_WIDTH = RW_HEADS * HEAD_DIM
FOX_HEADS = 8
FOX_WIDTH = FOX_HEADS * HEAD_DIM
HG_HEADS = 4
HG_WIDTH = HG_HEADS * HEAD_DIM
RW_W_LORA = 32
RW_A_LORA = 32
RW_G_LORA = 64
RW_LORA = RW_W_LORA + RW_A_LORA + RW_G_LORA
RW_COLS = 3 * RW_WIDTH + RW_LORA
FOX_COLS = 3 * FOX_WIDTH + FOX_HEADS
HG_COLS = 4 * HG_WIDTH
D_FF = 2816
EPS = 1e-6
RW_GN_EPS = 64e-5
FOX_SCALE = HEAD_DIM ** -0.5
HG_CHUNK = 64
RW_CHUNK = 64

VMEM_LIMIT_BYTES = 56 * 1024 * 1024
SUBLANES, LANES = 8, 128
ROW_TILE = 512
SCAN_TILE = 1024
MOD_COL_TILE = 1536
CUMSUM_TILE = 512

NT_DIMS = (((1,), (1,)), ((), ()))
TN_DIMS = (((0,), (0,)), ((), ()))
NN_DIMS = (((1,), (0,)), ((), ()))


def _params(*sem):
    return pltpu.CompilerParams(dimension_semantics=sem, vmem_limit_bytes=VMEM_LIMIT_BYTES)


def _const_spec(shape):
    nd = len(shape)
    return pl.BlockSpec(shape, lambda *_: (0,) * nd, pipeline_mode=pl.Buffered(1))


def _mm(a, b, dims=NN_DIMS):
    return lax.dot_general(a.astype(BF16), b.astype(BF16), dims, preferred_element_type=F32)


def _split(a):
    hi = a.astype(BF16)
    lo = (a - hi.astype(F32)).astype(BF16)
    return hi, lo


def _mm3(a, b, dims=NN_DIMS):
    ah, al = _split(a)
    bh, bl = _split(b)
    d = lambda x, y: lax.dot_general(x, y, dims, preferred_element_type=F32)
    return d(ah, bh) + (d(ah, bl) + d(al, bh))


def _head_sum(x, ones_bd):
    hi, lo = _split(x)
    d = lambda u: jnp.dot(u, ones_bd, preferred_element_type=F32)
    return d(hi) + d(lo)


def _block_ones_np(width):
    r = np.arange(width) // HEAD_DIM
    return (r[:, None] == r[None, :]).astype(np.float32)


def _log_sigmoid(z):
    return jnp.minimum(z, 0.0) - jnp.log1p(jnp.exp(-jnp.abs(z)))


def _sigmoid(z):
    return 1.0 / (1.0 + jnp.exp(-z))


def _mod_kernel(c_ref, w_ref, b_ref, o_ref):
    c = c_ref[...]
    s = c * _sigmoid(c)
    o_ref[0] = jnp.dot(s, w_ref[0], precision=HIGHEST, preferred_element_type=F32) + b_ref[0]


def _modulation(c_all, w_ada, b_ada):
    nb = c_all.shape[0]
    tn = MOD_COL_TILE
    n_out = w_ada.shape[2]
    return pl.pallas_call(
        _mod_kernel,
        grid=(DEPTH, n_out // tn),
        in_specs=[
            pl.BlockSpec((nb, D_MODEL), lambda l, j: (0, 0)),
            pl.BlockSpec((1, D_MODEL, tn), lambda l, j: (l, 0, j)),
            pl.BlockSpec((1, 1, tn), lambda l, j: (l, 0, j)),
        ],
        out_specs=pl.BlockSpec((1, nb, tn), lambda l, j: (l, 0, j)),
        out_shape=jax.ShapeDtypeStruct((DEPTH, nb, n_out), F32),
        compiler_params=_params("arbitrary", "arbitrary"),
        name="adaln_mod",
    )(c_all, w_ada, b_ada.reshape(DEPTH, 1, n_out))


def _lbs_kernel(x_ref, o_ref):
    x = x_ref[...]
    m = jnp.max(x, axis=0, keepdims=True)
    e = jnp.exp(x - m)
    p = e / jnp.sum(e, axis=0, keepdims=True)
    acc = jnp.zeros_like(p[0:1])
    rows = []
    for l in range(DEPTH):
        acc = acc + p[l:l + 1]
        rows.append(acc)
    cs = jnp.concatenate(rows, axis=0)
    o_ref[...] = cs - cs[0:1]


def _hg_lower_bounds(logits):
    return pl.pallas_call(
        _lbs_kernel,
        out_shape=jax.ShapeDtypeStruct(logits.shape, F32),
        name="hgrn_lower_bounds",
    )(logits)


def _normproj_kernel(x_ref, sh_ref, sc_ref, g_ref, wrw_ref, wq_ref, wkv_ref, wkvt_ref, wft_ref, whg_ref, *refs,
                     transposed_kv):
    prw_ref, q_ref, k_ref, v_ref, kt_ref, vt_ref, ft_ref, phg_ref = refs[-8:]
    x = x_ref[...]
    ms = jnp.mean(x * x, axis=-1, keepdims=True)
    h = x * lax.rsqrt(ms + EPS) * g_ref[...]
    h = h * (1.0 + sc_ref[0]) + sh_ref[0]
    hb = h.astype(BF16)
    prw_ref[...] = jnp.dot(hb, wrw_ref[...], preferred_element_type=F32)
    q_ref[...] = jnp.dot(hb, wq_ref[...], preferred_element_type=F32)
    if transposed_kv:
        kvt = lax.dot_general(wkvt_ref[...], hb, NT_DIMS, preferred_element_type=F32)
        kt_ref[...] = kvt[:FOX_WIDTH]
        vt_ref[...] = kvt[FOX_WIDTH:]
        k_ref[...] = jnp.zeros(k_ref.shape, F32)
        v_ref[...] = jnp.zeros(v_ref.shape, F32)
    else:
        kv = jnp.dot(hb, wkv_ref[...], preferred_element_type=F32)
        k_ref[...] = kv[:, :FOX_WIDTH]
        v_ref[...] = kv[:, FOX_WIDTH:]
        kt_ref[...] = jnp.zeros(kt_ref.shape, F32)
        vt_ref[...] = jnp.zeros(vt_ref.shape, F32)
    ft_ref[0] = lax.dot_general(wft_ref[...], hb, NT_DIMS, preferred_element_type=F32)
    phg_ref[...] = jnp.dot(hb, whg_ref[...], preferred_element_type=F32)


def _mod_spec(arr, tiles_per_batch):
    _, r, d = arr.shape
    return pl.BlockSpec((1, r, d), lambda i: (i // tiles_per_batch, 0, 0))


def _normproj(x, sh, sc, g, wts, kv_stack, layer, tm, tiles_per_batch):
    n = x.shape[0]
    transposed_kv = kv_stack is not None
    wrw, wq, wkv, wkvt, wft, whg = wts
    row = lambda w: pl.BlockSpec((tm, w), lambda i: (i, 0))
    dummy = pl.BlockSpec((SUBLANES, LANES), lambda i: (0, 0))
    dummy_shape = jax.ShapeDtypeStruct((SUBLANES, LANES), F32)
    nbat = n // (tm * tiles_per_batch)
    plain = jax.ShapeDtypeStruct((n, FOX_WIDTH), F32)
    flipped = jax.ShapeDtypeStruct((DEPTH, nbat, FOX_WIDTH, tm * tiles_per_batch), F32)
    flipped_spec = pl.BlockSpec((None, None, FOX_WIDTH, tm),
                                lambda i: (layer, i // tiles_per_batch, 0, i % tiles_per_batch))
    stack_args = list(kv_stack) if transposed_kv else []
    stack_specs = [pl.BlockSpec(memory_space=pl.ANY)] * len(stack_args)
    kv_specs = [dummy, dummy, flipped_spec, flipped_spec] if transposed_kv else [row(FOX_WIDTH)] * 2 + [dummy] * 2
    kv_shapes = [dummy_shape, dummy_shape, flipped, flipped] if transposed_kv else [plain, plain, dummy_shape,
                                                                                   dummy_shape]
    return pl.pallas_call(
        functools.partial(_normproj_kernel, transposed_kv=transposed_kv),
        grid=(n // tm,),
        in_specs=[
            row(D_MODEL), _mod_spec(sh, tiles_per_batch), _mod_spec(sc, tiles_per_batch),
            _const_spec((1, D_MODEL)),
            _const_spec(wrw.shape), _const_spec(wq.shape), _const_spec(wkv.shape), _const_spec(wkvt.shape),
            _const_spec(wft.shape), _const_spec(whg.shape),
        ] + stack_specs,
        out_specs=[row(RW_COLS), row(FOX_WIDTH)] + kv_specs + [
            pl.BlockSpec((1, FOX_HEADS, tm), lambda i: (i // tiles_per_batch, 0, i % tiles_per_batch)),
            row(HG_COLS)],
        out_shape=[jax.ShapeDtypeStruct((n, RW_COLS), F32), plain] + kv_shapes + [
            jax.ShapeDtypeStruct((nbat, FOX_HEADS, tm * tiles_per_batch), F32),
            jax.ShapeDtypeStruct((n, HG_COLS), F32)],
        input_output_aliases={10: 4, 11: 5} if transposed_kv else {},
        compiler_params=_params("arbitrary"),
        name="normproj",
    )(x, sh, sc, g, wrw, wq, wkv, wkvt, wft, whg, *stack_args)


def _cumsum_lanes(x, carry):
    length = x.shape[1]
    step = min(CUMSUM_TILE, length)
    r = lax.broadcasted_iota(jnp.int32, (step, step), 0)
    c = lax.broadcasted_iota(jnp.int32, (step, step), 1)
    upper = (r <= c).astype(F32)
    outs = []
    for j in range(length // step):
        seg = x[:, j * step:(j + 1) * step]
        cs = jnp.dot(seg, upper, precision=HIGHEST, preferred_element_type=F32) + carry
        carry = cs[:, step - 1:step]
        outs.append(cs)
    return outs, carry, step


def _gate_kernel(*refs, past):
    if past:
        f_ref, bf_ref, cache_ref, logf_ref, c_ref = refs
    else:
        f_ref, bf_ref, logf_ref, c_ref = refs
    logf = _log_sigmoid(f_ref[0] + bf_ref[...])
    logf_ref[0] = logf
    carry = jnp.zeros((FOX_HEADS, 1), F32)
    if past:
        outs, carry, step = _cumsum_lanes(cache_ref[0], carry)
        for j, cs in enumerate(outs):
            c_ref[0, :, j * step:(j + 1) * step] = cs
    outs, carry, step = _cumsum_lanes(logf, carry)
    for j, cs in enumerate(outs):
        c_ref[0, :, past + j * step:past + (j + 1) * step] = cs


def _gate(f_bht, b_f, cache_logf_bhp):
    nb, _, t = f_bht.shape
    past = 0 if cache_logf_bhp is None else cache_logf_bhp.shape[2]
    blk = lambda w: pl.BlockSpec((1, FOX_HEADS, w), lambda b: (b, 0, 0))
    in_specs = [blk(t), _const_spec((FOX_HEADS, 1))]
    args = [f_bht, b_f.reshape(FOX_HEADS, 1)]
    if past:
        in_specs.append(blk(past))
        args.append(cache_logf_bhp)
    return pl.pallas_call(
        functools.partial(_gate_kernel, past=past),
        grid=(nb,),
        in_specs=in_specs,
        out_specs=[blk(t), blk(past + t)],
        out_shape=[jax.ShapeDtypeStruct((nb, FOX_HEADS, t), F32),
                   jax.ShapeDtypeStruct((nb, FOX_HEADS, past + t), F32)],
        compiler_params=_params("arbitrary"),
        name="fox_gate",
    )(*args)


LOG2E = 1.4426950408889634
BIAS_ROWS = 16
DIAG_SPLIT = 4
SKIP_MARGIN = 137.0


def _fox_kernel(q_ref, kt_ref, vt_ref, c_ref, o_ref, kaug_scr, vaug_scr, kmax_scr, bmax_scr, *, tq):
    qi = pl.program_id(2)
    hd = HEAD_DIM
    nq = kaug_scr.shape[2] // tq
    lane = lax.broadcasted_iota(jnp.int32, (1, 2 * hd), 1)
    bias_lane0 = (hd, 0)
    ones_lane = (hd, 0)

    @pl.when(qi == 0)
    def _():
        kt = kt_ref[...]
        v = vt_ref[...].T
        t = kt.shape[1]
        for hh in range(2):
            c2 = c_ref[0, hh] * LOG2E
            hi = c2.astype(BF16).astype(F32)
            mid = (c2 - hi).astype(BF16).astype(F32)
            lo = c2 - hi - mid
            bias = jnp.concatenate([-hi, -mid, -lo, jnp.zeros((BIAS_ROWS - 3, t), F32)], axis=0).astype(BF16)
            pad = jnp.zeros((hd - BIAS_ROWS, t), BF16)
            kh = kt[hh * hd:(hh + 1) * hd].astype(BF16)
            if hh == 0:
                kaug_scr[hh] = jnp.concatenate([kh, bias, pad], axis=0)
            else:
                kaug_scr[hh] = jnp.concatenate([bias, pad, kh], axis=0)
            own = (lane // hd) == hh
            vaug_scr[hh] = jnp.where(own, v, jnp.where(lane == ones_lane[hh], 1.0, 0.0)).astype(BF16)
            kf = kh.astype(F32)
            knorm = jnp.sqrt(jnp.sum(kf * kf, axis=0, keepdims=True))
            bsum = -(hi + mid + lo.astype(BF16).astype(F32))
            for j in range(nq):
                kmax_scr[hh, j] = jnp.max(knorm[:, j * tq:(j + 1) * tq])
                bmax_scr[hh, j] = jnp.max(bsum[:, j * tq:(j + 1) * tq])

    q = q_ref[...] * (FOX_SCALE * LOG2E)
    qa = []
    for hh in range(2):
        own = (lane // hd) == hh
        is_bias = (lane >= bias_lane0[hh]) & (lane < bias_lane0[hh] + 3)
        qa.append(jnp.where(own, q, jnp.where(is_bias, 1.0, 0.0)).astype(BF16))

    def scores(qh, hh, start, nk):
        return jnp.dot(qh, kaug_scr[hh, :, pl.ds(start, nk)], preferred_element_type=F32)

    def update(s, hh, start, m, acc, row0):
        if row0 is not None:
            rows_i = lax.broadcasted_iota(jnp.int32, s.shape, 0) + row0
            cols_i = lax.broadcasted_iota(jnp.int32, s.shape, 1)
            s = jnp.where(cols_i <= rows_i, s, -jnp.inf)
        width = 2 * hd
        part = s
        if s.shape[1] % width == 0:
            part = s[:, 0:width]
            for c0 in range(width, s.shape[1], width):
                part = jnp.maximum(part, s[:, c0:c0 + width])
        m_new = jnp.maximum(m, jnp.max(part, axis=-1, keepdims=True))
        alpha = jnp.exp2(m - m_new)
        p = jnp.exp2(s - m_new).astype(BF16)
        acc = alpha * acc + jnp.dot(p, vaug_scr[hh, pl.ds(start, s.shape[1]), :], preferred_element_type=F32)
        return m_new, acc

    def chunk(j, carry):
        start = pl.multiple_of(j * tq, tq)
        sc = [scores(qa[hh], hh, start, tq) for hh in range(2)]
        return tuple(update(sc[hh], hh, start, *carry[hh], None) for hh in range(2))

    start = pl.multiple_of(qi * tq, tq)
    hr = tq // DIAG_SPLIT
    blocks = [(hh, r) for hh in range(2) for r in range(DIAG_SPLIT)]
    sc = [scores(qa[hh][r * hr:(r + 1) * hr], hh, start, (r + 1) * hr) for hh, r in blocks]
    first = [update(s, hh, start, jnp.full((hr, 1), -jnp.inf, F32), jnp.zeros((hr, 2 * hd), F32), r * hr)
             for s, (hh, r) in zip(sc, blocks)]
    carry = tuple((jnp.max(jnp.concatenate([jnp.broadcast_to(m, (hr, 2 * hd))
                                            for m, _ in first[hh * DIAG_SPLIT:(hh + 1) * DIAG_SPLIT]], axis=0),
                           axis=-1, keepdims=True),
                   jnp.concatenate([a for _, a in first[hh * DIAG_SPLIT:(hh + 1) * DIAG_SPLIT]], axis=0))
                  for hh in range(2))

    thresholds, q_norms = [], []
    for hh in range(2):
        qf = jnp.where((lane // hd) == hh, qa[hh].astype(F32), 0.0)
        q_norms.append(jnp.sqrt(jnp.max(jnp.sum(qf * qf, axis=-1, keepdims=True))))
        thresholds.append(jnp.min(carry[hh][0]) - SKIP_MARGIN)
    first_j = []
    for hh in range(2):
        fj = jnp.int32(0)
        for j in range(nq - 1):
            skippable = (j < qi) & (q_norms[hh] * kmax_scr[hh, j] + bmax_scr[hh, j] < thresholds[hh])
            fj = jnp.where(skippable & (fj == j), j + 1, fj)
        first_j.append(fj)
    shared_j = jnp.maximum(first_j[0], first_j[1])

    carry = list(carry)
    for hh in range(2):
        def alone(j, c, hh=hh):
            start = pl.multiple_of(j * tq, tq)
            return update(scores(qa[hh], hh, start, tq), hh, start, *c, None)
        carry[hh] = lax.fori_loop(first_j[hh], shared_j, alone, carry[hh])
    carry = lax.fori_loop(shared_j, qi, chunk, tuple(carry))
    outs = []
    for hh in range(2):
        acc = carry[hh][1]
        outs.append(acc / acc[:, ones_lane[hh]:ones_lane[hh] + 1])
    o_ref[...] = jnp.where(lane < hd, outs[0], outs[1])


def _fox(q, kt_stack, vt_stack, layer, c_bht, nb, t, tq):
    n = q.shape[0]
    nq = t // tq
    hw = 2 * HEAD_DIM
    return pl.pallas_call(
        functools.partial(_fox_kernel, tq=tq),
        grid=(nb, FOX_HEADS // 2, nq),
        in_specs=[
            pl.BlockSpec((tq, hw), lambda b, h, i: (b * nq + i, h)),
            pl.BlockSpec((None, None, hw, t), lambda b, h, i: (layer, b, h, 0)),
            pl.BlockSpec((None, None, hw, t), lambda b, h, i: (layer, b, h, 0)),
            pl.BlockSpec((1, 2, 1, t), lambda b, h, i: (b, h, 0, 0)),
        ],
        out_specs=pl.BlockSpec((tq, hw), lambda b, h, i: (b * nq + i, h)),
        out_shape=jax.ShapeDtypeStruct((n, FOX_WIDTH), F32),
        scratch_shapes=[pltpu.VMEM((2, hw, t), BF16), pltpu.VMEM((2, t, hw), BF16),
                        pltpu.SMEM((2, nq), F32), pltpu.SMEM((2, nq), F32)],
        compiler_params=_params("arbitrary", "arbitrary", "arbitrary"),
        name="fox_attention",
    )(q, kt_stack, vt_stack, c_bht.reshape(nb, FOX_HEADS, 1, t))


def _fox_cached_kernel(q_ref, kn_ref, vn_ref, kc_ref, vc_ref, c_ref, o_ref, *, past, t):
    row = lax.broadcasted_iota(jnp.int32, (t, t), 0)
    col = lax.broadcasted_iota(jnp.int32, (t, t), 1)
    q = q_ref[...] * FOX_SCALE
    kn_all = kn_ref[...]
    vn_all = vn_ref[...]
    outs = []
    for h in range(FOX_HEADS):
        lanes = slice(h * HEAD_DIM, (h + 1) * HEAD_DIM)
        qh = q[:, lanes].astype(BF16)
        kc = kc_ref[h * HEAD_DIM:(h + 1) * HEAD_DIM, :].astype(BF16)
        vc = vc_ref[h * HEAD_DIM:(h + 1) * HEAD_DIM, :].astype(BF16)
        kn = kn_all[:, lanes].astype(BF16)
        vn = vn_all[:, lanes].astype(BF16)
        s1 = jnp.dot(qh, kc, preferred_element_type=F32)
        s1 = s1 - c_ref[0, h, :, 0:past]
        s2 = lax.dot_general(qh, kn, NT_DIMS, preferred_element_type=F32)
        s2 = s2 - c_ref[0, h, :, past:past + t]
        s2 = jnp.where(col <= row, s2, -jnp.inf)
        m = jnp.maximum(jnp.max(s1, axis=-1, keepdims=True), jnp.max(s2, axis=-1, keepdims=True))
        p1 = jnp.exp(s1 - m)
        p2 = jnp.exp(s2 - m)
        l = jnp.sum(p1, axis=-1, keepdims=True) + jnp.sum(p2, axis=-1, keepdims=True)
        acc = (lax.dot_general(p1.astype(BF16), vc, NT_DIMS, preferred_element_type=F32)
               + jnp.dot(p2.astype(BF16), vn, preferred_element_type=F32))
        outs.append(acc / l)
    o_ref[...] = jnp.concatenate(outs, axis=1)


def _fox_cached(q, k, v, cache_k, cache_v, layer, c_bhl, nb, t):
    n = q.shape[0]
    past = cache_k.shape[2]
    flip = lambda c: jnp.transpose(c, (0, 1, 3, 4, 2)).reshape(DEPTH, nb, FOX_WIDTH, past)
    cache_k, cache_v = flip(cache_k), flip(cache_v)
    new = pl.BlockSpec((t, FOX_WIDTH), lambda b: (b, 0))
    old = pl.BlockSpec((None, None, FOX_WIDTH, past), lambda b: (layer, b, 0, 0))
    return pl.pallas_call(
        functools.partial(_fox_cached_kernel, past=past, t=t),
        grid=(nb,),
        in_specs=[new, new, new, old, old,
                  pl.BlockSpec((1, FOX_HEADS, 1, past + t), lambda b: (b, 0, 0, 0))],
        out_specs=new,
        out_shape=jax.ShapeDtypeStruct((n, FOX_WIDTH), F32),
        compiler_params=_params("arbitrary"),
        name="fox_attention_cached",
    )(q, k, v, cache_k, cache_v, c_bhl.reshape(nb, FOX_HEADS, 1, past + t))


def _levels(chunk):
    out, m = [], chunk // 2
    while m >= 1:
        out.append(m)
        m //= 2
    return out


FINE_LEVEL = 4


def _decay_sum_matrix(chunk):
    r = np.arange(chunk)[:, None]
    j = np.arange(chunk)[None, :]
    incl = (j <= r).astype(np.float32)
    blocks = [incl]
    for m in _levels(chunk):
        if m <= FINE_LEVEL:
            mid = (r // (2 * m)) * (2 * m) + m - 1
            blocks.append(incl - (j <= mid).astype(np.float32))
    return np.concatenate(blocks, axis=0)


def _level_ids(chunk):
    levels = _levels(chunk)
    t = np.arange(chunk)
    x = t[:, None] ^ t[None, :]
    m = np.where(x > 0, 2 ** np.floor(np.log2(np.maximum(x, 1))).astype(np.int64), 0)
    lvl = np.full((chunk, chunk), len(levels) + 1, np.int32)
    for li, lm in enumerate(levels):
        lvl[(t[:, None] > t[None, :]) & (m == lm)] = li
    lvl[x == 0] = len(levels)
    return lvl


def _level_ids_wide(chunk, heads):
    return np.tile(_level_ids(chunk), (1, heads))


def _two_pass(mat_bf16, x):
    hi, lo = _split(x)
    d = lambda u: jnp.dot(mat_bf16, u, preferred_element_type=F32)
    return d(hi) + d(lo)


def _hgrn_kernel(p_ref, lb_ref, ng_ref, s0_ref, dmat_ref, lvl_ref, ones_ref, o_ref, st_ref, st_scr, *, chunk,
                 nchunks, group):
    ti = pl.program_id(1)

    @pl.when(ti == 0)
    def _():
        st_scr[...] = s0_ref[0]

    w = HG_WIDTH
    levels = _levels(chunk)
    lb = lb_ref[...]
    log_lb = jnp.log(lb)
    log_1m_lb = jnp.log1p(-lb)
    row = lax.broadcasted_iota(jnp.int32, (chunk, 1), 0)
    lane_head = lax.broadcasted_iota(jnp.int32, (1, w), 1) // HEAD_DIM
    row_head = lax.broadcasted_iota(jnp.int32, (w, 1), 0) // HEAD_DIM

    head_lanes = [(lane_head == h).astype(BF16) for h in range(HG_HEADS)]

    def stack(x):
        xb = x.astype(BF16)
        return jnp.concatenate([xb * m for m in head_lanes], axis=0)

    def each(fn, *lists):
        return [fn(*xs) for xs in zip(*lists)]

    def chunk_group(gi, _):
        base = gi * (group * chunk)
        rows = [pl.ds(pl.multiple_of(base + i * chunk, chunk), chunk) for i in range(group)]
        q = [p_ref[r, 0:w] for r in rows]
        fx = [p_ref[r, w:2 * w] for r in rows]
        val = [p_ref[r, 2 * w:3 * w] for r in rows]
        e = each(lambda x: jnp.exp(-jnp.abs(x)), fx)
        bv = each(lambda x, ei: log_1m_lb + (jnp.minimum(x, 0.0) - jnp.log1p(ei)), fx, e)
        logf = each(lambda x: jnp.maximum(log_lb, x) + jnp.log1p(jnp.exp(-jnp.abs(log_lb - x))), bv)
        kd = each(lambda x, ei: (1.0 - lb) * (jnp.where(x >= 0.0, ei, 1.0) / (1.0 + ei)), fx, e)
        sums = each(lambda x: _two_pass(dmat_ref[...], x), logf)
        b = [x[0:chunk] for x in sums]

        lvl = lvl_ref[...]
        attn = each(lambda qi, ki: jnp.where(lvl == len(levels), _mm(qi, stack(ki), NT_DIMS), 0.0), q, kd)
        fine = 0
        for li, m in enumerate(levels):
            if m > FINE_LEVEL:
                def rel(bi):
                    mids = [jnp.broadcast_to(bi[s0 + m - 1:s0 + m, :], (2 * m, w)) for s0 in range(0, chunk, 2 * m)]
                    return bi - jnp.concatenate(mids, axis=0)
                x = each(rel, b)
            else:
                fine += 1
                x = [sm[fine * chunk:(fine + 1) * chunk] for sm in sums]
            wgt = each(lambda xi: jnp.exp(-jnp.abs(xi)), x)
            second = ((row // m) % 2) == 1
            ql = each(lambda qi, wi: jnp.where(second, qi * wi, 0.0), q, wgt)
            kl = each(lambda ki, wi: jnp.where(second, 0.0, ki * wi), kd, wgt)
            attn = each(lambda at, qi, ki: jnp.where(lvl == li, _mm(qi, stack(ki), NT_DIMS), at), attn, ql, kl)
        o_intra = each(lambda at, vi: _mm(at, stack(vi)), attn, val)
        q_dec = each(lambda qi, bi: qi * jnp.exp(bi), q, b)
        b_last = [bi[chunk - 1:chunk, :] for bi in b]
        upd = each(lambda vi, ki, bl, bi: _mm(vi, ki * jnp.exp(bl - bi), TN_DIMS), val, kd, b_last, b)

        st = st_scr[...]
        for i in range(group):
            o = _mm(q_dec[i], st, NT_DIMS) + o_intra[i]
            st = jnp.exp(b_last[i]) * st + jnp.where(row_head == lane_head, upd[i], 0.0)
            ms = _head_sum(o * o, ones_ref[...]) * (1.0 / HEAD_DIM)
            g = p_ref[rows[i], 3 * w:4 * w]
            o_ref[rows[i], :] = o * lax.rsqrt(ms + EPS) * ng_ref[...] * (g * _sigmoid(g))
        st_scr[...] = st
        return 0

    lax.fori_loop(0, nchunks // group, chunk_group, 0)

    @pl.when(ti == pl.num_programs(1) - 1)
    def _():
        st_ref[0] = st_scr[...]


def _hgrn(p_hg, lb, norm_g, st0, nb, t, chunk, tile):
    n = p_hg.shape[0]
    nt = t // tile
    w = HG_WIDTH
    dmat = jnp.asarray(_decay_sum_matrix(chunk), BF16)
    lvl = jnp.asarray(_level_ids_wide(chunk, HG_HEADS))
    ones = jnp.asarray(_block_ones_np(w), BF16)
    return pl.pallas_call(
        functools.partial(_hgrn_kernel, chunk=chunk, nchunks=tile // chunk, group=min(4, tile // chunk)),
        grid=(nb, nt),
        in_specs=[
            pl.BlockSpec((tile, HG_COLS), lambda b, i: (b * nt + i, 0)),
            _const_spec((1, w)), _const_spec((1, w)),
            pl.BlockSpec((1, w, w), lambda b, i: (b, 0, 0)),
            _const_spec(dmat.shape), _const_spec(lvl.shape), _const_spec(ones.shape),
        ],
        out_specs=[pl.BlockSpec((tile, w), lambda b, i: (b * nt + i, 0)),
                   pl.BlockSpec((1, w, w), lambda b, i: (b, 0, 0))],
        out_shape=[jax.ShapeDtypeStruct((n, w), F32), jax.ShapeDtypeStruct((nb, w, w), F32)],
        scratch_shapes=[pltpu.VMEM((w, w), F32)],
        compiler_params=_params("arbitrary", "arbitrary"),
        name="hgrn2",
    )(p_hg, lb, norm_g, st0, dmat, lvl, ones)


def _rwkv_kernel(p_ref, prev_ref, s0_ref, mu_ref, vec_ref, w2_ref, a2_ref, g2_ref, ltri_ref, ones_ref,
                 lvlw_ref, y_ref, st_ref, st_scr, prev_scr, *, chunk, nchunks, group):
    ti = pl.program_id(1)

    @pl.when(ti == 0)
    def _():
        st_scr[...] = s0_ref[0]
        prev_scr[...] = prev_ref[0]

    w = RW_WIDTH
    hc = RW_HEADS * chunk
    levels = _levels(chunk)
    nl = len(levels)
    w0 = vec_ref[0:1, :]
    a0 = vec_ref[1:2, :]
    k_k = vec_ref[2:3, :]
    k_a = vec_ref[3:4, :]
    r_k = vec_ref[4:5, :]
    ln_w = vec_ref[5:6, :]
    ln_b = vec_ref[6:7, :]
    row = lax.broadcasted_iota(jnp.int32, (chunk, 1), 0)
    lane_head = lax.broadcasted_iota(jnp.int32, (1, w), 1) // HEAD_DIM
    row_id = lax.broadcasted_iota(jnp.int32, (w, 1), 0)
    lane_id = lax.broadcasted_iota(jnp.int32, (1, w), 1)
    row_head = row_id // HEAD_DIM
    wide_head = lax.broadcasted_iota(jnp.int32, (1, hc), 1) // chunk

    head_lanes = [(lane_head == h).astype(BF16) for h in range(RW_HEADS)]
    head_lanes2 = [jnp.concatenate([m, m], axis=1) for m in head_lanes]
    wide_lanes = [(wide_head == h).astype(BF16) for h in range(RW_HEADS)]

    def stack(x):
        xb = x.astype(BF16)
        masks = head_lanes if x.shape[1] == w else head_lanes2
        return jnp.concatenate([xb * m for m in masks], axis=0)

    def blockdiag(xw):
        xb = xw.astype(BF16)
        return jnp.concatenate([xb * m for m in wide_lanes], axis=0)

    def each(fn, *lists):
        return [fn(*xs) for xs in zip(*lists)]

    def chunk_group(gi, _):
        base = gi * (group * chunk)
        rows = [pl.ds(pl.multiple_of(base + i * chunk, chunk), chunk) for i in range(group)]
        ones_bd = ones_ref[...]
        mu = mu_ref[...]
        ps = [p_ref[r, :] for r in rows]
        prevs = [prev_scr[...]] + [p[chunk - 1:chunk, :] for p in ps[:-1]]
        prev_scr[...] = ps[-1][chunk - 1:chunk, :]
        xs = each(lambda p, pv: p + mu * (jnp.where(row == 0, pv, pltpu.roll(p, 1, 0)) - p), ps, prevs)
        r = [x[:, 0:w] for x in xs]
        k = [x[:, w:2 * w] for x in xs]
        v = [x[:, 2 * w:3 * w] for x in xs]
        tail = [x[:, 3 * w:] for x in xs]
        w_lin = each(lambda t: w0 + _mm3(jnp.tanh(t), w2_ref[...]), tail)
        a = each(lambda t: _sigmoid(a0 + _mm(t, a2_ref[...])), tail)
        g = each(lambda t: _mm(_sigmoid(t), g2_ref[...]), tail)
        logw = each(lambda wl: -jnp.exp(-(jnp.maximum(-wl, 0.0) + jnp.log1p(jnp.exp(-jnp.abs(wl)))) - 0.5), w_lin)
        kk = each(lambda x: x * k_k, k)
        kk = each(lambda x: x / jnp.maximum(jnp.sqrt(_head_sum(x * x, ones_bd)), 1e-12), kk)
        kmod = each(lambda x, ai: x * (1.0 + (ai - 1.0) * k_a), k, a)
        beta = each(lambda x, ai: x * ai, kk, a)

        lg = each(lambda x: _two_pass(ltri_ref[...], x), logw)
        lg_end = [x[chunk - 1:chunk, :] for x in lg]
        a_n = each(lambda x, l, lw: x * jnp.exp(l - lw), kk, lg, logw)
        r_n = each(lambda x, l: x * jnp.exp(l), r, lg)
        inv_g = each(lambda l: jnp.exp(-l), lg)
        to_end = each(lambda le, l: jnp.exp(le - l), lg_end, lg)
        v_s = each(stack, v)

        gram = each(lambda an, rn, be, km, ig: _mm(jnp.concatenate([an, rn], axis=0),
                                                   jnp.concatenate([stack(be * ig), stack(km * ig)], axis=0),
                                                   NT_DIMS), a_n, r_n, beta, kmod, inv_g)
        lvlw = lvlw_ref[...]
        strict = lvlw < nl
        incl = lvlw <= nl
        n_w = [jnp.where(strict, x[0:chunk, 0:hc], 0.0).astype(BF16) for x in gram]
        m_k = [jnp.where(strict, x[0:chunk, hc:2 * hc], 0.0) for x in gram]
        l_b = [jnp.where(incl, x[chunk:2 * chunk, 0:hc], 0.0) for x in gram]
        l_k = [jnp.where(incl, x[chunk:2 * chunk, hc:2 * hc], 0.0) for x in gram]

        t_w = each(lambda n: (jnp.where(lvlw == nl, 1.0, 0.0)
                              - jnp.where(lvlw == nl - 1, n, jnp.zeros_like(n)).astype(F32)), n_w)
        blk = 2
        while blk < chunk:
            li = levels.index(blk)
            level = (lvlw == li).astype(BF16)
            x_w = each(lambda t, n: _mm(t, blockdiag(n * level)), t_w, n_w)
            t_w = each(lambda t, x: t - _mm(x, blockdiag(t)), t_w, x_w)
            blk *= 2

        rhs = each(lambda an, mk, vs: jnp.concatenate([an, _mm(mk, vs)], axis=1), a_n, m_k, v_s)
        x0 = each(lambda t, rh: _mm(t, stack(rh)), t_w, rhs)

        def residual(n, rh, x):
            x_hi, x_lo = _split(x)
            return rh - x - (jnp.dot(n, stack(x_hi), preferred_element_type=F32)
                             + jnp.dot(n, stack(x_lo), preferred_element_type=F32))

        res = each(residual, n_w, rhs, x0)
        sol = each(lambda x, t, rs: x + _mm(t, stack(rs)), x0, t_w, res)
        a_hat = [x[:, 0:w] for x in sol]
        e0 = [x[:, w:2 * w] for x in sol]
        lb_x = each(lambda lb, x: _mm(lb, stack(x)), l_b, sol)
        r_hat = each(lambda rn, lx: rn - lx[:, 0:w], r_n, lb_x)
        y0 = each(lambda lk, vs, lx: _mm(lk, vs) - lx[:, w:2 * w], l_k, v_s, lb_x)
        same_head = row_head == lane_head
        bh = each(lambda be, te: be * te, beta, to_end)
        kh = each(lambda km, te: km * te, kmod, to_end)
        trans = each(lambda ah, b_, le: (jnp.where(row_id == lane_id, jnp.exp(le), 0.0)
                                         - jnp.where(same_head, _mm(ah, b_, TN_DIMS), 0.0)), a_hat, bh, lg_end)
        q0 = each(lambda vi, ei, k_, b_: jnp.where(same_head, _mm(jnp.concatenate([vi, ei], axis=0),
                                                                   jnp.concatenate([k_, -b_], axis=0), TN_DIMS), 0.0),
                  v, e0, kh, bh)

        st = st_scr[...]
        for i in range(group):
            y = _mm(r_hat[i], st, NT_DIMS) + y0[i]
            st = _mm(st, trans[i]) + q0[i]

            inv_n = 1.0 / HEAD_DIM
            mean = _head_sum(y, ones_bd) * inv_n
            yc = y - mean
            var = _head_sum(yc * yc, ones_bd) * inv_n
            yn = yc * lax.rsqrt(var + RW_GN_EPS) * ln_w + ln_b
            bonus = _head_sum(r[i] * kmod[i] * r_k, ones_bd) * v[i]
            y_ref[rows[i], :] = (yn + bonus) * g[i]
        st_scr[...] = st
        return 0

    lax.fori_loop(0, nchunks // group, chunk_group, 0)

    @pl.when(ti == pl.num_programs(1) - 1)
    def _():
        st_ref[0] = st_scr[...]


def _rwkv(p_rw, prev, st0, mu, vecs, w2p, a2p, g2p, nb, t, chunk, tile):
    n = p_rw.shape[0]
    nt = t // tile
    w = RW_WIDTH
    consts = (jnp.asarray(np.tril(np.ones((chunk, chunk), np.float32)), BF16),
              jnp.asarray(_block_ones_np(w), BF16),
              jnp.asarray(_level_ids_wide(chunk, RW_HEADS)))
    return pl.pallas_call(
        functools.partial(_rwkv_kernel, chunk=chunk, nchunks=tile // chunk, group=min(8, tile // chunk)),
        grid=(nb, nt),
        in_specs=[
            pl.BlockSpec((tile, RW_COLS), lambda b, i: (b * nt + i, 0)),
            pl.BlockSpec((1, 1, RW_COLS), lambda b, i: (b, 0, 0)),
            pl.BlockSpec((1, w, w), lambda b, i: (b, 0, 0)),
            _const_spec((1, RW_COLS)), _const_spec((8, w)),
            _const_spec((RW_LORA, w)), _const_spec((RW_LORA, w)), _const_spec((RW_LORA, w)),
        ] + [_const_spec(c.shape) for c in consts],
        out_specs=[pl.BlockSpec((tile, w), lambda b, i: (b * nt + i, 0)),
                   pl.BlockSpec((1, w, w), lambda b, i: (b, 0, 0))],
        out_shape=[jax.ShapeDtypeStruct((n, w), F32), jax.ShapeDtypeStruct((nb, w, w), F32)],
        scratch_shapes=[pltpu.VMEM((w, w), F32), pltpu.VMEM((1, RW_COLS), F32)],
        compiler_params=_params("arbitrary", "arbitrary"),
        name="rwkv7",
    )(p_rw, prev, st0, mu, vecs, w2p, a2p, g2p, *consts)


def _outffn_kernel(yrw_ref, yfox_ref, yhg_ref, x_ref, ga1_ref, sh2_ref, sc2_ref, ga2_ref, g2_ref,
                   worw_ref, wofox_ref, wohg_ref, wg_ref, wu_ref, wo_ref, fg_ref, o_ref, *, final_norm, ff_step):
    mix = (jnp.dot(yrw_ref[...].astype(BF16), worw_ref[...], preferred_element_type=F32)
           + jnp.dot(yfox_ref[...].astype(BF16), wofox_ref[...], preferred_element_type=F32)
           + jnp.dot(yhg_ref[...].astype(BF16), wohg_ref[...], preferred_element_type=F32))
    x1 = x_ref[...] + ga1_ref[0] * mix
    ms = jnp.mean(x1 * x1, axis=-1, keepdims=True)
    h2 = x1 * lax.rsqrt(ms + EPS) * g2_ref[...]
    h2 = (h2 * (1.0 + sc2_ref[0]) + sh2_ref[0]).astype(BF16)
    ffn = jnp.zeros_like(x1)
    for j in range(D_FF // ff_step):
        cols = slice(j * ff_step, (j + 1) * ff_step)
        gate = jnp.dot(h2, wg_ref[:, cols], preferred_element_type=F32)
        up = jnp.dot(h2, wu_ref[:, cols], preferred_element_type=F32)
        act = (gate * _sigmoid(gate) * up).astype(BF16)
        ffn = ffn + jnp.dot(act, wo_ref[cols, :], preferred_element_type=F32)
    x2 = x1 + ga2_ref[0] * ffn
    if final_norm:
        ms2 = jnp.mean(x2 * x2, axis=-1, keepdims=True)
        x2 = x2 * lax.rsqrt(ms2 + EPS) * fg_ref[...]
    o_ref[...] = x2


def _outffn(y_rw, y_fox, y_hg, x, ga1, sh2, sc2, ga2, g2, wts, final_g, final_norm, tm, tiles_per_batch):
    n = x.shape[0]
    row = lambda w: pl.BlockSpec((tm, w), lambda i: (i, 0))
    mods = [_mod_spec(m, tiles_per_batch) for m in (ga1, sh2, sc2, ga2)]
    return pl.pallas_call(
        functools.partial(_outffn_kernel, final_norm=final_norm, ff_step=D_FF // 2),
        grid=(n // tm,),
        in_specs=[row(RW_WIDTH), row(FOX_WIDTH), row(HG_WIDTH), row(D_MODEL)] + mods
        + [_const_spec((1, D_MODEL))] + [_const_spec(w.shape) for w in wts] + [_const_spec((1, D_MODEL))],
        out_specs=row(D_MODEL),
        out_shape=jax.ShapeDtypeStruct((n, D_MODEL), F32),
        compiler_params=_params("arbitrary"),
        name="outffn",
    )(y_rw, y_fox, y_hg, x, ga1, sh2, sc2, ga2, g2, *wts, final_g)


def _block_diag_t(s):
    nb, h, dk, dv = s.shape
    st = jnp.swapaxes(s, 2, 3)
    eye = jnp.eye(h, dtype=s.dtype)
    return jnp.einsum("bhvk,hg->bhvgk", st, eye).reshape(nb, h * dv, h * dk)


def _block_diag(s):
    nb, h, dv, dk = s.shape
    eye = jnp.eye(h, dtype=s.dtype)
    return jnp.einsum("bhvk,hg->bhvgk", s, eye).reshape(nb, h * dv, h * dk)


def _diag_blocks(sbd, h):
    nb, rows, cols = sbd.shape
    x = sbd.reshape(nb, h, rows // h, h, cols // h)
    return jnp.stack([x[:, i, :, i, :] for i in range(h)], axis=1)


def _layer_weights(l, w_in, rw_w2, rw_a2, rw_g2, w_out, w_ffn_in, w_ffn_out):
    wi = w_in[l]
    c0, c1 = RW_COLS, RW_COLS + 3 * FOX_WIDTH
    w_kv = wi[:, c0 + FOX_WIDTH:c1].astype(BF16)
    proj = (wi[:, :c0].astype(BF16), wi[:, c0:c0 + FOX_WIDTH].astype(BF16), w_kv, w_kv.T,
            wi[:, c1:c1 + FOX_HEADS].T.astype(BF16), wi[:, c1 + FOX_HEADS:].astype(BF16))
    zeros = lambda r: jnp.zeros((r, RW_WIDTH), F32)
    w2p = jnp.concatenate([rw_w2[l], zeros(RW_LORA - RW_W_LORA)], axis=0)
    a2p = jnp.concatenate([zeros(RW_W_LORA), rw_a2[l], zeros(RW_G_LORA)], axis=0)
    g2p = jnp.concatenate([zeros(RW_W_LORA + RW_A_LORA), rw_g2[l]], axis=0)
    wo = w_out[l].astype(BF16)
    ffn = (wo[:RW_WIDTH], wo[RW_WIDTH:RW_WIDTH + FOX_WIDTH], wo[RW_WIDTH + FOX_WIDTH:],
           w_ffn_in[l][:, :D_FF].astype(BF16), w_ffn_in[l][:, D_FF:].astype(BF16), w_ffn_out[l].astype(BF16))
    return proj, (w2p, a2p, g2p), ffn


def _group_layer(x, mods, lw, lp, layer, kv_stack, prev, rw_st0, hg_st0, cache, nb, t, cfg, final_g, final_norm):
    proj_w, lora_w, ffn_w = lw
    sh1, sc1, ga1, sh2, sc2, ga2 = mods
    tm, tpb = cfg["tm"], cfg["tiles_per_batch"]
    p_rw, q, k, v, kt, vt, f, p_hg = _normproj(x, sh1, sc1, lp["norm1_g"], proj_w, kv_stack, layer, tm, tpb)
    if f.shape[0] == nb:
        f_bht = f
    else:
        f_bht = jnp.swapaxes(f.reshape(FOX_HEADS, nb, t), 0, 1)
    if cache is None:
        logf_bht, c_bhl = _gate(f_bht, lp["fox_b_f"], None)
        y_fox = _fox(q, kt, vt, layer, c_bhl, nb, t, cfg["tq"])
        kv_stack = (kt, vt)
    else:
        cache_k, cache_v, cache_logf = cache
        logf_bht, c_bhl = _gate(f_bht, lp["fox_b_f"], jnp.swapaxes(cache_logf, 1, 2))
        y_fox = _fox_cached(q, k, v, cache_k, cache_v, layer, c_bhl, nb, t)
    y_hg, hg_st = _hgrn(p_hg, lp["hg_lb"], lp["hg_norm_g"], hg_st0, nb, t, cfg["hg_chunk"], cfg["hg_tile"])
    y_rw, rw_st = _rwkv(p_rw, prev, rw_st0, lp["rw_mu"], lp["rw_vecs"], *lora_w, nb, t,
                        cfg["rw_chunk"], cfg["rw_tile"])
    x = _outffn(y_rw, y_fox, y_hg, x, ga1, sh2, sc2, ga2, lp["norm2_g"], ffn_w, final_g, final_norm, tm, tpb)
    new_kv = (None, None) if cache is None else (k.reshape(nb, t, FOX_HEADS, HEAD_DIM),
                                                  v.reshape(nb, t, FOX_HEADS, HEAD_DIM))
    states = new_kv + (
        jnp.swapaxes(logf_bht, 1, 2),
        _diag_blocks(rw_st, RW_HEADS),
        p_rw.reshape(nb, t, RW_COLS)[:, t - 1:t, :],
        jnp.swapaxes(_diag_blocks(hg_st, HG_HEADS), 2, 3),
    )
    return x, kv_stack, states


def kernel(x_prompt, x_sample, c_prompt, c_sample, cache_fox_k, cache_fox_v, cache_fox_logf, state_rwkv,
           state_rwkv_shift, state_hgrn, norm1_g, w_ada, b_ada, w_in, rw_mu, rw_w0, rw_w2, rw_a0, rw_a2, rw_g2,
           rw_k_k, rw_k_a, rw_r_k, rw_ln_w, rw_ln_b, fox_b_f, hg_lb_logits, hg_norm_g, w_out, norm2_g,
           w_ffn_in, w_ffn_out, final_norm_g):
    dt = x_prompt.dtype
    bp, tp, _ = x_prompt.shape
    bs, ts, _ = x_sample.shape

    mod = _modulation(jnp.concatenate([c_prompt, c_sample], axis=0).astype(F32), w_ada, b_ada)
    lbs = _hg_lower_bounds(hg_lb_logits.astype(F32))

    cfg_p = dict(tm=ROW_TILE, tiles_per_batch=tp // ROW_TILE, tq=ROW_TILE, hg_chunk=HG_CHUNK, hg_tile=SCAN_TILE,
                 rw_chunk=RW_CHUNK, rw_tile=SCAN_TILE)
    cfg_s = dict(tm=bs * ts, tiles_per_batch=1, hg_chunk=ts, hg_tile=ts, rw_chunk=ts, rw_tile=ts)

    xp = x_prompt.reshape(bp * tp, D_MODEL).astype(F32)
    xs = x_sample.reshape(bs * ts, D_MODEL).astype(F32)
    zero_prev = jnp.zeros((bp, 1, RW_COLS), F32)
    zero_state = jnp.zeros((bp, RW_WIDTH, RW_WIDTH), F32)
    final_g = final_norm_g.reshape(1, D_MODEL)
    row = lambda a: a.reshape(1, -1)

    kv_p = (jnp.zeros((DEPTH, bp, FOX_WIDTH, tp), F32), jnp.zeros((DEPTH, bp, FOX_WIDTH, tp), F32))
    outs_p, outs_s = [], []
    for l in range(DEPTH):
        lw = _layer_weights(l, w_in, rw_w2, rw_a2, rw_g2, w_out, w_ffn_in, w_ffn_out)
        zero_row = jnp.zeros((RW_WIDTH,), F32)
        lp = dict(
            norm1_g=row(norm1_g[l]), norm2_g=row(norm2_g[l]), fox_b_f=fox_b_f[l],
            hg_lb=row(lbs[l]), hg_norm_g=row(hg_norm_g[l]), rw_mu=row(rw_mu[l]),
            rw_vecs=jnp.stack([rw_w0[l], rw_a0[l], rw_k_k[l], rw_k_a[l], rw_r_k[l].reshape(-1),
                               rw_ln_w[l], rw_ln_b[l], zero_row], axis=0),
        )
        chunks_p = [m[:, None, :] for m in jnp.split(mod[l, :bp], 6, axis=-1)]
        chunks_s = [jnp.repeat(m, ts, axis=0)[None] for m in jnp.split(mod[l, bp:], 6, axis=-1)]
        last = l == DEPTH - 1
        xp, kv_p, st_p = _group_layer(xp, chunks_p, lw, lp, l, kv_p, zero_prev, zero_state, zero_state, None,
                                      bp, tp, cfg_p, final_g, last)
        xs, _, st_s = _group_layer(xs, chunks_s, lw, lp, l, None, state_rwkv_shift[l].astype(F32),
                                _block_diag(state_rwkv[l].astype(F32)),
                                _block_diag_t(state_hgrn[l].astype(F32)),
                                (cache_fox_k.astype(F32), cache_fox_v.astype(F32), cache_fox_logf[l].astype(F32)),
                                bs, ts, cfg_s, final_g, last)
        outs_p.append(st_p)
        outs_s.append(st_s)

    stk = lambda outs, j: jnp.stack([o[j] for o in outs], axis=0).astype(dt)
    y_prompt = xp.reshape(bp, tp, D_MODEL).astype(dt)
    y_sample = xs.reshape(bs, ts, D_MODEL).astype(dt)
    unflip = lambda a: jnp.transpose(a.reshape(DEPTH, bp, FOX_HEADS, HEAD_DIM, tp), (0, 1, 4, 2, 3))
    return (y_prompt, y_sample,
            unflip(kv_p[0]).astype(dt), unflip(kv_p[1]).astype(dt), stk(outs_p, 2), stk(outs_p, 3), stk(outs_p, 4), stk(outs_p, 5),
            stk(outs_s, 0), stk(outs_s, 1), stk(outs_s, 2), stk(outs_s, 3), stk(outs_s, 4), stk(outs_s, 5))
```

```python
import functools

import numpy as np
import jax
import jax.numpy as jnp
from jax import lax
from jax.experimental import pallas as pl
from jax.experimental.pallas import tpu as pltpu

F32 = jnp.float32
BF16 = jnp.bfloat16
HIGHEST = lax.Precision.HIGHEST

D_MODEL = 1024
DEPTH = 4
HEAD_DIM = 64
RW_HEADS = 4
RW_WIDTH = RW_HEADS * HEAD_DIM
FOX_HEADS = 8
FOX_WIDTH = FOX_HEADS * HEAD_DIM
HG_HEADS = 4
HG_WIDTH = HG_HEADS * HEAD_DIM
RW_W_LORA = 32
RW_A_LORA = 32
RW_G_LORA = 64
RW_LORA = RW_W_LORA + RW_A_LORA + RW_G_LORA
RW_COLS = 3 * RW_WIDTH + RW_LORA
FOX_COLS = 3 * FOX_WIDTH + FOX_HEADS
HG_COLS = 4 * HG_WIDTH
D_FF = 2816
EPS = 1e-6
RW_GN_EPS = 64e-5
FOX_SCALE = HEAD_DIM ** -0.5
HG_CHUNK = 64
RW_CHUNK = 64

VMEM_LIMIT_BYTES = 56 * 1024 * 1024
SUBLANES, LANES = 8, 128
ROW_TILE = 512
SCAN_TILE = 1024
MOD_COL_TILE = 1536
CUMSUM_TILE = 512

NT_DIMS = (((1,), (1,)), ((), ()))
TN_DIMS = (((0,), (0,)), ((), ()))
NN_DIMS = (((1,), (0,)), ((), ()))


def _params(*sem):
    return pltpu.CompilerParams(dimension_semantics=sem, vmem_limit_bytes=VMEM_LIMIT_BYTES)


def _const_spec(shape):
    nd = len(shape)
    return pl.BlockSpec(shape, lambda *_: (0,) * nd, pipeline_mode=pl.Buffered(1))


def _mm(a, b, dims=NN_DIMS):
    return lax.dot_general(a.astype(BF16), b.astype(BF16), dims, preferred_element_type=F32)


def _split(a):
    hi = a.astype(BF16)
    lo = (a - hi.astype(F32)).astype(BF16)
    return hi, lo


def _mm3(a, b, dims=NN_DIMS):
    ah, al = _split(a)
    bh, bl = _split(b)
    d = lambda x, y: lax.dot_general(x, y, dims, preferred_element_type=F32)
    return d(ah, bh) + (d(ah, bl) + d(al, bh))


def _head_sum(x, ones_bd):
    hi, lo = _split(x)
    d = lambda u: jnp.dot(u, ones_bd, preferred_element_type=F32)
    return d(hi) + d(lo)


def _block_ones_np(width):
    r = np.arange(width) // HEAD_DIM
    return (r[:, None] == r[None, :]).astype(np.float32)


def _log_sigmoid(z):
    return jnp.minimum(z, 0.0) - jnp.log1p(jnp.exp(-jnp.abs(z)))


def _sigmoid(z):
    return 1.0 / (1.0 + jnp.exp(-z))


def _mod_kernel(c_ref, w_ref, b_ref, o_ref):
    c = c_ref[...]
    s = c * _sigmoid(c)
    o_ref[0] = jnp.dot(s, w_ref[0], precision=HIGHEST, preferred_element_type=F32) + b_ref[0]


def _modulation(c_all, w_ada, b_ada):
    nb = c_all.shape[0]
    tn = MOD_COL_TILE
    n_out = w_ada.shape[2]
    return pl.pallas_call(
        _mod_kernel,
        grid=(DEPTH, n_out // tn),
        in_specs=[
            pl.BlockSpec((nb, D_MODEL), lambda l, j: (0, 0)),
            pl.BlockSpec((1, D_MODEL, tn), lambda l, j: (l, 0, j)),
            pl.BlockSpec((1, 1, tn), lambda l, j: (l, 0, j)),
        ],
        out_specs=pl.BlockSpec((1, nb, tn), lambda l, j: (l, 0, j)),
        out_shape=jax.ShapeDtypeStruct((DEPTH, nb, n_out), F32),
        compiler_params=_params("arbitrary", "arbitrary"),
        name="adaln_mod",
    )(c_all, w_ada, b_ada.reshape(DEPTH, 1, n_out))


def _lbs_kernel(x_ref, o_ref):
    x = x_ref[...]
    m = jnp.max(x, axis=0, keepdims=True)
    e = jnp.exp(x - m)
    p = e / jnp.sum(e, axis=0, keepdims=True)
    acc = jnp.zeros_like(p[0:1])
    rows = []
    for l in range(DEPTH):
        acc = acc + p[l:l + 1]
        rows.append(acc)
    cs = jnp.concatenate(rows, axis=0)
    o_ref[...] = cs - cs[0:1]


def _hg_lower_bounds(logits):
    return pl.pallas_call(
        _lbs_kernel,
        out_shape=jax.ShapeDtypeStruct(logits.shape, F32),
        name="hgrn_lower_bounds",
    )(logits)


def _normproj_kernel(x_ref, sh_ref, sc_ref, g_ref, wrw_ref, wq_ref, wkv_ref, wkvt_ref, wft_ref, whg_ref, *refs,
                     transposed_kv):
    prw_ref, q_ref, k_ref, v_ref, kt_ref, vt_ref, ft_ref, phg_ref = refs[-8:]
    x = x_ref[...]
    ms = jnp.mean(x * x, axis=-1, keepdims=True)
    h = x * lax.rsqrt(ms + EPS) * g_ref[...]
    h = h * (1.0 + sc_ref[0]) + sh_ref[0]
    hb = h.astype(BF16)
    prw_ref[...] = jnp.dot(hb, wrw_ref[...], preferred_element_type=F32)
    q_ref[...] = jnp.dot(hb, wq_ref[...], preferred_element_type=F32)
    if transposed_kv:
        kvt = lax.dot_general(wkvt_ref[...], hb, NT_DIMS, preferred_element_type=F32)
        kt_ref[...] = kvt[:FOX_WIDTH]
        vt_ref[...] = kvt[FOX_WIDTH:]
        k_ref[...] = jnp.zeros(k_ref.shape, F32)
        v_ref[...] = jnp.zeros(v_ref.shape, F32)
    else:
        kv = jnp.dot(hb, wkv_ref[...], preferred_element_type=F32)
        k_ref[...] = kv[:, :FOX_WIDTH]
        v_ref[...] = kv[:, FOX_WIDTH:]
        kt_ref[...] = jnp.zeros(kt_ref.shape, F32)
        vt_ref[...] = jnp.zeros(vt_ref.shape, F32)
    ft_ref[0] = lax.dot_general(wft_ref[...], hb, NT_DIMS, preferred_element_type=F32)
    phg_ref[...] = jnp.dot(hb, whg_ref[...], preferred_element_type=F32)


def _mod_spec(arr, tiles_per_batch):
    _, r, d = arr.shape
    return pl.BlockSpec((1, r, d), lambda i: (i // tiles_per_batch, 0, 0))


def _normproj(x, sh, sc, g, wts, kv_stack, layer, tm, tiles_per_batch):
    n = x.shape[0]
    transposed_kv = kv_stack is not None
    wrw, wq, wkv, wkvt, wft, whg = wts
    row = lambda w: pl.BlockSpec((tm, w), lambda i: (i, 0))
    dummy = pl.BlockSpec((SUBLANES, LANES), lambda i: (0, 0))
    dummy_shape = jax.ShapeDtypeStruct((SUBLANES, LANES), F32)
    nbat = n // (tm * tiles_per_batch)
    plain = jax.ShapeDtypeStruct((n, FOX_WIDTH), F32)
    flipped = jax.ShapeDtypeStruct((DEPTH, nbat, FOX_WIDTH, tm * tiles_per_batch), F32)
    flipped_spec = pl.BlockSpec((None, None, FOX_WIDTH, tm),
                                lambda i: (layer, i // tiles_per_batch, 0, i % tiles_per_batch))
    stack_args = list(kv_stack) if transposed_kv else []
    stack_specs = [pl.BlockSpec(memory_space=pl.ANY)] * len(stack_args)
    kv_specs = [dummy, dummy, flipped_spec, flipped_spec] if transposed_kv else [row(FOX_WIDTH)] * 2 + [dummy] * 2
    kv_shapes = [dummy_shape, dummy_shape, flipped, flipped] if transposed_kv else [plain, plain, dummy_shape,
                                                                                   dummy_shape]
    return pl.pallas_call(
        functools.partial(_normproj_kernel, transposed_kv=transposed_kv),
        grid=(n // tm,),
        in_specs=[
            row(D_MODEL), _mod_spec(sh, tiles_per_batch), _mod_spec(sc, tiles_per_batch),
            _const_spec((1, D_MODEL)),
            _const_spec(wrw.shape), _const_spec(wq.shape), _const_spec(wkv.shape), _const_spec(wkvt.shape),
            _const_spec(wft.shape), _const_spec(whg.shape),
        ] + stack_specs,
        out_specs=[row(RW_COLS), row(FOX_WIDTH)] + kv_specs + [
            pl.BlockSpec((1, FOX_HEADS, tm), lambda i: (i // tiles_per_batch, 0, i % tiles_per_batch)),
            row(HG_COLS)],
        out_shape=[jax.ShapeDtypeStruct((n, RW_COLS), F32), plain] + kv_shapes + [
            jax.ShapeDtypeStruct((nbat, FOX_HEADS, tm * tiles_per_batch), F32),
            jax.ShapeDtypeStruct((n, HG_COLS), F32)],
        input_output_aliases={10: 4, 11: 5} if transposed_kv else {},
        compiler_params=_params("arbitrary"),
        name="normproj",
    )(x, sh, sc, g, wrw, wq, wkv, wkvt, wft, whg, *stack_args)


def _cumsum_lanes(x, carry):
    length = x.shape[1]
    step = min(CUMSUM_TILE, length)
    r = lax.broadcasted_iota(jnp.int32, (step, step), 0)
    c = lax.broadcasted_iota(jnp.int32, (step, step), 1)
    upper = (r <= c).astype(BF16)
    outs = []
    for j in range(length // step):
        hi, lo = _split(x[:, j * step:(j + 1) * step])
        cs = (jnp.dot(hi, upper, preferred_element_type=F32) + jnp.dot(lo, upper, preferred_element_type=F32)) + carry
        carry = cs[:, step - 1:step]
        outs.append(cs)
    return outs, carry, step


def _gate_kernel(*refs, past):
    if past:
        f_ref, bf_ref, cache_ref, logf_ref, c_ref = refs
    else:
        f_ref, bf_ref, logf_ref, c_ref = refs
    logf = _log_sigmoid(f_ref[0] + bf_ref[...])
    logf_ref[0] = logf
    carry = jnp.zeros((FOX_HEADS, 1), F32)
    if past:
        outs, carry, step = _cumsum_lanes(cache_ref[0], carry)
        for j, cs in enumerate(outs):
            c_ref[0, :, j * step:(j + 1) * step] = cs
    outs, carry, step = _cumsum_lanes(logf, carry)
    for j, cs in enumerate(outs):
        c_ref[0, :, past + j * step:past + (j + 1) * step] = cs


def _gate(f_bht, b_f, cache_logf_bhp):
    nb, _, t = f_bht.shape
    past = 0 if cache_logf_bhp is None else cache_logf_bhp.shape[2]
    blk = lambda w: pl.BlockSpec((1, FOX_HEADS, w), lambda b: (b, 0, 0))
    in_specs = [blk(t), _const_spec((FOX_HEADS, 1))]
    args = [f_bht, b_f.reshape(FOX_HEADS, 1)]
    if past:
        in_specs.append(blk(past))
        args.append(cache_logf_bhp)
    return pl.pallas_call(
        functools.partial(_gate_kernel, past=past),
        grid=(nb,),
        in_specs=in_specs,
        out_specs=[blk(t), blk(past + t)],
        out_shape=[jax.ShapeDtypeStruct((nb, FOX_HEADS, t), F32),
                   jax.ShapeDtypeStruct((nb, FOX_HEADS, past + t), F32)],
        compiler_params=_params("arbitrary"),
        name="fox_gate",
    )(*args)


LOG2E = 1.4426950408889634
BIAS_ROWS = 16
DIAG_SPLIT = 4
SKIP_MARGIN = 137.0


def _fox_kernel(q_ref, kt_ref, vt_ref, c_ref, o_ref, kaug_scr, vaug_scr, kmax_scr, bmax_scr, *, tq):
    qi = pl.program_id(2)
    hd = HEAD_DIM
    nq = kaug_scr.shape[2] // tq
    lane = lax.broadcasted_iota(jnp.int32, (1, 2 * hd), 1)
    bias_lane0 = (hd, 0)
    ones_lane = (hd, 0)

    @pl.when(qi == 0)
    def _():
        kt = kt_ref[...]
        v = vt_ref[...].T
        t = kt.shape[1]
        for hh in range(2):
            c2 = c_ref[0, hh] * LOG2E
            hi = c2.astype(BF16).astype(F32)
            mid = (c2 - hi).astype(BF16).astype(F32)
            lo = c2 - hi - mid
            bias = jnp.concatenate([-hi, -mid, -lo, jnp.zeros((BIAS_ROWS - 3, t), F32)], axis=0).astype(BF16)
            pad = jnp.zeros((hd - BIAS_ROWS, t), BF16)
            kh = kt[hh * hd:(hh + 1) * hd].astype(BF16)
            if hh == 0:
                kaug_scr[hh] = jnp.concatenate([kh, bias, pad], axis=0)
            else:
                kaug_scr[hh] = jnp.concatenate([bias, pad, kh], axis=0)
            own = (lane // hd) == hh
            vaug_scr[hh] = jnp.where(own, v, jnp.where(lane == ones_lane[hh], 1.0, 0.0)).astype(BF16)
            kf = kh.astype(F32)
            knorm = jnp.sqrt(jnp.sum(kf * kf, axis=0, keepdims=True))
            bsum = -(hi + mid + lo.astype(BF16).astype(F32))
            for j in range(nq):
                kmax_scr[hh, j] = jnp.max(knorm[:, j * tq:(j + 1) * tq])
                bmax_scr[hh, j] = jnp.max(bsum[:, j * tq:(j + 1) * tq])

    q = q_ref[...] * (FOX_SCALE * LOG2E)
    qa = []
    for hh in range(2):
        own = (lane // hd) == hh
        is_bias = (lane >= bias_lane0[hh]) & (lane < bias_lane0[hh] + 3)
        qa.append(jnp.where(own, q, jnp.where(is_bias, 1.0, 0.0)).astype(BF16))

    def scores(qh, hh, start, nk):
        return jnp.dot(qh, kaug_scr[hh, :, pl.ds(start, nk)], preferred_element_type=F32)

    def update(s, hh, start, m, acc, row0):
        if row0 is not None:
            rows_i = lax.broadcasted_iota(jnp.int32, s.shape, 0) + row0
            cols_i = lax.broadcasted_iota(jnp.int32, s.shape, 1)
            s = jnp.where(cols_i <= rows_i, s, -jnp.inf)
        width = 2 * hd
        part = s
        if s.shape[1] % width == 0:
            part = s[:, 0:width]
            for c0 in range(width, s.shape[1], width):
                part = jnp.maximum(part, s[:, c0:c0 + width])
        m_new = jnp.maximum(m, jnp.max(part, axis=-1, keepdims=True))
        alpha = jnp.exp2(m - m_new)
        p = jnp.exp2(s - m_new).astype(BF16)
        acc = alpha * acc + jnp.dot(p, vaug_scr[hh, pl.ds(start, s.shape[1]), :], preferred_element_type=F32)
        return m_new, acc

    def chunk(j, carry):
        start = pl.multiple_of(j * tq, tq)
        sc = [scores(qa[hh], hh, start, tq) for hh in range(2)]
        return tuple(update(sc[hh], hh, start, *carry[hh], None) for hh in range(2))

    start = pl.multiple_of(qi * tq, tq)
    hr = tq // DIAG_SPLIT
    blocks = [(hh, r) for hh in range(2) for r in range(DIAG_SPLIT)]
    sc = [scores(qa[hh][r * hr:(r + 1) * hr], hh, start, (r + 1) * hr) for hh, r in blocks]
    first = [update(s, hh, start, jnp.full((hr, 1), -jnp.inf, F32), jnp.zeros((hr, 2 * hd), F32), r * hr)
             for s, (hh, r) in zip(sc, blocks)]
    carry = tuple((jnp.max(jnp.concatenate([jnp.broadcast_to(m, (hr, 2 * hd))
                                            for m, _ in first[hh * DIAG_SPLIT:(hh + 1) * DIAG_SPLIT]], axis=0),
                           axis=-1, keepdims=True),
                   jnp.concatenate([a for _, a in first[hh * DIAG_SPLIT:(hh + 1) * DIAG_SPLIT]], axis=0))
                  for hh in range(2))

    thresholds, q_norms = [], []
    for hh in range(2):
        qf = jnp.where((lane // hd) == hh, qa[hh].astype(F32), 0.0)
        q_norms.append(jnp.sqrt(jnp.max(jnp.sum(qf * qf, axis=-1, keepdims=True))))
        thresholds.append(jnp.min(carry[hh][0]) - SKIP_MARGIN)
    first_j = []
    for hh in range(2):
        fj = jnp.int32(0)
        for j in range(nq - 1):
            skippable = (j < qi) & (q_norms[hh] * kmax_scr[hh, j] + bmax_scr[hh, j] < thresholds[hh])
            fj = jnp.where(skippable & (fj == j), j + 1, fj)
        first_j.append(fj)
    shared_j = jnp.maximum(first_j[0], first_j[1])

    carry = list(carry)
    for hh in range(2):
        def alone(j, c, hh=hh):
            start = pl.multiple_of(j * tq, tq)
            return update(scores(qa[hh], hh, start, tq), hh, start, *c, None)
        carry[hh] = lax.fori_loop(first_j[hh], shared_j, alone, carry[hh])
    carry = lax.fori_loop(shared_j, qi, chunk, tuple(carry))
    outs = []
    for hh in range(2):
        acc = carry[hh][1]
        outs.append(acc / acc[:, ones_lane[hh]:ones_lane[hh] + 1])
    o_ref[...] = jnp.where(lane < hd, outs[0], outs[1])


def _fox(q, kt_stack, vt_stack, layer, c_bht, nb, t, tq):
    n = q.shape[0]
    nq = t // tq
    hw = 2 * HEAD_DIM
    return pl.pallas_call(
        functools.partial(_fox_kernel, tq=tq),
        grid=(nb, FOX_HEADS // 2, nq),
        in_specs=[
            pl.BlockSpec((tq, hw), lambda b, h, i: (b * nq + i, h)),
            pl.BlockSpec((None, None, hw, t), lambda b, h, i: (layer, b, h, 0)),
            pl.BlockSpec((None, None, hw, t), lambda b, h, i: (layer, b, h, 0)),
            pl.BlockSpec((1, 2, 1, t), lambda b, h, i: (b, h, 0, 0)),
        ],
        out_specs=pl.BlockSpec((tq, hw), lambda b, h, i: (b * nq + i, h)),
        out_shape=jax.ShapeDtypeStruct((n, FOX_WIDTH), F32),
        scratch_shapes=[pltpu.VMEM((2, hw, t), BF16), pltpu.VMEM((2, t, hw), BF16),
                        pltpu.SMEM((2, nq), F32), pltpu.SMEM((2, nq), F32)],
        compiler_params=_params("arbitrary", "arbitrary", "arbitrary"),
        name="fox_attention",
    )(q, kt_stack, vt_stack, c_bht.reshape(nb, FOX_HEADS, 1, t))


def _fox_cached_kernel(q_ref, kn_ref, vn_ref, kc_ref, vc_ref, c_ref, o_ref, *, past, t):
    row = lax.broadcasted_iota(jnp.int32, (t, t), 0)
    col = lax.broadcasted_iota(jnp.int32, (t, t), 1)
    q = q_ref[...] * FOX_SCALE
    kn_all = kn_ref[...]
    vn_all = vn_ref[...]
    outs = []
    for h in range(FOX_HEADS):
        lanes = slice(h * HEAD_DIM, (h + 1) * HEAD_DIM)
        qh = q[:, lanes].astype(BF16)
        kc = kc_ref[h * HEAD_DIM:(h + 1) * HEAD_DIM, :].astype(BF16)
        vc = vc_ref[h * HEAD_DIM:(h + 1) * HEAD_DIM, :].astype(BF16)
        kn = kn_all[:, lanes].astype(BF16)
        vn = vn_all[:, lanes].astype(BF16)
        s1 = jnp.dot(qh, kc, preferred_element_type=F32)
        s1 = s1 - c_ref[0, h, :, 0:past]
        s2 = lax.dot_general(qh, kn, NT_DIMS, preferred_element_type=F32)
        s2 = s2 - c_ref[0, h, :, past:past + t]
        s2 = jnp.where(col <= row, s2, -jnp.inf)
        m = jnp.maximum(jnp.max(s1, axis=-1, keepdims=True), jnp.max(s2, axis=-1, keepdims=True))
        p1 = jnp.exp(s1 - m)
        p2 = jnp.exp(s2 - m)
        l = jnp.sum(p1, axis=-1, keepdims=True) + jnp.sum(p2, axis=-1, keepdims=True)
        acc = (lax.dot_general(p1.astype(BF16), vc, NT_DIMS, preferred_element_type=F32)
               + jnp.dot(p2.astype(BF16), vn, preferred_element_type=F32))
        outs.append(acc / l)
    o_ref[...] = jnp.concatenate(outs, axis=1)


def _fox_cached(q, k, v, cache_k, cache_v, layer, c_bhl, nb, t):
    n = q.shape[0]
    past = cache_k.shape[2]
    flip = lambda c: jnp.transpose(c, (0, 1, 3, 4, 2)).reshape(DEPTH, nb, FOX_WIDTH, past)
    cache_k, cache_v = flip(cache_k), flip(cache_v)
    new = pl.BlockSpec((t, FOX_WIDTH), lambda b: (b, 0))
    old = pl.BlockSpec((None, None, FOX_WIDTH, past), lambda b: (layer, b, 0, 0))
    return pl.pallas_call(
        functools.partial(_fox_cached_kernel, past=past, t=t),
        grid=(nb,),
        in_specs=[new, new, new, old, old,
                  pl.BlockSpec((1, FOX_HEADS, 1, past + t), lambda b: (b, 0, 0, 0))],
        out_specs=new,
        out_shape=jax.ShapeDtypeStruct((n, FOX_WIDTH), F32),
        compiler_params=_params("arbitrary"),
        name="fox_attention_cached",
    )(q, k, v, cache_k, cache_v, c_bhl.reshape(nb, FOX_HEADS, 1, past + t))


def _levels(chunk):
    out, m = [], chunk // 2
    while m >= 1:
        out.append(m)
        m //= 2
    return out


FINE_LEVEL = 4


def _decay_sum_matrix(chunk):
    r = np.arange(chunk)[:, None]
    j = np.arange(chunk)[None, :]
    incl = (j <= r).astype(np.float32)
    blocks = [incl]
    for m in _levels(chunk):
        if m <= FINE_LEVEL:
            mid = (r // (2 * m)) * (2 * m) + m - 1
            blocks.append(incl - (j <= mid).astype(np.float32))
    return np.concatenate(blocks, axis=0)


def _level_ids(chunk):
    levels = _levels(chunk)
    t = np.arange(chunk)
    x = t[:, None] ^ t[None, :]
    m = np.where(x > 0, 2 ** np.floor(np.log2(np.maximum(x, 1))).astype(np.int64), 0)
    lvl = np.full((chunk, chunk), len(levels) + 1, np.int32)
    for li, lm in enumerate(levels):
        lvl[(t[:, None] > t[None, :]) & (m == lm)] = li
    lvl[x == 0] = len(levels)
    return lvl


def _level_ids_wide(chunk, heads):
    return np.tile(_level_ids(chunk), (1, heads))


def _two_pass(mat_bf16, x):
    hi, lo = _split(x)
    d = lambda u: jnp.dot(mat_bf16, u, preferred_element_type=F32)
    return d(hi) + d(lo)


def _hgrn_kernel(p_ref, lb_ref, ng_ref, s0_ref, dmat_ref, lvl_ref, ones_ref, o_ref, st_ref, st_scr, *, chunk,
                 nchunks, group):
    ti = pl.program_id(1)

    @pl.when(ti == 0)
    def _():
        st_scr[...] = s0_ref[0]

    w = HG_WIDTH
    levels = _levels(chunk)
    lb = lb_ref[...]
    log_lb = jnp.log(lb)
    log_1m_lb = jnp.log1p(-lb)
    row = lax.broadcasted_iota(jnp.int32, (chunk, 1), 0)
    lane_head = lax.broadcasted_iota(jnp.int32, (1, w), 1) // HEAD_DIM
    row_head = lax.broadcasted_iota(jnp.int32, (w, 1), 0) // HEAD_DIM

    head_lanes = [(lane_head == h).astype(BF16) for h in range(HG_HEADS)]

    def stack(x):
        xb = x.astype(BF16)
        return jnp.concatenate([xb * m for m in head_lanes], axis=0)

    def each(fn, *lists):
        return [fn(*xs) for xs in zip(*lists)]

    def chunk_group(gi, _):
        base = gi * (group * chunk)
        rows = [pl.ds(pl.multiple_of(base + i * chunk, chunk), chunk) for i in range(group)]
        q = [p_ref[r, 0:w] for r in rows]
        fx = [p_ref[r, w:2 * w] for r in rows]
        val = [p_ref[r, 2 * w:3 * w] for r in rows]
        e = each(lambda x: jnp.exp(-jnp.abs(x)), fx)
        bv = each(lambda x, ei: log_1m_lb + (jnp.minimum(x, 0.0) - jnp.log1p(ei)), fx, e)
        logf = each(lambda x: jnp.maximum(log_lb, x) + jnp.log1p(jnp.exp(-jnp.abs(log_lb - x))), bv)
        kd = each(lambda x, ei: (1.0 - lb) * (jnp.where(x >= 0.0, ei, 1.0) / (1.0 + ei)), fx, e)
        sums = each(lambda x: _two_pass(dmat_ref[...], x), logf)
        b = [x[0:chunk] for x in sums]

        lvl = lvl_ref[...]
        attn = each(lambda qi, ki: jnp.where(lvl == len(levels), _mm(qi, stack(ki), NT_DIMS), 0.0), q, kd)
        fine = 0
        for li, m in enumerate(levels):
            if m > FINE_LEVEL:
                def rel(bi):
                    mids = [jnp.broadcast_to(bi[s0 + m - 1:s0 + m, :], (2 * m, w)) for s0 in range(0, chunk, 2 * m)]
                    return bi - jnp.concatenate(mids, axis=0)
                x = each(rel, b)
            else:
                fine += 1
                x = [sm[fine * chunk:(fine + 1) * chunk] for sm in sums]
            second = ((row // m) % 2) == 1
            signed_l2e = jnp.where(second, LOG2E, -LOG2E)
            wgt = each(lambda xi: jnp.exp2(xi * signed_l2e), x)
            ql = each(lambda qi, wi: jnp.where(second, qi * wi, 0.0), q, wgt)
            kl = each(lambda ki, wi: jnp.where(second, 0.0, ki * wi), kd, wgt)
            attn = each(lambda at, qi, ki: jnp.where(lvl == li, _mm(qi, stack(ki), NT_DIMS), at), attn, ql, kl)
        o_intra = each(lambda at, vi: _mm(at, stack(vi)), attn, val)
        q_dec = each(lambda qi, bi: qi * jnp.exp(bi), q, b)
        b_last = [bi[chunk - 1:chunk, :] for bi in b]
        upd = each(lambda vi, ki, bl, bi: _mm(vi, ki * jnp.exp(bl - bi), TN_DIMS), val, kd, b_last, b)

        st = st_scr[...]
        for i in range(group):
            o = _mm(q_dec[i], st, NT_DIMS) + o_intra[i]
            st = jnp.exp(b_last[i]) * st + jnp.where(row_head == lane_head, upd[i], 0.0)
            ms = _head_sum(o * o, ones_ref[...]) * (1.0 / HEAD_DIM)
            g = p_ref[rows[i], 3 * w:4 * w]
            o_ref[rows[i], :] = o * lax.rsqrt(ms + EPS) * ng_ref[...] * (g * _sigmoid(g))
        st_scr[...] = st
        return 0

    lax.fori_loop(0, nchunks // group, chunk_group, 0)

    @pl.when(ti == pl.num_programs(1) - 1)
    def _():
        st_ref[0] = st_scr[...]


def _hgrn(p_hg, lb, norm_g, st0, nb, t, chunk, tile):
    n = p_hg.shape[0]
    nt = t // tile
    w = HG_WIDTH
    dmat = jnp.asarray(_decay_sum_matrix(chunk), BF16)
    lvl = jnp.asarray(_level_ids_wide(chunk, HG_HEADS))
    ones = jnp.asarray(_block_ones_np(w), BF16)
    return pl.pallas_call(
        functools.partial(_hgrn_kernel, chunk=chunk, nchunks=tile // chunk, group=min(4, tile // chunk)),
        grid=(nb, nt),
        in_specs=[
            pl.BlockSpec((tile, HG_COLS), lambda b, i: (b * nt + i, 0)),
            _const_spec((1, w)), _const_spec((1, w)),
            pl.BlockSpec((1, w, w), lambda b, i: (b, 0, 0)),
            _const_spec(dmat.shape), _const_spec(lvl.shape), _const_spec(ones.shape),
        ],
        out_specs=[pl.BlockSpec((tile, w), lambda b, i: (b * nt + i, 0)),
                   pl.BlockSpec((1, w, w), lambda b, i: (b, 0, 0))],
        out_shape=[jax.ShapeDtypeStruct((n, w), F32), jax.ShapeDtypeStruct((nb, w, w), F32)],
        scratch_shapes=[pltpu.VMEM((w, w), F32)],
        compiler_params=_params("arbitrary", "arbitrary"),
        name="hgrn2",
    )(p_hg, lb, norm_g, st0, dmat, lvl, ones)


def _rwkv_kernel(p_ref, prev_ref, s0_ref, mu_ref, vec_ref, w2_ref, a2_ref, g2_ref, ltri_ref, ones_ref,
                 lvlw_ref, y_ref, st_ref, st_scr, prev_scr, *, chunk, nchunks, group):
    ti = pl.program_id(1)

    @pl.when(ti == 0)
    def _():
        st_scr[...] = s0_ref[0]
        prev_scr[...] = prev_ref[0]

    w = RW_WIDTH
    hc = RW_HEADS * chunk
    levels = _levels(chunk)
    nl = len(levels)
    w0 = vec_ref[0:1, :]
    a0 = vec_ref[1:2, :]
    k_k = vec_ref[2:3, :]
    k_a = vec_ref[3:4, :]
    r_k = vec_ref[4:5, :]
    ln_w = vec_ref[5:6, :]
    ln_b = vec_ref[6:7, :]
    row = lax.broadcasted_iota(jnp.int32, (chunk, 1), 0)
    lane_head = lax.broadcasted_iota(jnp.int32, (1, w), 1) // HEAD_DIM
    row_id = lax.broadcasted_iota(jnp.int32, (w, 1), 0)
    lane_id = lax.broadcasted_iota(jnp.int32, (1, w), 1)
    row_head = row_id // HEAD_DIM
    wide_head = lax.broadcasted_iota(jnp.int32, (1, hc), 1) // chunk

    head_lanes = [(lane_head == h).astype(BF16) for h in range(RW_HEADS)]
    head_lanes2 = [jnp.concatenate([m, m], axis=1) for m in head_lanes]
    wide_lanes = [(wide_head == h).astype(BF16) for h in range(RW_HEADS)]

    def stack(x):
        xb = x.astype(BF16)
        masks = head_lanes if x.shape[1] == w else head_lanes2
        return jnp.concatenate([xb * m for m in masks], axis=0)

    def blockdiag(xw):
        xb = xw.astype(BF16)
        return jnp.concatenate([xb * m for m in wide_lanes], axis=0)

    def each(fn, *lists):
        return [fn(*xs) for xs in zip(*lists)]

    def chunk_group(gi, _):
        base = gi * (group * chunk)
        rows = [pl.ds(pl.multiple_of(base + i * chunk, chunk), chunk) for i in range(group)]
        ones_bd = ones_ref[...]
        mu = mu_ref[...]
        ps = [p_ref[r, :] for r in rows]
        prevs = [prev_scr[...]] + [p[chunk - 1:chunk, :] for p in ps[:-1]]
        prev_scr[...] = ps[-1][chunk - 1:chunk, :]
        xs = each(lambda p, pv: p + mu * (jnp.where(row == 0, pv, pltpu.roll(p, 1, 0)) - p), ps, prevs)
        r = [x[:, 0:w] for x in xs]
        k = [x[:, w:2 * w] for x in xs]
        v = [x[:, 2 * w:3 * w] for x in xs]
        tail = [x[:, 3 * w:] for x in xs]
        w_lin = each(lambda t: w0 + _mm3(jnp.tanh(t), w2_ref[...]), tail)
        a = each(lambda t: _sigmoid(a0 + _mm(t, a2_ref[...])), tail)
        g = each(lambda t: _mm(_sigmoid(t), g2_ref[...]), tail)
        logw = each(lambda wl: -jnp.exp(-(jnp.maximum(-wl, 0.0) + jnp.log1p(jnp.exp(-jnp.abs(wl)))) - 0.5), w_lin)
        kk = each(lambda x: x * k_k, k)
        kk = each(lambda x: x / jnp.maximum(jnp.sqrt(_head_sum(x * x, ones_bd)), 1e-12), kk)
        kmod = each(lambda x, ai: x * (1.0 + (ai - 1.0) * k_a), k, a)
        beta = each(lambda x, ai: x * ai, kk, a)

        lg = each(lambda x: _two_pass(ltri_ref[...], x), logw)
        lg_end = [x[chunk - 1:chunk, :] for x in lg]
        a_n = each(lambda x, l, lw: x * jnp.exp(l - lw), kk, lg, logw)
        r_n = each(lambda x, l: x * jnp.exp(l), r, lg)
        inv_g = each(lambda l: jnp.exp(-l), lg)
        to_end = each(lambda le, l: jnp.exp(le - l), lg_end, lg)
        v_s = each(stack, v)

        gram = each(lambda an, rn, be, km, ig: _mm(jnp.concatenate([an, rn], axis=0),
                                                   jnp.concatenate([stack(be * ig), stack(km * ig)], axis=0),
                                                   NT_DIMS), a_n, r_n, beta, kmod, inv_g)
        lvlw = lvlw_ref[...]
        strict = lvlw < nl
        incl = lvlw <= nl
        n_w = [jnp.where(strict, x[0:chunk, 0:hc], 0.0).astype(BF16) for x in gram]
        m_k = [jnp.where(strict, x[0:chunk, hc:2 * hc], 0.0) for x in gram]
        l_b = [jnp.where(incl, x[chunk:2 * chunk, 0:hc], 0.0) for x in gram]
        l_k = [jnp.where(incl, x[chunk:2 * chunk, hc:2 * hc], 0.0) for x in gram]

        t_w = each(lambda n: (jnp.where(lvlw == nl, 1.0, 0.0)
                              - jnp.where(lvlw == nl - 1, n, jnp.zeros_like(n)).astype(F32)), n_w)
        blk = 2
        while blk < chunk:
            li = levels.index(blk)
            level = (lvlw == li).astype(BF16)
            x_w = each(lambda t, n: _mm(t, blockdiag(n * level)), t_w, n_w)
            t_w = each(lambda t, x: t - _mm(x, blockdiag(t)), t_w, x_w)
            blk *= 2

        rhs = each(lambda an, mk, vs: jnp.concatenate([an, _mm(mk, vs)], axis=1), a_n, m_k, v_s)
        x0 = each(lambda t, rh: _mm(t, stack(rh)), t_w, rhs)

        def residual(n, rh, x):
            x_hi, x_lo = _split(x)
            return rh - x - (jnp.dot(n, stack(x_hi), preferred_element_type=F32)
                             + jnp.dot(n, stack(x_lo), preferred_element_type=F32))

        res = each(residual, n_w, rhs, x0)
        sol = each(lambda x, t, rs: x + _mm(t, stack(rs)), x0, t_w, res)
        a_hat = [x[:, 0:w] for x in sol]
        e0 = [x[:, w:2 * w] for x in sol]
        lb_x = each(lambda lb, x: _mm(lb, stack(x)), l_b, sol)
        r_hat = each(lambda rn, lx: rn - lx[:, 0:w], r_n, lb_x)
        y0 = each(lambda lk, vs, lx: _mm(lk, vs) - lx[:, w:2 * w], l_k, v_s, lb_x)
        same_head = row_head == lane_head
        bh = each(lambda be, te: be * te, beta, to_end)
        kh = each(lambda km, te: km * te, kmod, to_end)
        trans = each(lambda ah, b_, le: (jnp.where(row_id == lane_id, jnp.exp(le), 0.0)
                                         - jnp.where(same_head, _mm(ah, b_, TN_DIMS), 0.0)), a_hat, bh, lg_end)
        q0 = each(lambda vi, ei, k_, b_: jnp.where(same_head, _mm(jnp.concatenate([vi, ei], axis=0),
                                                                   jnp.concatenate([k_, -b_], axis=0), TN_DIMS), 0.0),
                  v, e0, kh, bh)

        st = st_scr[...]
        for i in range(group):
            y = _mm(r_hat[i], st, NT_DIMS) + y0[i]
            st = _mm(st, trans[i]) + q0[i]

            inv_n = 1.0 / HEAD_DIM
            mean = _head_sum(y, ones_bd) * inv_n
            yc = y - mean
            var = _head_sum(yc * yc, ones_bd) * inv_n
            yn = yc * lax.rsqrt(var + RW_GN_EPS) * ln_w + ln_b
            bonus = _head_sum(r[i] * kmod[i] * r_k, ones_bd) * v[i]
            y_ref[rows[i], :] = (yn + bonus) * g[i]
        st_scr[...] = st
        return 0

    lax.fori_loop(0, nchunks // group, chunk_group, 0)

    @pl.when(ti == pl.num_programs(1) - 1)
    def _():
        st_ref[0] = st_scr[...]


def _rwkv(p_rw, prev, st0, mu, vecs, w2p, a2p, g2p, nb, t, chunk, tile):
    n = p_rw.shape[0]
    nt = t // tile
    w = RW_WIDTH
    consts = (jnp.asarray(np.tril(np.ones((chunk, chunk), np.float32)), BF16),
              jnp.asarray(_block_ones_np(w), BF16),
              jnp.asarray(_level_ids_wide(chunk, RW_HEADS)))
    return pl.pallas_call(
        functools.partial(_rwkv_kernel, chunk=chunk, nchunks=tile // chunk, group=min(8, tile // chunk)),
        grid=(nb, nt),
        in_specs=[
            pl.BlockSpec((tile, RW_COLS), lambda b, i: (b * nt + i, 0)),
            pl.BlockSpec((1, 1, RW_COLS), lambda b, i: (b, 0, 0)),
            pl.BlockSpec((1, w, w), lambda b, i: (b, 0, 0)),
            _const_spec((1, RW_COLS)), _const_spec((8, w)),
            _const_spec((RW_LORA, w)), _const_spec((RW_LORA, w)), _const_spec((RW_LORA, w)),
        ] + [_const_spec(c.shape) for c in consts],
        out_specs=[pl.BlockSpec((tile, w), lambda b, i: (b * nt + i, 0)),
                   pl.BlockSpec((1, w, w), lambda b, i: (b, 0, 0))],
        out_shape=[jax.ShapeDtypeStruct((n, w), F32), jax.ShapeDtypeStruct((nb, w, w), F32)],
        scratch_shapes=[pltpu.VMEM((w, w), F32), pltpu.VMEM((1, RW_COLS), F32)],
        compiler_params=_params("arbitrary", "arbitrary"),
        name="rwkv7",
    )(p_rw, prev, st0, mu, vecs, w2p, a2p, g2p, *consts)


def _outffn_kernel(yrw_ref, yfox_ref, yhg_ref, x_ref, ga1_ref, sh2_ref, sc2_ref, ga2_ref, g2_ref,
                   worw_ref, wofox_ref, wohg_ref, wg_ref, wu_ref, wo_ref, fg_ref, o_ref, *, final_norm, ff_step):
    mix = (jnp.dot(yrw_ref[...].astype(BF16), worw_ref[...], preferred_element_type=F32)
           + jnp.dot(yfox_ref[...].astype(BF16), wofox_ref[...], preferred_element_type=F32)
           + jnp.dot(yhg_ref[...].astype(BF16), wohg_ref[...], preferred_element_type=F32))
    x1 = x_ref[...] + ga1_ref[0] * mix
    ms = jnp.mean(x1 * x1, axis=-1, keepdims=True)
    h2 = x1 * lax.rsqrt(ms + EPS) * g2_ref[...]
    h2 = (h2 * (1.0 + sc2_ref[0]) + sh2_ref[0]).astype(BF16)
    ffn = jnp.zeros_like(x1)
    for j in range(D_FF // ff_step):
        cols = slice(j * ff_step, (j + 1) * ff_step)
        gate = jnp.dot(h2, wg_ref[:, cols], preferred_element_type=F32)
        up = jnp.dot(h2, wu_ref[:, cols], preferred_element_type=F32)
        act = (gate * _sigmoid(gate) * up).astype(BF16)
        ffn = ffn + jnp.dot(act, wo_ref[cols, :], preferred_element_type=F32)
    x2 = x1 + ga2_ref[0] * ffn
    if final_norm:
        ms2 = jnp.mean(x2 * x2, axis=-1, keepdims=True)
        x2 = x2 * lax.rsqrt(ms2 + EPS) * fg_ref[...]
    o_ref[...] = x2


def _outffn(y_rw, y_fox, y_hg, x, ga1, sh2, sc2, ga2, g2, wts, final_g, final_norm, tm, tiles_per_batch):
    n = x.shape[0]
    row = lambda w: pl.BlockSpec((tm, w), lambda i: (i, 0))
    mods = [_mod_spec(m, tiles_per_batch) for m in (ga1, sh2, sc2, ga2)]
    return pl.pallas_call(
        functools.partial(_outffn_kernel, final_norm=final_norm, ff_step=D_FF // 2),
        grid=(n // tm,),
        in_specs=[row(RW_WIDTH), row(FOX_WIDTH), row(HG_WIDTH), row(D_MODEL)] + mods
        + [_const_spec((1, D_MODEL))] + [_const_spec(w.shape) for w in wts] + [_const_spec((1, D_MODEL))],
        out_specs=row(D_MODEL),
        out_shape=jax.ShapeDtypeStruct((n, D_MODEL), F32),
        compiler_params=_params("arbitrary"),
        name="outffn",
    )(y_rw, y_fox, y_hg, x, ga1, sh2, sc2, ga2, g2, *wts, final_g)


def _block_diag_t(s):
    nb, h, dk, dv = s.shape
    st = jnp.swapaxes(s, 2, 3)
    eye = jnp.eye(h, dtype=s.dtype)
    return jnp.einsum("bhvk,hg->bhvgk", st, eye).reshape(nb, h * dv, h * dk)


def _block_diag(s):
    nb, h, dv, dk = s.shape
    eye = jnp.eye(h, dtype=s.dtype)
    return jnp.einsum("bhvk,hg->bhvgk", s, eye).reshape(nb, h * dv, h * dk)


def _diag_blocks(sbd, h):
    nb, rows, cols = sbd.shape
    x = sbd.reshape(nb, h, rows // h, h, cols // h)
    return jnp.stack([x[:, i, :, i, :] for i in range(h)], axis=1)


def _layer_weights(l, w_in, rw_w2, rw_a2, rw_g2, w_out, w_ffn_in, w_ffn_out):
    wi = w_in[l]
    c0, c1 = RW_COLS, RW_COLS + 3 * FOX_WIDTH
    w_kv = wi[:, c0 + FOX_WIDTH:c1].astype(BF16)
    proj = (wi[:, :c0].astype(BF16), wi[:, c0:c0 + FOX_WIDTH].astype(BF16), w_kv, w_kv.T,
            wi[:, c1:c1 + FOX_HEADS].T.astype(BF16), wi[:, c1 + FOX_HEADS:].astype(BF16))
    zeros = lambda r: jnp.zeros((r, RW_WIDTH), F32)
    w2p = jnp.concatenate([rw_w2[l], zeros(RW_LORA - RW_W_LORA)], axis=0)
    a2p = jnp.concatenate([zeros(RW_W_LORA), rw_a2[l], zeros(RW_G_LORA)], axis=0)
    g2p = jnp.concatenate([zeros(RW_W_LORA + RW_A_LORA), rw_g2[l]], axis=0)
    wo = w_out[l].astype(BF16)
    ffn = (wo[:RW_WIDTH], wo[RW_WIDTH:RW_WIDTH + FOX_WIDTH], wo[RW_WIDTH + FOX_WIDTH:],
           w_ffn_in[l][:, :D_FF].astype(BF16), w_ffn_in[l][:, D_FF:].astype(BF16), w_ffn_out[l].astype(BF16))
    return proj, (w2p, a2p, g2p), ffn


def _group_layer(x, mods, lw, lp, layer, kv_stack, prev, rw_st0, hg_st0, cache, nb, t, cfg, final_g, final_norm):
    proj_w, lora_w, ffn_w = lw
    sh1, sc1, ga1, sh2, sc2, ga2 = mods
    tm, tpb = cfg["tm"], cfg["tiles_per_batch"]
    p_rw, q, k, v, kt, vt, f, p_hg = _normproj(x, sh1, sc1, lp["norm1_g"], proj_w, kv_stack, layer, tm, tpb)
    if f.shape[0] == nb:
        f_bht = f
    else:
        f_bht = jnp.swapaxes(f.reshape(FOX_HEADS, nb, t), 0, 1)
    if cache is None:
        logf_bht, c_bhl = _gate(f_bht, lp["fox_b_f"], None)
        y_fox = _fox(q, kt, vt, layer, c_bhl, nb, t, cfg["tq"])
        kv_stack = (kt, vt)
    else:
        cache_k, cache_v, cache_logf = cache
        logf_bht, c_bhl = _gate(f_bht, lp["fox_b_f"], jnp.swapaxes(cache_logf, 1, 2))
        y_fox = _fox_cached(q, k, v, cache_k, cache_v, layer, c_bhl, nb, t)
    y_hg, hg_st = _hgrn(p_hg, lp["hg_lb"], lp["hg_norm_g"], hg_st0, nb, t, cfg["hg_chunk"], cfg["hg_tile"])
    y_rw, rw_st = _rwkv(p_rw, prev, rw_st0, lp["rw_mu"], lp["rw_vecs"], *lora_w, nb, t,
                        cfg["rw_chunk"], cfg["rw_tile"])
    x = _outffn(y_rw, y_fox, y_hg, x, ga1, sh2, sc2, ga2, lp["norm2_g"], ffn_w, final_g, final_norm, tm, tpb)
    new_kv = (None, None) if cache is None else (k.reshape(nb, t, FOX_HEADS, HEAD_DIM),
                                                  v.reshape(nb, t, FOX_HEADS, HEAD_DIM))
    states = new_kv + (
        jnp.swapaxes(logf_bht, 1, 2),
        _diag_blocks(rw_st, RW_HEADS),
        p_rw.reshape(nb, t, RW_COLS)[:, t - 1:t, :],
        jnp.swapaxes(_diag_blocks(hg_st, HG_HEADS), 2, 3),
    )
    return x, kv_stack, states


def kernel(x_prompt, x_sample, c_prompt, c_sample, cache_fox_k, cache_fox_v, cache_fox_logf, state_rwkv,
           state_rwkv_shift, state_hgrn, norm1_g, w_ada, b_ada, w_in, rw_mu, rw_w0, rw_w2, rw_a0, rw_a2, rw_g2,
           rw_k_k, rw_k_a, rw_r_k, rw_ln_w, rw_ln_b, fox_b_f, hg_lb_logits, hg_norm_g, w_out, norm2_g,
           w_ffn_in, w_ffn_out, final_norm_g):
    dt = x_prompt.dtype
    bp, tp, _ = x_prompt.shape
    bs, ts, _ = x_sample.shape

    mod = _modulation(jnp.concatenate([c_prompt, c_sample], axis=0).astype(F32), w_ada, b_ada)
    lbs = _hg_lower_bounds(hg_lb_logits.astype(F32))

    cfg_p = dict(tm=ROW_TILE, tiles_per_batch=tp // ROW_TILE, tq=ROW_TILE, hg_chunk=HG_CHUNK, hg_tile=SCAN_TILE,
                 rw_chunk=RW_CHUNK, rw_tile=SCAN_TILE)
    cfg_s = dict(tm=bs * ts, tiles_per_batch=1, hg_chunk=ts, hg_tile=ts, rw_chunk=ts, rw_tile=ts)

    xp = x_prompt.reshape(bp * tp, D_MODEL).astype(F32)
    xs = x_sample.reshape(bs * ts, D_MODEL).astype(F32)
    zero_prev = jnp.zeros((bp, 1, RW_COLS), F32)
    zero_state = jnp.zeros((bp, RW_WIDTH, RW_WIDTH), F32)
    final_g = final_norm_g.reshape(1, D_MODEL)
    row = lambda a: a.reshape(1, -1)

    kv_p = (jnp.zeros((DEPTH, bp, FOX_WIDTH, tp), F32), jnp.zeros((DEPTH, bp, FOX_WIDTH, tp), F32))
    outs_p, outs_s = [], []
    for l in range(DEPTH):
        lw = _layer_weights(l, w_in, rw_w2, rw_a2, rw_g2, w_out, w_ffn_in, w_ffn_out)
        zero_row = jnp.zeros((RW_WIDTH,), F32)
        lp = dict(
            norm1_g=row(norm1_g[l]), norm2_g=row(norm2_g[l]), fox_b_f=fox_b_f[l],
            hg_lb=row(lbs[l]), hg_norm_g=row(hg_norm_g[l]), rw_mu=row(rw_mu[l]),
            rw_vecs=jnp.stack([rw_w0[l], rw_a0[l], rw_k_k[l], rw_k_a[l], rw_r_k[l].reshape(-1),
                               rw_ln_w[l], rw_ln_b[l], zero_row], axis=0),
        )
        chunks_p = [m[:, None, :] for m in jnp.split(mod[l, :bp], 6, axis=-1)]
        chunks_s = [jnp.repeat(m, ts, axis=0)[None] for m in jnp.split(mod[l, bp:], 6, axis=-1)]
        last = l == DEPTH - 1
        xp, kv_p, st_p = _group_layer(xp, chunks_p, lw, lp, l, kv_p, zero_prev, zero_state, zero_state, None,
                                      bp, tp, cfg_p, final_g, last)
        xs, _, st_s = _group_layer(xs, chunks_s, lw, lp, l, None, state_rwkv_shift[l].astype(F32),
                                _block_diag(state_rwkv[l].astype(F32)),
                                _block_diag_t(state_hgrn[l].astype(F32)),
                                (cache_fox_k.astype(F32), cache_fox_v.astype(F32), cache_fox_logf[l].astype(F32)),
                                bs, ts, cfg_s, final_g, last)
        outs_p.append(st_p)
        outs_s.append(st_s)

    stk = lambda outs, j: jnp.stack([o[j] for o in outs], axis=0).astype(dt)
    y_prompt = xp.reshape(bp, tp, D_MODEL).astype(dt)
    y_sample = xs.reshape(bs, ts, D_MODEL).astype(dt)
    unflip = lambda a: jnp.transpose(a.reshape(DEPTH, bp, FOX_HEADS, HEAD_DIM, tp), (0, 1, 4, 2, 3))
    return (y_prompt, y_sample,
            unflip(kv_p[0]).astype(dt), unflip(kv_p[1]).astype(dt), stk(outs_p, 2), stk(outs_p, 3), stk(outs_p, 4), stk(outs_p, 5),
            stk(outs_s, 0), stk(outs_s, 1), stk(outs_s, 2), stk(outs_s, 3), stk(outs_s, 4), stk(outs_s, 5))
```
